```python
import jax, jax.numpy as jnp
from jax import lax
import numpy as np

D_MODEL = 1024
BATCH = 8
SEQ = 2048
DEPTH = 4

N_META = 16
GRID_W = 64
CHUNK = 64
META_PAD = CHUNK - N_META

M_HEADS = 4
M_WIDTH = D_MODEL // 2
M_HEAD_DIM = M_WIDTH // M_HEADS
M_CONV = 5
R_HEADS = 4
R_WIDTH = D_MODEL // 2
R_HEAD_DIM = R_WIDTH // R_HEADS
ROPE_BASE = 10000.0
EVEN_IN = 4 * M_WIDTH + 4 * M_HEADS + 4 * R_WIDTH
EVEN_MIX = M_WIDTH + R_WIDTH

NA_HEAD_DIM = 64
NA_HEADS = D_MODEL // NA_HEAD_DIM
NA_WIDTH = NA_HEADS * NA_HEAD_DIM
NA_WIN_ROWS = 8
NA_WIN_COLS = 16
ODD_IN = 3 * NA_WIDTH

N_EXPERTS = 16
EC_CAPACITY_FACTOR = 2
EXPERT_FF = D_MODEL

N_EVEN = (DEPTH + 1) // 2
N_ODD = DEPTH // 2
EPS = 1e-6

kernel_name = 'hybrid_mlstm_retention_natten_ecmoe_encoder'


def rms_norm(x, g):
    xf = x.astype(jnp.float32)
    y = xf * lax.rsqrt(jnp.mean(xf * xf, axis=-1, keepdims=True) + EPS)
    return (y * g.astype(jnp.float32)).astype(x.dtype)


def centred_depthwise_conv(x, w):
    k = w.shape[0]
    return lax.conv_general_dilated(
        x, w[:, None, :].astype(x.dtype), window_strides=(1,),
        padding=[((k - 1) // 2, k // 2)], dimension_numbers=('NWC', 'WIO', 'NWC'),
        feature_group_count=x.shape[-1])


def rotary(x, pos):
    half = x.shape[-1] // 2
    inv = ROPE_BASE ** (-jnp.arange(half, dtype=jnp.float32) / half)
    ang = pos.astype(jnp.float32)[:, None] * inv[None, :]
    cos, sin = jnp.cos(ang)[:, None, :], jnp.sin(ang)[:, None, :]
    x1, x2 = x[..., :half], x[..., half:]
    return jnp.concatenate([x1 * cos - x2 * sin, x2 * cos + x1 * sin], axis=-1)


def to_scan_order(x, reverse):
    meta, real = x[:, :N_META], x[:, N_META:]
    if reverse:
        real = jnp.flip(real, axis=1)
    pad = jnp.zeros((x.shape[0], META_PAD) + x.shape[2:], x.dtype)
    return jnp.concatenate([pad, meta, real], axis=1)


def from_scan_order(y, reverse):
    meta, real = y[:, META_PAD:CHUNK], y[:, CHUNK:]
    if reverse:
        real = jnp.flip(real, axis=1)
    return jnp.concatenate([meta, real], axis=1)


def to_chunks(x):
    b, lp = x.shape[:2]
    x = x.reshape((b, lp // CHUNK, CHUNK) + x.shape[2:])
    return jnp.moveaxis(x, (1, 3), (0, 2))


def from_chunks(y):
    y = jnp.moveaxis(y, (0, 2), (1, 3))
    return y.reshape((y.shape[0], y.shape[1] * y.shape[2]) + y.shape[3:])


def mlstm_scan(q, k, v, log_i, log_f):
    bsz, _, nh, dh = q.shape
    causal = jnp.tril(jnp.ones((CHUNK, CHUNK), bool))

    def step(carry, inp):
        c_st, n_st, m_prev = carry
        qt, kt, vt, it, ft = inp
        b = jnp.cumsum(ft, axis=-1)
        a = it - b
        m = b + jnp.maximum(m_prev[..., None], lax.cummax(a, axis=2))
        log_d = b[..., :, None] + a[..., None, :] - m[..., :, None]
        dmat = jnp.exp(jnp.where(causal, log_d, -jnp.inf))
        w_inter = jnp.exp(b + m_prev[..., None] - m)
        s = jnp.einsum('bhqd,bhkd->bhqk', qt, kt) * dmat
        num = (w_inter[..., None] * jnp.einsum('bhqd,bhde->bhqe', qt, c_st)
               + jnp.einsum('bhqk,bhke->bhqe', s, vt))
        den = w_inter * jnp.einsum('bhqd,bhd->bhq', qt, n_st) + s.sum(-1)
        out = num / jnp.maximum(jnp.abs(den), jnp.exp(-m))[..., None]
        m_new = m[..., -1]
        w_state = jnp.exp(b[..., -1] + m_prev - m_new)
        kw = kt * jnp.exp(b[..., -1:] + a - m_new[..., None])[..., None]
        c_new = w_state[..., None, None] * c_st + jnp.einsum('bhkd,bhke->bhde', kw, vt)
        n_new = w_state[..., None] * n_st + kw.sum(2)
        return (c_new, n_new, m_new), out

    init = (jnp.zeros((bsz, nh, dh, dh), jnp.float32), jnp.zeros((bsz, nh, dh), jnp.float32),
            jnp.zeros((bsz, nh), jnp.float32))
    _, ys = lax.scan(step, init, (to_chunks(q), to_chunks(k), to_chunks(v),
                                  to_chunks(log_i), to_chunks(log_f)))
    return from_chunks(ys)


def retention_scan(q, k, v, log_gamma):
    bsz, _, nh, dh = q.shape
    pos = jnp.arange(CHUNK, dtype=jnp.float32)
    causal = jnp.tril(jnp.ones((CHUNK, CHUNK), bool))
    rel = (pos[:, None] - pos[None, :])[None] * log_gamma[:, None, None]
    decay_intra = jnp.exp(jnp.where(causal[None], rel, -jnp.inf))
    decay_q = jnp.exp((pos + 1.0)[None, :] * log_gamma[:, None])
    decay_k = jnp.exp((CHUNK - 1.0 - pos)[None, :] * log_gamma[:, None])
    decay_chunk = jnp.exp(CHUNK * log_gamma)

    def step(r_st, inp):
        qt, kt, vt = inp
        s = jnp.einsum('bhqd,bhkd->bhqk', qt, kt) * decay_intra
        out = (jnp.einsum('bhqk,bhke->bhqe', s, vt)
               + decay_q[..., None] * jnp.einsum('bhqd,bhde->bhqe', qt, r_st))
        r_new = (decay_chunk[:, None, None] * r_st
                 + jnp.einsum('bhkd,bhke->bhde', kt * decay_k[..., None], vt))
        return r_new, out

    _, ys = lax.scan(step, jnp.zeros((bsz, nh, dh, dh), jnp.float32),
                     (to_chunks(q), to_chunks(k), to_chunks(v)))
    return from_chunks(ys)


def even_mixer(u, w_in, conv_w, gate_b, m_norm_g, ret_decay_logit, r_norm_g, w_out):
    f32 = jnp.float32
    bsz, seq_len, _ = u.shape
    sizes = [M_WIDTH] * 4 + [4 * M_HEADS] + [R_WIDTH] * 4
    cuts = np.cumsum(sizes)[:-1].tolist()
    mq, mk, mv, mo, gates, rq, rk, rv, rg = jnp.split(u @ w_in, cuts, axis=-1)

    def heads(t, nh):
        return t.reshape(bsz, seq_len, nh, -1).astype(f32)

    qk = jax.nn.silu(centred_depthwise_conv(jnp.concatenate([mq, mk], axis=-1), conv_w))
    mq, mk = jnp.split(qk, 2, axis=-1)
    q, k, v = heads(mq, M_HEADS), heads(mk, M_HEADS) * M_HEAD_DIM ** -0.5, heads(mv, M_HEADS)
    g = gates.astype(f32) + gate_b.astype(f32)
    i_fw, f_fw, i_bw, f_bw = jnp.split(g, 4, axis=-1)
    h_m = jnp.zeros_like(v)
    for reverse, ig, fg in ((False, i_fw, f_fw), (True, i_bw, f_bw)):
        o = mlstm_scan(to_scan_order(q, reverse), to_scan_order(k, reverse),
                       to_scan_order(v, reverse), to_scan_order(ig, reverse),
                       to_scan_order(jax.nn.log_sigmoid(fg), reverse))
        h_m = h_m + from_scan_order(o, reverse)
    m_out = (jax.nn.sigmoid(mo.astype(f32))
             * rms_norm(h_m, m_norm_g.reshape(M_HEADS, M_HEAD_DIM)).reshape(bsz, seq_len, M_WIDTH))

    pos = jnp.arange(seq_len)
    rq_h = rotary(heads(rq, R_HEADS), pos)
    rk_h = rotary(heads(rk, R_HEADS), pos) * R_HEAD_DIM ** -0.5
    rv_h = heads(rv, R_HEADS)
    log_gamma = jax.nn.log_sigmoid(ret_decay_logit.astype(f32))
    h_r = jnp.zeros_like(rv_h)
    for d_idx, reverse in ((0, False), (1, True)):
        o = retention_scan(to_scan_order(rq_h, reverse), to_scan_order(rk_h, reverse),
                           to_scan_order(rv_h, reverse), log_gamma[d_idx])
        h_r = h_r + from_scan_order(o, reverse)
    r_out = (jax.nn.silu(rg.astype(f32))
             * rms_norm(h_r, r_norm_g.reshape(R_HEADS, R_HEAD_DIM)).reshape(bsz, seq_len, R_WIDTH))

    mix = jnp.concatenate([m_out, r_out], axis=-1).astype(u.dtype)
    return mix @ w_out


def odd_mixer(u, w_in, q_norm_g, k_norm_g, rpb, w_out):
    f32 = jnp.float32
    bsz, seq_len, _ = u.shape
    n_tok = seq_len - N_META
    rows = n_tok // GRID_W
    wr = min(NA_WIN_ROWS, rows)
    z = (u @ w_in).reshape(bsz, seq_len, 3, NA_HEADS, NA_HEAD_DIM)
    q = rms_norm(z[:, :, 0], q_norm_g) * NA_HEAD_DIM ** -0.5
    k = rms_norm(z[:, :, 1], k_norm_g)
    v = z[:, :, 2]
    qm, km, vm = q[:, :N_META], k[:, :N_META], v[:, :N_META]
    grid = (bsz, rows, GRID_W, NA_HEADS, NA_HEAD_DIM)
    qg, kg, vg = (t[:, N_META:].reshape(grid) for t in (q, k, v))

    col = jnp.arange(GRID_W)
    col_start = jnp.clip(col - NA_WIN_COLS // 2, 0, GRID_W - NA_WIN_COLS)
    col_in = (col[None, :] >= col_start[:, None]) & (col[None, :] < col_start[:, None] + NA_WIN_COLS)
    dc_idx = jnp.clip(col[None, :] - col[:, None], -(NA_WIN_COLS - 1), NA_WIN_COLS - 1) + NA_WIN_COLS - 1
    rpb_cols = rpb.astype(f32)[:, :, dc_idx]

    p_meta = jax.nn.softmax(jnp.einsum('bqhd,bkhd->bhqk', qm, km).astype(f32), axis=-1)
    out_meta = jnp.einsum('bhqk,bkhd->bqhd', p_meta.astype(v.dtype), vm)

    def row_block(r):
        rs = jnp.clip(r - wr // 2, 0, rows - wr)
        q_row = lax.dynamic_index_in_dim(qg, r, axis=1, keepdims=False)
        k_band = lax.dynamic_slice_in_dim(kg, rs, wr, axis=1)
        v_band = lax.dynamic_slice_in_dim(vg, rs, wr, axis=1)
        s_band = jnp.einsum('bqhd,brkhd->bhqrk', q_row, k_band).astype(f32)
        dr_idx = rs + jnp.arange(wr) - r + NA_WIN_ROWS - 1
        bias = jnp.transpose(jnp.take(rpb_cols, dr_idx, axis=1), (0, 2, 1, 3))
        s_band = jnp.where(col_in[:, None, :], s_band + bias, -jnp.inf)
        s_band = s_band.reshape(bsz, NA_HEADS, GRID_W, wr * GRID_W)
        s_meta = jnp.einsum('bqhd,bkhd->bhqk', q_row, km).astype(f32)
        p = jax.nn.softmax(jnp.concatenate([s_meta, s_band], axis=-1), axis=-1).astype(v.dtype)
        v_flat = v_band.reshape(bsz, wr * GRID_W, NA_HEADS, NA_HEAD_DIM)
        return (jnp.einsum('bhqk,bkhd->bqhd', p[..., :N_META], vm)
                + jnp.einsum('bhqk,bkhd->bqhd', p[..., N_META:], v_flat))

    out_rows = lax.map(row_block, jnp.arange(rows))
    out_real = jnp.moveaxis(out_rows, 0, 1).reshape(bsz, n_tok, NA_WIDTH)
    out = jnp.concatenate([out_meta.reshape(bsz, N_META, NA_WIDTH), out_real], axis=1)
    return out @ w_out


def expert_choice_ffn(u, w_router, w_gate, w_up, w_down):
    bsz, seq_len, _ = u.shape
    cap = EC_CAPACITY_FACTOR * seq_len // N_EXPERTS
    aff = jax.nn.softmax((u @ w_router).astype(jnp.float32), axis=-1)
    gate, idx = lax.top_k(jnp.swapaxes(aff, 1, 2), cap)
    bidx = jnp.arange(bsz)[:, None, None]
    xs = u[bidx, idx]
    hdn = jax.nn.silu(jnp.einsum('becd,edf->becf', xs, w_gate)) * jnp.einsum('becd,edf->becf', xs, w_up)
    ys = jnp.einsum('becf,efd->becd', hdn, w_down) * gate[..., None].astype(u.dtype)
    return jnp.zeros_like(u).at[bidx, idx].add(ys)


def setup_inputs(seed: int = 0) -> dict:
    key = jax.random.key(seed)
    ks = iter(jax.random.split(key, 32))
    n = lambda shape: jax.random.normal(next(ks), shape, jnp.float32)
    fb = jnp.linspace(3.0, 6.0, M_HEADS)
    gate_b = jnp.concatenate([0.1 * n((N_EVEN, M_HEADS)), fb + 0.1 * n((N_EVEN, M_HEADS)),
                              0.1 * n((N_EVEN, M_HEADS)), fb + 0.1 * n((N_EVEN, M_HEADS))], axis=-1)
    base_logit = jnp.log(2.0 ** (5.0 + jnp.arange(R_HEADS, dtype=jnp.float32)) - 1.0)
    return {
        'x': n((BATCH, SEQ, D_MODEL)),
        'meta_tokens': n((N_META, D_MODEL)),
        'attn_norm_g': 1.0 + 0.02 * n((DEPTH, D_MODEL)),
        'ffn_norm_g': 1.0 + 0.02 * n((DEPTH, D_MODEL)),
        'even_w_in': n((N_EVEN, D_MODEL, EVEN_IN)) * D_MODEL ** -0.5,
        'even_conv_w': n((N_EVEN, M_CONV, 2 * M_WIDTH)) * M_CONV ** -0.5,
        'even_gate_b': gate_b,
        'even_m_norm_g': 1.0 + 0.02 * n((N_EVEN, M_WIDTH)),
        'even_ret_decay_logit': base_logit + 0.1 * n((N_EVEN, 2, R_HEADS)),
        'even_r_norm_g': 1.0 + 0.02 * n((N_EVEN, R_WIDTH)),
        'even_w_out': n((N_EVEN, EVEN_MIX, D_MODEL)) * EVEN_MIX ** -0.5,
        'odd_w_in': n((N_ODD, D_MODEL, ODD_IN)) * D_MODEL ** -0.5,
        'odd_q_norm_g': 1.0 + 0.02 * n((N_ODD, NA_HEAD_DIM)),
        'odd_k_norm_g': 1.0 + 0.02 * n((N_ODD, NA_HEAD_DIM)),
        'odd_rpb': 0.1 * n((N_ODD, NA_HEADS, 2 * NA_WIN_ROWS - 1, 2 * NA_WIN_COLS - 1)),
        'odd_w_out': n((N_ODD, NA_WIDTH, D_MODEL)) * NA_WIDTH ** -0.5,
        'router_w': n((DEPTH, D_MODEL, N_EXPERTS)) * D_MODEL ** -0.5,
        'expert_w_gate': n((DEPTH, N_EXPERTS, D_MODEL, EXPERT_FF)) * D_MODEL ** -0.5,
        'expert_w_up': n((DEPTH, N_EXPERTS, D_MODEL, EXPERT_FF)) * D_MODEL ** -0.5,
        'expert_w_down': n((DEPTH, N_EXPERTS, EXPERT_FF, D_MODEL)) * EXPERT_FF ** -0.5,
    }


def reference(x, meta_tokens, attn_norm_g, ffn_norm_g, even_w_in, even_conv_w, even_gate_b,
              even_m_norm_g, even_ret_decay_logit, even_r_norm_g, even_w_out, odd_w_in,
              odd_q_norm_g, odd_k_norm_g, odd_rpb, odd_w_out, router_w, expert_w_gate,
              expert_w_up, expert_w_down):
    bsz = x.shape[0]
    meta = jnp.broadcast_to(meta_tokens.astype(x.dtype)[None], (bsz, N_META, x.shape[-1]))
    h = jnp.concatenate([meta, x], axis=1)
    for layer in range(DEPTH):
        j = layer // 2
        u = rms_norm(h, attn_norm_g[layer])
        if layer % 2 == 0:
            h = h + even_mixer(u, even_w_in[j], even_conv_w[j], even_gate_b[j], even_m_norm_g[j],
                               even_ret_decay_logit[j], even_r_norm_g[j], even_w_out[j])
        else:
            h = h + odd_mixer(u, odd_w_in[j], odd_q_norm_g[j], odd_k_norm_g[j], odd_rpb[j], odd_w_out[j])
        h = h + expert_choice_ffn(rms_norm(h, ffn_norm_g[layer]), router_w[layer],
                                  expert_w_gate[layer], expert_w_up[layer], expert_w_down[layer])
    return h[:, N_META:]
```

```python
import functools

import jax
import jax.numpy as jnp
from jax import lax
from jax.experimental import pallas as pl
from jax.experimental.pallas import tpu as pltpu

F32 = jnp.float32
BF16 = jnp.bfloat16

LANES = 128
BF16_ROWS = 16
N_META = 16
GRID_W = 64
EPS = 1e-6
HEAD_DIM = 128
N_HEADS = 4
CONV_K = 5
CONV_PAD = 8
ROPE_BASE = 10000.0
NA_HEAD_DIM = 64
NA_WIN_ROWS = 8
NA_WIN_COLS = 16
N_EXPERTS = 16
CAP_FACTOR = 2
NEG = -1e30
SCAN_T = 128
VMEM_LIMIT = 56 * 1024 * 1024


def _cparams(n_axes):
    return pltpu.CompilerParams(
        dimension_semantics=("arbitrary",) * n_axes, vmem_limit_bytes=VMEM_LIMIT)


def _row_blocks(n_rows):
    for nb in (3, 2, 4, 6, 1):
        if n_rows % (8 * nb) == 0:
            step = n_rows // nb
            return [(i * step, step) for i in range(nb)]
    return [(0, n_rows)]


def _rms(x, g):
    return x * lax.rsqrt(jnp.mean(x * x, axis=-1, keepdims=True) + EPS) * g


def _sigmoid(x):
    return 1.0 / (1.0 + jnp.exp(-x))


def _log_sigmoid(x):
    return jnp.minimum(x, 0.0) - jnp.log(1.0 + jnp.exp(-jnp.abs(x)))


def _dot(a, b):
    return jnp.dot(a, b, preferred_element_type=F32)


def _dot_nt(a, b):
    return lax.dot_general(a, b, (((1,), (1,)), ((), ())), preferred_element_type=F32)


def _dot_tn(a, b):
    return lax.dot_general(a, b, (((0,), (0,)), ((), ())), preferred_element_type=F32)


def _split3(x):
    hi = x.astype(BF16)
    r1 = x - hi.astype(F32)
    mid = r1.astype(BF16)
    lo = (r1 - mid.astype(F32)).astype(BF16)
    return hi, mid, lo


def _tri_prefix(tri_bf16, x):
    hi, mid, lo = _split3(x)
    return _dot(tri_bf16, hi) + _dot(tri_bf16, mid) + _dot(tri_bf16, lo)


def _norm_to_scratch(h_ref, gn_ref, un_ref):
    n_rows = h_ref.shape[1]
    for r0, nr in _row_blocks(n_rows):
        x = h_ref[0, pl.ds(r0, nr), :]
        un_ref[pl.ds(r0, nr), :] = _rms(x, gn_ref[...]).astype(BF16)


def _project(un_ref, w_ref, z_ref):
    n_rows = un_ref.shape[0]
    for r0, nr in _row_blocks(n_rows):
        z_ref[pl.ds(r0, nr), :] = _dot(un_ref[pl.ds(r0, nr), :], w_ref[0])


def _tri_mask(n, lower):
    r = lax.broadcasted_iota(jnp.int32, (n, n), 0)
    c = lax.broadcasted_iota(jnp.int32, (n, n), 1)
    return (c <= r) if lower else (c >= r)


def _mlstm_chunk(qc, kc, vc, a_row, a_col, f_col, mask, state):
    c_st, n_st, g_prev = state
    qb, kb, vb = qc.astype(BF16), kc.astype(BF16), vc.astype(BF16)
    cm = jnp.max(jnp.where(mask, a_row, NEG), axis=-1, keepdims=True)
    g_col = jnp.maximum(g_prev, cm)
    dm = jnp.exp(jnp.where(mask, a_row - g_col, NEG))
    s = _dot_nt(qb, kb) * dm
    w_inter = jnp.exp(g_prev - g_col)
    num = w_inter * _dot(qb, c_st.astype(BF16)) + _dot(s.astype(BF16), vb)
    den = (w_inter * jnp.sum(qc * n_st, axis=-1, keepdims=True)
           + jnp.sum(s, axis=-1, keepdims=True))
    out = num / jnp.maximum(jnp.abs(den), jnp.exp(-(f_col + g_col)))
    g_end = jnp.maximum(g_prev, jnp.max(a_row, axis=-1, keepdims=True))
    kw = kc * jnp.exp(a_col - g_end)
    w_state = jnp.exp(g_prev - g_end)
    c_new = w_state * c_st + _dot_tn(kw.astype(BF16), vb)
    n_new = w_state * n_st + jnp.sum(kw, axis=0, keepdims=True)
    return out, (c_new, n_new, g_end)


def _mlstm_kernel(h_ref, gn_ref, w_ref, conv_ref, gb_ref, ng_ref, out_ref,
                  un_ref, z_ref, zc_ref, acc_ref, acol_ref, fcol_ref, atm_ref, atr_ref):
    n_rows = h_ref.shape[1]
    n_real = n_rows - N_META
    t = SCAN_T
    n_chunks = n_real // t

    @pl.when(pl.program_id(1) == 0)
    def _():
        _norm_to_scratch(h_ref, gn_ref, un_ref)

    _project(un_ref, w_ref, z_ref)

    zc_ref[pl.ds(0, CONV_PAD), :] = jnp.zeros((CONV_PAD, 2 * HEAD_DIM), F32)
    zc_ref[pl.ds(CONV_PAD + n_rows, CONV_PAD), :] = jnp.zeros((CONV_PAD, 2 * HEAD_DIM), F32)
    for r0, nr in _row_blocks(n_rows):
        zc_ref[pl.ds(CONV_PAD + r0, nr), :] = z_ref[pl.ds(r0, nr), 0:2 * HEAD_DIM]
    lane2 = lax.broadcasted_iota(jnp.int32, (1, 2 * HEAD_DIM), 1)
    qk_scale = jnp.where(lane2 >= HEAD_DIM, HEAD_DIM ** -0.5, 1.0).astype(F32)
    for r0, nr in _row_blocks(n_rows):
        acc = jnp.zeros((nr, 2 * HEAD_DIM), F32)
        for j in range(CONV_K):
            off = CONV_PAD - (CONV_K - 1) // 2 + j + r0
            acc = acc + zc_ref[pl.ds(off, nr), :] * conv_ref[0, pl.ds(j, 1), :]
        z_ref[pl.ds(r0, nr), 0:2 * HEAD_DIM] = acc * _sigmoid(acc) * qk_scale

    gi_off, gf_off = 4 * HEAD_DIM, 5 * HEAD_DIM
    bias_i = gb_ref[0, :, 0:LANES]
    bias_f = gb_ref[0, :, LANES:2 * LANES]
    lane = lax.broadcasted_iota(jnp.int32, (1, LANES), 1)
    tri_m = jnp.where(_tri_mask(N_META, True), 1.0, 0.0).astype(BF16)
    lf_m = _log_sigmoid(z_ref[pl.ds(0, N_META), gf_off:gf_off + LANES] + bias_f)
    f_meta = _tri_prefix(tri_m, lf_m)
    f_meta_end = f_meta[N_META - 1:N_META, :]
    tri_b = jnp.where(_tri_mask(LANES, True), 1.0, 0.0).astype(BF16)

    def prefix_body(c, carry):
        r0 = pl.multiple_of(N_META + c * LANES, 8)
        lf = _log_sigmoid(z_ref[pl.ds(r0, LANES), gf_off:gf_off + LANES] + bias_f)
        p = _tri_prefix(tri_b, lf) + carry
        fcol_ref[pl.ds(r0, LANES), :] = p
        return p[LANES - 1:LANES, :]

    total = lax.fori_loop(0, n_real // LANES, prefix_body, jnp.zeros((1, LANES), F32))

    fcol_ref[pl.ds(0, N_META), :] = f_meta
    acol_ref[pl.ds(0, N_META), :] = z_ref[pl.ds(0, N_META), gi_off:gi_off + LANES] + bias_i - f_meta

    def finish_body(c, carry):
        r0 = pl.multiple_of(N_META + c * LANES, 8)
        lf = _log_sigmoid(z_ref[pl.ds(r0, LANES), gf_off:gf_off + LANES] + bias_f)
        p = fcol_ref[pl.ds(r0, LANES), :]
        f = f_meta_end + jnp.where(lane == 0, p, total - p + lf)
        fcol_ref[pl.ds(r0, LANES), :] = f
        a = z_ref[pl.ds(r0, LANES), gi_off:gi_off + LANES] + bias_i - f
        acol_ref[pl.ds(r0, LANES), :] = a
        atr_ref[:, pl.ds(pl.multiple_of(c * LANES, LANES), LANES)] = a.T[0:8, :]
        return carry

    lax.fori_loop(0, n_real // LANES, finish_body, 0)
    atm_ref[...] = acol_ref[pl.ds(0, LANES), :].T[0:8, :]

    mask_m = _tri_mask(N_META, True)
    for d in (0, 1):
        mask = _tri_mask(t, d == 0)
        state = (jnp.zeros((HEAD_DIM, HEAD_DIM), F32), jnp.zeros((1, HEAD_DIM), F32),
                 jnp.zeros((1, 1), F32))
        out_m, state = _mlstm_chunk(
            z_ref[pl.ds(0, N_META), 0:HEAD_DIM], z_ref[pl.ds(0, N_META), HEAD_DIM:2 * HEAD_DIM],
            z_ref[pl.ds(0, N_META), 2 * HEAD_DIM:3 * HEAD_DIM],
            atm_ref[pl.ds(d, 1), 0:N_META], acol_ref[pl.ds(0, N_META), d:d + 1],
            fcol_ref[pl.ds(0, N_META), d:d + 1], mask_m, state)
        if d == 0:
            acc_ref[pl.ds(0, N_META), :] = out_m
        else:
            acc_ref[pl.ds(0, N_META), :] += out_m

        def body(i, st, d=d, mask=mask):
            c = i if d == 0 else n_chunks - 1 - i
            r0 = pl.multiple_of(N_META + c * t, 8)
            l0 = pl.multiple_of(c * t, LANES)
            out, st = _mlstm_chunk(
                z_ref[pl.ds(r0, t), 0:HEAD_DIM], z_ref[pl.ds(r0, t), HEAD_DIM:2 * HEAD_DIM],
                z_ref[pl.ds(r0, t), 2 * HEAD_DIM:3 * HEAD_DIM],
                atr_ref[pl.ds(d, 1), pl.ds(l0, t)], acol_ref[pl.ds(r0, t), d:d + 1],
                fcol_ref[pl.ds(r0, t), d:d + 1], mask, st)
            if d == 0:
                acc_ref[pl.ds(r0, t), :] = out
            else:
                acc_ref[pl.ds(r0, t), :] += out
            return st

        lax.fori_loop(0, n_chunks, body, state)

    for r0, nr in _row_blocks(n_rows):
        hs = acc_ref[pl.ds(r0, nr), :]
        o = z_ref[pl.ds(r0, nr), 3 * HEAD_DIM:4 * HEAD_DIM]
        out_ref[0, pl.ds(r0, nr), :] = (_sigmoid(o) * _rms(hs, ng_ref[0])).astype(BF16)


def _mlstm_call(h, gn, w, conv, gb, ng):
    bsz, n_rows, d = h.shape
    nw = w.shape[-1]
    n_real = n_rows - N_META
    return pl.pallas_call(
        _mlstm_kernel,
        grid=(bsz, N_HEADS),
        in_specs=[
            pl.BlockSpec((1, n_rows, d), lambda b, hd: (b, 0, 0)),
            pl.BlockSpec((1, d), lambda b, hd: (0, 0)),
            pl.BlockSpec((1, d, nw), lambda b, hd: (hd, 0, 0)),
            pl.BlockSpec((1, CONV_K, 2 * HEAD_DIM), lambda b, hd: (hd, 0, 0)),
            pl.BlockSpec((1, 1, 2 * LANES), lambda b, hd: (hd, 0, 0)),
            pl.BlockSpec((1, 1, HEAD_DIM), lambda b, hd: (hd, 0, 0)),
        ],
        out_specs=pl.BlockSpec((1, n_rows, HEAD_DIM), lambda b, hd: (b, 0, hd)),
        out_shape=jax.ShapeDtypeStruct((bsz, n_rows, N_HEADS * HEAD_DIM), BF16),
        scratch_shapes=[
            pltpu.VMEM((n_rows, d), BF16),
            pltpu.VMEM((n_rows, nw), F32),
            pltpu.VMEM((n_rows + 2 * CONV_PAD, 2 * HEAD_DIM), F32),
            pltpu.VMEM((n_rows, HEAD_DIM), F32),
            pltpu.VMEM((n_rows, LANES), F32),
            pltpu.VMEM((n_rows, LANES), F32),
            pltpu.VMEM((8, LANES), F32),
            pltpu.VMEM((8, n_real), F32),
        ],
        compiler_params=_cparams(2),
        name="mlstm_heads",
    )(h, gn, w, conv, gb, ng)


def _ret_chunk(qc, kc, vc, dmat, dq, dk, dchunk, r_st):
    qb, kb, vb = qc.astype(BF16), kc.astype(BF16), vc.astype(BF16)
    s = _dot_nt(qb, kb) * dmat
    out = _dot(s.astype(BF16), vb) + dq * _dot(qb, r_st.astype(BF16))
    r_new = dchunk * r_st + _dot_tn((kc * dk).astype(BF16), vb)
    return out, r_new


def _decay_tables(n, lg, forward):
    r = lax.broadcasted_iota(jnp.int32, (n, n), 0)
    c = lax.broadcasted_iota(jnp.int32, (n, n), 1)
    pos = lax.broadcasted_iota(jnp.int32, (n, 1), 0).astype(F32)
    if forward:
        dist, mask = (r - c).astype(F32), c <= r
        dq, dk = jnp.exp((pos + 1.0) * lg), jnp.exp((n - 1.0 - pos) * lg)
    else:
        dist, mask = (c - r).astype(F32), c >= r
        dq, dk = jnp.exp((n - pos) * lg), jnp.exp(pos * lg)
    dmat = jnp.exp(jnp.where(mask, dist * lg, NEG))
    return dmat, dq, dk, jnp.exp(n * lg)


def _ret_kernel(h_ref, gn_ref, w_ref, cos_ref, sin_ref, dl_ref, ng_ref, out_ref,
                un_ref, z_ref, acc_ref):
    n_rows = h_ref.shape[1]
    n_real = n_rows - N_META
    t = SCAN_T
    n_chunks = n_real // t

    @pl.when(pl.program_id(1) == 0)
    def _():
        _norm_to_scratch(h_ref, gn_ref, un_ref)

    _project(un_ref, w_ref, z_ref)

    for r0, nr in _row_blocks(n_rows):
        cs, sn = cos_ref[pl.ds(r0, nr), :], sin_ref[pl.ds(r0, nr), :]
        q = z_ref[pl.ds(r0, nr), 0:HEAD_DIM]
        z_ref[pl.ds(r0, nr), 0:HEAD_DIM] = q * cs + pltpu.roll(q, HEAD_DIM // 2, 1) * sn
        k = z_ref[pl.ds(r0, nr), HEAD_DIM:2 * HEAD_DIM]
        z_ref[pl.ds(r0, nr), HEAD_DIM:2 * HEAD_DIM] = (
            (k * cs + pltpu.roll(k, HEAD_DIM // 2, 1) * sn) * HEAD_DIM ** -0.5)

    lg_all = _log_sigmoid(dl_ref[0])
    for d in (0, 1):
        lg = lg_all[:, d:d + 1]
        dmat_m, _, dk_m, dch_m = _decay_tables(N_META, lg, True)
        dmat, dq, dk, dch = _decay_tables(t, lg, d == 0)
        r_st = jnp.zeros((HEAD_DIM, HEAD_DIM), F32)
        out_m, r_st = _ret_chunk(
            z_ref[pl.ds(0, N_META), 0:HEAD_DIM], z_ref[pl.ds(0, N_META), HEAD_DIM:2 * HEAD_DIM],
            z_ref[pl.ds(0, N_META), 2 * HEAD_DIM:3 * HEAD_DIM],
            dmat_m, jnp.zeros((N_META, 1), F32), dk_m, dch_m, r_st)
        if d == 0:
            acc_ref[pl.ds(0, N_META), :] = out_m
        else:
            acc_ref[pl.ds(0, N_META), :] += out_m

        def body(i, st, d=d, dmat=dmat, dq=dq, dk=dk, dch=dch):
            c = i if d == 0 else n_chunks - 1 - i
            r0 = pl.multiple_of(N_META + c * t, 8)
            out, st = _ret_chunk(
                z_ref[pl.ds(r0, t), 0:HEAD_DIM], z_ref[pl.ds(r0, t), HEAD_DIM:2 * HEAD_DIM],
                z_ref[pl.ds(r0, t), 2 * HEAD_DIM:3 * HEAD_DIM], dmat, dq, dk, dch, st)
            if d == 0:
                acc_ref[pl.ds(r0, t), :] = out
            else:
                acc_ref[pl.ds(r0, t), :] += out
            return st

        lax.fori_loop(0, n_chunks, body, r_st)

    for r0, nr in _row_blocks(n_rows):
        hs = acc_ref[pl.ds(r0, nr), :]
        g = z_ref[pl.ds(r0, nr), 3 * HEAD_DIM:4 * HEAD_DIM]
        out_ref[0, pl.ds(r0, nr), :] = (g * _sigmoid(g) * _rms(hs, ng_ref[0])).astype(BF16)


def _ret_call(h, gn, w, cos2, sin2, dl, ng):
    bsz, n_rows, d = h.shape
    nw = w.shape[-1]
    return pl.pallas_call(
        _ret_kernel,
        grid=(bsz, N_HEADS),
        in_specs=[
            pl.BlockSpec((1, n_rows, d), lambda b, hd: (b, 0, 0)),
            pl.BlockSpec((1, d), lambda b, hd: (0, 0)),
            pl.BlockSpec((1, d, nw), lambda b, hd: (hd, 0, 0)),
            pl.BlockSpec((n_rows, HEAD_DIM), lambda b, hd: (0, 0)),
            pl.BlockSpec((n_rows, HEAD_DIM), lambda b, hd: (0, 0)),
            pl.BlockSpec((1, 1, LANES), lambda b, hd: (hd, 0, 0)),
            pl.BlockSpec((1, 1, HEAD_DIM), lambda b, hd: (hd, 0, 0)),
        ],
        out_specs=pl.BlockSpec((1, n_rows, HEAD_DIM), lambda b, hd: (b, 0, hd)),
        out_shape=jax.ShapeDtypeStruct((bsz, n_rows, N_HEADS * HEAD_DIM), BF16),
        scratch_shapes=[
            pltpu.VMEM((n_rows, d), BF16),
            pltpu.VMEM((n_rows, nw), F32),
            pltpu.VMEM((n_rows, HEAD_DIM), F32),
        ],
        compiler_params=_cparams(2),
        name="retention_heads",
    )(h, gn, w, cos2, sin2, dl, ng)


def _na_kernel(h_ref, gn_ref, w_ref, qg_ref, kg_ref, bias_ref, out_ref, un_ref, z_ref, q_ref, k_ref, v_ref):
    n_rows = h_ref.shape[1]
    n_grid_rows = (n_rows - N_META) // GRID_W
    dh = NA_HEAD_DIM
    band = NA_WIN_ROWS * GRID_W

    @pl.when(pl.program_id(1) == 0)
    def _():
        _norm_to_scratch(h_ref, gn_ref, un_ref)

    _project(un_ref, w_ref, z_ref)

    for i in (0, 1):
        for r0, nr in _row_blocks(n_rows):
            q = z_ref[pl.ds(r0, nr), i * dh:(i + 1) * dh]
            q_ref[i, pl.ds(r0, nr), :] = (_rms(q, qg_ref[...]) * dh ** -0.5).astype(BF16)
            k = z_ref[pl.ds(r0, nr), 2 * dh + i * dh:2 * dh + (i + 1) * dh]
            k_ref[i, pl.ds(r0, nr), :] = _rms(k, kg_ref[...]).astype(BF16)
            v_ref[i, pl.ds(r0, nr), :] = z_ref[pl.ds(r0, nr), 4 * dh + i * dh:4 * dh + (i + 1) * dh].astype(BF16)

    for i in (0, 1):
        k_meta = k_ref[i, pl.ds(0, N_META), :]
        v_meta = v_ref[i, pl.ds(0, N_META), :]
        s = _dot_nt(q_ref[i, pl.ds(0, N_META), :], k_meta)
        p = jnp.exp(s - jnp.max(s, axis=-1, keepdims=True))
        o = _dot(p.astype(BF16), v_meta) / jnp.sum(p, axis=-1, keepdims=True)
        out_ref[0, pl.ds(0, N_META), i * dh:(i + 1) * dh] = o.astype(BF16)

        def row_body(r, carry, i=i, k_meta=k_meta, v_meta=v_meta):
            rs = jnp.clip(r - NA_WIN_ROWS // 2, 0, n_grid_rows - NA_WIN_ROWS)
            q0 = pl.multiple_of(N_META + r * GRID_W, 8)
            k0 = pl.multiple_of(N_META + rs * GRID_W, 8)
            qr = q_ref[i, pl.ds(q0, GRID_W), :]
            s = _dot_nt(qr, k_ref[i, pl.ds(k0, band), :]) + bias_ref[i, r - rs]
            sm = _dot_nt(qr, k_meta)
            m = jnp.maximum(jnp.max(s, axis=-1, keepdims=True), jnp.max(sm, axis=-1, keepdims=True))
            p, pm = jnp.exp(s - m), jnp.exp(sm - m)
            den = jnp.sum(p, axis=-1, keepdims=True) + jnp.sum(pm, axis=-1, keepdims=True)
            o = (_dot(p.astype(BF16), v_ref[i, pl.ds(k0, band), :]) + _dot(pm.astype(BF16), v_meta)) / den
            out_ref[0, pl.ds(q0, GRID_W), i * dh:(i + 1) * dh] = o.astype(BF16)
            return carry

        lax.fori_loop(0, n_grid_rows, row_body, 0)


def _na_call(h, gn, w, qg, kg, bias):
    bsz, n_rows, d = h.shape
    n_pairs, _, nw = w.shape
    band = NA_WIN_ROWS * GRID_W
    return pl.pallas_call(
        _na_kernel,
        grid=(bsz, n_pairs),
        in_specs=[
            pl.BlockSpec((1, n_rows, d), lambda b, hp: (b, 0, 0)),
            pl.BlockSpec((1, d), lambda b, hp: (0, 0)),
            pl.BlockSpec((1, d, nw), lambda b, hp: (hp, 0, 0)),
            pl.BlockSpec((1, NA_HEAD_DIM), lambda b, hp: (0, 0)),
            pl.BlockSpec((1, NA_HEAD_DIM), lambda b, hp: (0, 0)),
            pl.BlockSpec((2, NA_WIN_ROWS, GRID_W, band), lambda b, hp: (hp, 0, 0, 0)),
        ],
        out_specs=pl.BlockSpec((1, n_rows, 2 * NA_HEAD_DIM), lambda b, hp: (b, 0, hp)),
        out_shape=jax.ShapeDtypeStruct((bsz, n_rows, n_pairs * 2 * NA_HEAD_DIM), BF16),
        scratch_shapes=[
            pltpu.VMEM((n_rows, d), BF16),
            pltpu.VMEM((n_rows, nw), F32),
            pltpu.VMEM((2, n_rows, NA_HEAD_DIM), BF16),
            pltpu.VMEM((2, n_rows, NA_HEAD_DIM), BF16),
            pltpu.VMEM((2, n_rows, NA_HEAD_DIM), BF16),
        ],
        compiler_params=_cparams(2),
        name="neighbourhood_attention",
    )(h, gn, w, qg, kg, bias)


def _lane_prefix_exclusive(x, tri_strict):
    n_blocks = x.shape[1] // LANES
    carry = jnp.zeros((x.shape[0], 1), F32)
    pieces = []
    for j in range(n_blocks):
        blk = x[:, j * LANES:(j + 1) * LANES]
        pieces.append(_dot(blk.astype(BF16), tri_strict) + carry)
        carry = carry + jnp.sum(blk, axis=-1, keepdims=True)
    return jnp.concatenate(pieces, axis=1)


def _aligned_row_blocks(n_rows, step=512):
    blocks, r0 = [], 0
    while r0 < n_rows:
        nr = min(step, n_rows - r0)
        blocks.append((r0, nr))
        r0 += nr
    return blocks


def _post_kernel(n_mix, cap, *refs):
    h_ref = refs[0]
    mix_refs = refs[1:1 + n_mix]
    wo_ref, fg_ref, wr_ref = refs[1 + n_mix:4 + n_mix]
    hn_ref, u2_ref, slot_ref, gate_ref = refs[4 + n_mix:8 + n_mix]
    lg_ref = refs[8 + n_mix]
    n_rows = h_ref.shape[1]
    n_pad = u2_ref.shape[1]

    lg_ref[...] = jnp.zeros(lg_ref.shape, F32)
    for r0, nr in _aligned_row_blocks(n_rows):
        acc = h_ref[0, pl.ds(r0, nr), :]
        k0 = 0
        for m_ref in mix_refs:
            kw = m_ref.shape[2]
            acc = acc + _dot(m_ref[0, pl.ds(r0, nr), :], wo_ref[pl.ds(k0, kw), :])
            k0 += kw
        hn_ref[0, pl.ds(r0, nr), :] = acc
        u = _rms(acc, fg_ref[...])
        u2_ref[0, pl.ds(r0, nr), :] = u.astype(BF16)
        lg_ref[:, pl.ds(r0, nr)] = lax.dot_general(
            wr_ref[...], u, (((1,), (1,)), ((), ())),
            preferred_element_type=F32, precision=lax.Precision.HIGHEST)
    u2_ref[0, pl.ds(n_rows, n_pad - n_rows), :] = jnp.zeros((n_pad - n_rows, u2_ref.shape[2]), BF16)

    logits = lg_ref[...]
    ex = jnp.exp(logits - jnp.max(logits, axis=0, keepdims=True))
    aff = ex / jnp.sum(ex, axis=0, keepdims=True)
    tok = lax.broadcasted_iota(jnp.int32, aff.shape, 1)
    aff = jnp.where(tok < n_rows, aff, -1.0)

    def count_ge(x):
        return jnp.sum(jnp.where(aff >= x, 1.0, 0.0), axis=-1, keepdims=True)

    capf = float(cap)
    tiny = jnp.full((aff.shape[0], 1), 2.0 ** -126, F32)
    ok0 = count_ge(tiny) >= capf
    p = tiny
    for j in (64, 32, 16, 8, 4, 2, 1):
        cand = p * (2.0 ** j)
        p = jnp.where(count_ge(cand) >= capf, cand, p)
    m = p
    for i in range(1, 24):
        cand = m + p * (2.0 ** -i)
        m = jnp.where(count_ge(cand) >= capf, cand, m)
    thr = jnp.where(ok0, m, 0.0)

    gt = aff > thr
    eq = aff == thr
    need = capf - jnp.sum(jnp.where(gt, 1.0, 0.0), axis=-1, keepdims=True)
    r = lax.broadcasted_iota(jnp.int32, (LANES, LANES), 0)
    c = lax.broadcasted_iota(jnp.int32, (LANES, LANES), 1)
    tri_strict = jnp.where(r < c, 1.0, 0.0).astype(BF16)
    eq_rank = _lane_prefix_exclusive(jnp.where(eq, 1.0, 0.0), tri_strict)
    sel = gt | (eq & (eq_rank < need))
    pos = _lane_prefix_exclusive(jnp.where(sel, 1.0, 0.0), tri_strict)
    slot_ref[0] = jnp.where(sel, pos, -1.0).astype(jnp.int32)
    gate_ref[0] = jnp.where(sel, aff, 0.0)


def _post_call(h, mixes, wo, fg, wr, cap):
    bsz, n_rows, d = h.shape
    n_pad = ((n_rows + LANES - 1) // LANES) * LANES
    n_mix = len(mixes)
    row_spec = lambda w: pl.BlockSpec((1, n_rows, w), lambda b: (b, 0, 0), pipeline_mode=pl.Buffered(1))
    return pl.pallas_call(
        functools.partial(_post_kernel, n_mix, cap),
        grid=(bsz,),
        in_specs=[row_spec(d)] + [row_spec(m.shape[2]) for m in mixes] + [
            pl.BlockSpec((d, d), lambda b: (0, 0)),
            pl.BlockSpec((1, d), lambda b: (0, 0)),
            pl.BlockSpec((N_EXPERTS, d), lambda b: (0, 0)),
        ],
        out_specs=[
            row_spec(d),
            pl.BlockSpec((1, n_pad, d), lambda b: (b, 0, 0), pipeline_mode=pl.Buffered(1)),
            pl.BlockSpec((1, N_EXPERTS, n_pad), lambda b: (b, 0, 0)),
            pl.BlockSpec((1, N_EXPERTS, n_pad), lambda b: (b, 0, 0)),
        ],
        out_shape=[
            jax.ShapeDtypeStruct((bsz, n_rows, d), F32),
            jax.ShapeDtypeStruct((bsz, n_pad, d), BF16),
            jax.ShapeDtypeStruct((bsz, N_EXPERTS, n_pad), jnp.int32),
            jax.ShapeDtypeStruct((bsz, N_EXPERTS, n_pad), F32),
        ],
        scratch_shapes=[pltpu.VMEM((N_EXPERTS, n_pad), F32)],
        compiler_params=_cparams(1),
        name="outproj_router",
    )(h, *mixes, wo, fg, wr)


def _gather_kernel(u2_ref, slot_ref, xs_ref):
    cap_pad = xs_ref.shape[2]
    n_pad = u2_ref.shape[1]
    srow_id = lax.broadcasted_iota(jnp.int32, (cap_pad, n_pad), 0)

    def body(e, carry):
        onehot = jnp.where(srow_id == slot_ref[0, pl.ds(e, 1), :], 1.0, 0.0).astype(BF16)
        xs_ref[0, e] = _dot(onehot, u2_ref[0]).astype(BF16)
        return carry

    lax.fori_loop(0, N_EXPERTS, body, 0)


def _gather_call(u2, slot, cap_pad):
    bsz, n_pad, d = u2.shape
    return pl.pallas_call(
        _gather_kernel,
        grid=(bsz,),
        in_specs=[
            pl.BlockSpec((1, n_pad, d), lambda b: (b, 0, 0)),
            pl.BlockSpec((1, N_EXPERTS, n_pad), lambda b: (b, 0, 0)),
        ],
        out_specs=pl.BlockSpec((1, N_EXPERTS, cap_pad, d), lambda b: (b, 0, 0, 0)),
        out_shape=jax.ShapeDtypeStruct((bsz, N_EXPERTS, cap_pad, d), BF16),
        compiler_params=_cparams(1),
        name="expert_gather",
    )(u2, slot)


def _expert_kernel(xs_ref, wg_ref, wu_ref, wd_ref, ys_ref, acc_ref):
    bsz, _, cap_pad, d = xs_ref.shape
    fc = pl.program_id(1)
    x = xs_ref[...].reshape(bsz * cap_pad, d)
    g = _dot(x, wg_ref[0].astype(BF16))
    u = _dot(x, wu_ref[0].astype(BF16))
    hdn = (g * _sigmoid(g) * u).astype(BF16)
    y = _dot(hdn, wd_ref[0].astype(BF16))

    @pl.when(fc == 0)
    def _():
        acc_ref[...] = y

    @pl.when(fc > 0)
    def _():
        acc_ref[...] += y

    @pl.when(fc == pl.num_programs(1) - 1)
    def _():
        ys_ref[...] = acc_ref[...].astype(BF16).reshape(ys_ref.shape)


def _expert_call(xs, wg, wu, wd, layer, f_chunks=4):
    bsz, n_e, cap_pad, d = xs.shape
    ff = wg.shape[-1]
    fb = ff // f_chunks
    return pl.pallas_call(
        _expert_kernel,
        grid=(n_e, f_chunks),
        in_specs=[
            pl.BlockSpec((bsz, 1, cap_pad, d), lambda e, f: (0, e, 0, 0)),
            pl.BlockSpec((1, None, d, fb), lambda e, f: (layer, e, 0, f)),
            pl.BlockSpec((1, None, d, fb), lambda e, f: (layer, e, 0, f)),
            pl.BlockSpec((1, None, fb, d), lambda e, f: (layer, e, f, 0)),
        ],
        out_specs=pl.BlockSpec((bsz, 1, cap_pad, d), lambda e, f: (0, e, 0, 0)),
        out_shape=jax.ShapeDtypeStruct((bsz, n_e, cap_pad, d), BF16),
        scratch_shapes=[pltpu.VMEM((bsz * cap_pad, d), F32)],
        compiler_params=_cparams(2),
        name="expert_swiglu",
    )(xs, wg, wu, wd)


def _combine_kernel(h_ref, ys_ref, slot_ref, gate_ref, out_ref):
    n_rows = h_ref.shape[1]
    cap_pad = ys_ref.shape[2]
    n_pad = slot_ref.shape[2]
    e = pl.program_id(1)

    @pl.when(e == 0)
    def _():
        out_ref[0] = h_ref[0]

    srow_id = lax.broadcasted_iota(jnp.int32, (cap_pad, n_pad), 0)
    hit = srow_id == slot_ref[0, pl.ds(e, 1), :]
    weights = jnp.where(hit, gate_ref[0, pl.ds(e, 1), :], 0.0).astype(BF16)
    for t0, tn in _aligned_row_blocks(n_pad, 1024):
        nr = min(tn, n_rows - t0)
        part = _dot_tn(weights[:, t0:t0 + tn], ys_ref[0, 0])
        out_ref[0, pl.ds(t0, nr), :] += part[0:nr, :]


def _combine_call(h, ys, slot, gate):
    bsz, n_rows, d = h.shape
    _, n_e, cap_pad, _ = ys.shape
    n_pad = slot.shape[2]
    return pl.pallas_call(
        _combine_kernel,
        grid=(bsz, n_e),
        in_specs=[
            pl.BlockSpec((1, n_rows, d), lambda b, e: (b, 0, 0)),
            pl.BlockSpec((1, 1, cap_pad, d), lambda b, e: (b, e, 0, 0)),
            pl.BlockSpec((1, n_e, n_pad), lambda b, e: (b, 0, 0)),
            pl.BlockSpec((1, n_e, n_pad), lambda b, e: (b, 0, 0)),
        ],
        out_specs=pl.BlockSpec((1, n_rows, d), lambda b, e: (b, 0, 0)),
        out_shape=jax.ShapeDtypeStruct((bsz, n_rows, d), F32),
        compiler_params=_cparams(2),
        name="expert_combine",
    )(h, ys, slot, gate)


def _even_weights(w_in, conv_w, gate_b, decay_logit):
    d = w_in.shape[0]
    mw = N_HEADS * HEAD_DIM
    mq, mk, mv, mo = (w_in[:, i * mw:(i + 1) * mw] for i in range(4))
    gates = w_in[:, 4 * mw:4 * mw + 4 * N_HEADS]
    r0 = 4 * mw + 4 * N_HEADS
    rq, rk, rv, rg = (w_in[:, r0 + i * mw:r0 + (i + 1) * mw] for i in range(4))

    def per_head(t):
        return t.reshape(d, N_HEADS, HEAD_DIM).transpose(1, 0, 2)

    def gate_cols(fw_off, bw_off):
        cols = jnp.stack([gates[:, fw_off:fw_off + N_HEADS], gates[:, bw_off:bw_off + N_HEADS]], axis=-1)
        cols = cols.transpose(1, 0, 2)
        return jnp.pad(cols, ((0, 0), (0, 0), (0, LANES - 2)))

    def gate_bias(fw_off, bw_off):
        b = jnp.stack([gate_b[fw_off:fw_off + N_HEADS], gate_b[bw_off:bw_off + N_HEADS]], axis=-1)
        return jnp.pad(b, ((0, 0), (0, LANES - 2)))

    w_m = jnp.concatenate([per_head(mq), per_head(mk), per_head(mv), per_head(mo),
                           gate_cols(0, 2 * N_HEADS), gate_cols(N_HEADS, 3 * N_HEADS)], axis=-1).astype(BF16)
    gb = jnp.concatenate([gate_bias(0, 2 * N_HEADS), gate_bias(N_HEADS, 3 * N_HEADS)], axis=-1)[:, None, :]
    conv = jnp.concatenate([conv_w[:, :mw].reshape(CONV_K, N_HEADS, HEAD_DIM),
                            conv_w[:, mw:].reshape(CONV_K, N_HEADS, HEAD_DIM)], axis=-1).transpose(1, 0, 2)
    w_r = jnp.concatenate([per_head(rq), per_head(rk), per_head(rv), per_head(rg)], axis=-1).astype(BF16)
    dl = jnp.pad(decay_logit.T, ((0, 0), (0, LANES - 2)))[:, None, :]
    return w_m, gb.astype(F32), conv.astype(F32), w_r, dl.astype(F32)


def _rotary_tables(n_rows):
    half = HEAD_DIM // 2
    inv = ROPE_BASE ** (-jnp.arange(half, dtype=F32) / half)
    ang = jnp.arange(n_rows, dtype=F32)[:, None] * inv[None, :]
    cos, sin = jnp.cos(ang), jnp.sin(ang)
    return jnp.concatenate([cos, cos], axis=-1), jnp.concatenate([-sin, sin], axis=-1)


def _na_weights(w_in):
    d = w_in.shape[0]
    n_heads = w_in.shape[1] // (3 * NA_HEAD_DIM)
    z = w_in.reshape(d, 3, n_heads // 2, 2 * NA_HEAD_DIM)
    return z.transpose(2, 0, 1, 3).reshape(n_heads // 2, d, 6 * NA_HEAD_DIM).astype(BF16)


def _na_bias_table(rpb):
    col = jnp.arange(GRID_W)
    col_start = jnp.clip(col - NA_WIN_COLS // 2, 0, GRID_W - NA_WIN_COLS)
    col_in = (col[None, :] >= col_start[:, None]) & (col[None, :] < col_start[:, None] + NA_WIN_COLS)
    dc_idx = jnp.clip(col[None, :] - col[:, None], -(NA_WIN_COLS - 1), NA_WIN_COLS - 1) + NA_WIN_COLS - 1
    rpb_cols = rpb.astype(F32)[:, :, dc_idx]
    off = jnp.arange(NA_WIN_ROWS)
    dr_idx = jnp.arange(NA_WIN_ROWS)[None, :] - off[:, None] + NA_WIN_ROWS - 1
    tbl = rpb_cols[:, dr_idx]
    tbl = jnp.where(col_in[None, None, None], tbl, NEG)
    tbl = tbl.transpose(0, 1, 3, 2, 4)
    return tbl.reshape(rpb.shape[0], NA_WIN_ROWS, GRID_W, NA_WIN_ROWS * GRID_W)


def _ffn(h, mixes, wo, fg, wr, wg, wu, wd, layer):
    n_rows = h.shape[1]
    cap = CAP_FACTOR * n_rows // N_EXPERTS
    cap_pad = ((cap + BF16_ROWS - 1) // BF16_ROWS) * BF16_ROWS
    hn, u2, slot, gate = _post_call(h, mixes, wo.astype(BF16), fg[None, :], wr.T, cap)
    xs = _gather_call(u2, slot, cap_pad)
    ys = _expert_call(xs, wg, wu, wd, layer)
    return _combine_call(hn, ys, slot, gate)


def kernel(x, meta_tokens, attn_norm_g, ffn_norm_g, even_w_in, even_conv_w, even_gate_b, even_m_norm_g, even_ret_decay_logit, even_r_norm_g, even_w_out, odd_w_in, odd_q_norm_g, odd_k_norm_g, odd_rpb, odd_w_out, router_w, expert_w_gate, expert_w_up, expert_w_down):
    bsz = x.shape[0]
    depth = attn_norm_g.shape[0]
    meta = jnp.broadcast_to(meta_tokens.astype(x.dtype)[None], (bsz,) + meta_tokens.shape)
    h = jnp.concatenate([meta, x], axis=1)
    n_rows = h.shape[1]
    cos2, sin2 = _rotary_tables(n_rows)
    mw = N_HEADS * HEAD_DIM
    for layer in range(depth):
        j = layer // 2
        gn = attn_norm_g[layer][None, :]
        if layer % 2 == 0:
            w_m, gb, conv, w_r, dl = _even_weights(even_w_in[j], even_conv_w[j], even_gate_b[j],
                                                   even_ret_decay_logit[j])
            m_out = _mlstm_call(h, gn, w_m, conv, gb, even_m_norm_g[j].reshape(N_HEADS, 1, HEAD_DIM))
            r_out = _ret_call(h, gn, w_r, cos2, sin2, dl, even_r_norm_g[j].reshape(N_HEADS, 1, HEAD_DIM))
            mixes, wo = [m_out, r_out], even_w_out[j]
        else:
            a_out = _na_call(h, gn, _na_weights(odd_w_in[j]), odd_q_norm_g[j][None, :],
                             odd_k_norm_g[j][None, :], _na_bias_table(odd_rpb[j]))
            mixes, wo = [a_out], odd_w_out[j]
        h = _ffn(h, mixes, wo, ffn_norm_g[layer], router_w[layer],
                 expert_w_gate, expert_w_up, expert_w_down, layer)
    return h[:, N_META:]
```

```python
import functools

import jax
import jax.numpy as jnp
from jax import lax
from jax.experimental import pallas as pl
from jax.experimental.pallas import tpu as pltpu

F32 = jnp.float32
BF16 = jnp.bfloat16

LANES = 128
BF16_ROWS = 16
N_META = 16
GRID_W = 64
EPS = 1e-6
HEAD_DIM = 128
N_HEADS = 4
CONV_K = 5
CONV_PAD = 8
ROPE_BASE = 10000.0
NA_HEAD_DIM = 64
NA_WIN_ROWS = 8
NA_WIN_COLS = 16
N_EXPERTS = 16
CAP_FACTOR = 2
NEG = -1e30
SCAN_T = 128
VMEM_LIMIT = 56 * 1024 * 1024


def _cparams(n_axes):
    return pltpu.CompilerParams(
        dimension_semantics=("arbitrary",) * n_axes, vmem_limit_bytes=VMEM_LIMIT)


def _row_blocks(n_rows):
    for nb in (3, 2, 4, 6, 1):
        if n_rows % (8 * nb) == 0:
            step = n_rows // nb
            return [(i * step, step) for i in range(nb)]
    return [(0, n_rows)]


def _rms(x, g):
    return x * lax.rsqrt(jnp.mean(x * x, axis=-1, keepdims=True) + EPS) * g


def _sigmoid(x):
    return 1.0 / (1.0 + jnp.exp(-x))


def _log_sigmoid(x):
    return jnp.minimum(x, 0.0) - jnp.log(1.0 + jnp.exp(-jnp.abs(x)))


def _dot(a, b):
    return jnp.dot(a, b, preferred_element_type=F32)


def _dot_nt(a, b):
    return lax.dot_general(a, b, (((1,), (1,)), ((), ())), preferred_element_type=F32)


def _dot_tn(a, b):
    return lax.dot_general(a, b, (((0,), (0,)), ((), ())), preferred_element_type=F32)


def _split3(x):
    hi = x.astype(BF16)
    r1 = x - hi.astype(F32)
    mid = r1.astype(BF16)
    lo = (r1 - mid.astype(F32)).astype(BF16)
    return hi, mid, lo


def _tri_prefix(tri_bf16, x):
    hi, mid, lo = _split3(x)
    return _dot(tri_bf16, hi) + _dot(tri_bf16, mid) + _dot(tri_bf16, lo)


def _norm_to_scratch(h_ref, gn_ref, un_ref):
    n_rows = h_ref.shape[1]
    for r0, nr in _row_blocks(n_rows):
        x = h_ref[0, pl.ds(r0, nr), :]
        un_ref[pl.ds(r0, nr), :] = _rms(x, gn_ref[...]).astype(BF16)


def _project(un_ref, w_ref, z_ref):
    n_rows = un_ref.shape[0]
    for r0, nr in _row_blocks(n_rows):
        z_ref[pl.ds(r0, nr), :] = _dot(un_ref[pl.ds(r0, nr), :], w_ref[0])


def _tri_mask(n, lower):
    r = lax.broadcasted_iota(jnp.int32, (n, n), 0)
    c = lax.broadcasted_iota(jnp.int32, (n, n), 1)
    return (c <= r) if lower else (c >= r)


def _mlstm_chunk(qc, kc, vc, a_row, a_col, f_col, mask, state):
    c_st, n_st, g_prev = state
    qb, kb, vb = qc.astype(BF16), kc.astype(BF16), vc.astype(BF16)
    cm = jnp.max(jnp.where(mask, a_row, NEG), axis=-1, keepdims=True)
    g_col = jnp.maximum(g_prev, cm)
    dm = jnp.exp(jnp.where(mask, a_row - g_col, NEG))
    s = _dot_nt(qb, kb) * dm
    w_inter = jnp.exp(g_prev - g_col)
    num = w_inter * _dot(qb, c_st.astype(BF16)) + _dot(s.astype(BF16), vb)
    den = (w_inter * jnp.sum(qc * n_st, axis=-1, keepdims=True)
           + jnp.sum(s, axis=-1, keepdims=True))
    out = num / jnp.maximum(jnp.abs(den), jnp.exp(-(f_col + g_col)))
    g_end = jnp.maximum(g_prev, jnp.max(a_row, axis=-1, keepdims=True))
    kw = kc * jnp.exp(a_col - g_end)
    w_state = jnp.exp(g_prev - g_end)
    c_new = w_state * c_st + _dot_tn(kw.astype(BF16), vb)
    n_new = w_state * n_st + jnp.sum(kw, axis=0, keepdims=True)
    return out, (c_new, n_new, g_end)


def _mlstm_kernel(h_ref, gn_ref, w_ref, conv_ref, gb_ref, ng_ref, out_ref,
                  un_ref, z_ref, zc_ref, acc_ref, acol_ref, fcol_ref, atm_ref, atr_ref):
    n_rows = h_ref.shape[1]
    n_real = n_rows - N_META
    t = SCAN_T
    n_chunks = n_real // t

    @pl.when(pl.program_id(1) == 0)
    def _():
        _norm_to_scratch(h_ref, gn_ref, un_ref)

    _project(un_ref, w_ref, z_ref)

    zc_ref[pl.ds(0, CONV_PAD), :] = jnp.zeros((CONV_PAD, 2 * HEAD_DIM), F32)
    zc_ref[pl.ds(CONV_PAD + n_rows, CONV_PAD), :] = jnp.zeros((CONV_PAD, 2 * HEAD_DIM), F32)
    for r0, nr in _row_blocks(n_rows):
        zc_ref[pl.ds(CONV_PAD + r0, nr), :] = z_ref[pl.ds(r0, nr), 0:2 * HEAD_DIM]
    lane2 = lax.broadcasted_iota(jnp.int32, (1, 2 * HEAD_DIM), 1)
    qk_scale = jnp.where(lane2 >= HEAD_DIM, HEAD_DIM ** -0.5, 1.0).astype(F32)
    for r0, nr in _row_blocks(n_rows):
        acc = jnp.zeros((nr, 2 * HEAD_DIM), F32)
        for j in range(CONV_K):
            off = CONV_PAD - (CONV_K - 1) // 2 + j + r0
            acc = acc + zc_ref[pl.ds(off, nr), :] * conv_ref[0, pl.ds(j, 1), :]
        z_ref[pl.ds(r0, nr), 0:2 * HEAD_DIM] = acc * _sigmoid(acc) * qk_scale

    gi_off, gf_off = 4 * HEAD_DIM, 5 * HEAD_DIM
    bias_i = gb_ref[0, :, 0:LANES]
    bias_f = gb_ref[0, :, LANES:2 * LANES]
    lane = lax.broadcasted_iota(jnp.int32, (1, LANES), 1)
    tri_m = jnp.where(_tri_mask(N_META, True), 1.0, 0.0).astype(BF16)
    lf_m = _log_sigmoid(z_ref[pl.ds(0, N_META), gf_off:gf_off + LANES] + bias_f)
    f_meta = _tri_prefix(tri_m, lf_m)
    f_meta_end = f_meta[N_META - 1:N_META, :]
    tri_b = jnp.where(_tri_mask(LANES, True), 1.0, 0.0).astype(BF16)

    def prefix_body(c, carry):
        r0 = pl.multiple_of(N_META + c * LANES, 8)
        lf = _log_sigmoid(z_ref[pl.ds(r0, LANES), gf_off:gf_off + LANES] + bias_f)
        p = _tri_prefix(tri_b, lf) + carry
        fcol_ref[pl.ds(r0, LANES), :] = p
        return p[LANES - 1:LANES, :]

    total = lax.fori_loop(0, n_real // LANES, prefix_body, jnp.zeros((1, LANES), F32))

    fcol_ref[pl.ds(0, N_META), :] = f_meta
    acol_ref[pl.ds(0, N_META), :] = z_ref[pl.ds(0, N_META), gi_off:gi_off + LANES] + bias_i - f_meta

    def finish_body(c, carry):
        r0 = pl.multiple_of(N_META + c * LANES, 8)
        lf = _log_sigmoid(z_ref[pl.ds(r0, LANES), gf_off:gf_off + LANES] + bias_f)
        p = fcol_ref[pl.ds(r0, LANES), :]
        f = f_meta_end + jnp.where(lane == 0, p, total - p + lf)
        fcol_ref[pl.ds(r0, LANES), :] = f
        a = z_ref[pl.ds(r0, LANES), gi_off:gi_off + LANES] + bias_i - f
        acol_ref[pl.ds(r0, LANES), :] = a
        atr_ref[:, pl.ds(pl.multiple_of(c * LANES, LANES), LANES)] = a.T[0:8, :]
        return carry

    lax.fori_loop(0, n_real // LANES, finish_body, 0)
    atm_ref[...] = acol_ref[pl.ds(0, LANES), :].T[0:8, :]

    mask_m = _tri_mask(N_META, True)
    for d in (0, 1):
        mask = _tri_mask(t, d == 0)
        state = (jnp.zeros((HEAD_DIM, HEAD_DIM), F32), jnp.zeros((1, HEAD_DIM), F32),
                 jnp.zeros((1, 1), F32))
        out_m, state = _mlstm_chunk(
            z_ref[pl.ds(0, N_META), 0:HEAD_DIM], z_ref[pl.ds(0, N_META), HEAD_DIM:2 * HEAD_DIM],
            z_ref[pl.ds(0, N_META), 2 * HEAD_DIM:3 * HEAD_DIM],
            atm_ref[pl.ds(d, 1), 0:N_META], acol_ref[pl.ds(0, N_META), d:d + 1],
            fcol_ref[pl.ds(0, N_META), d:d + 1], mask_m, state)
        if d == 0:
            acc_ref[pl.ds(0, N_META), :] = out_m
        else:
            acc_ref[pl.ds(0, N_META), :] += out_m

        def body(i, st, d=d, mask=mask):
            c = i if d == 0 else n_chunks - 1 - i
            r0 = pl.multiple_of(N_META + c * t, 8)
            l0 = pl.multiple_of(c * t, LANES)
            out, st = _mlstm_chunk(
                z_ref[pl.ds(r0, t), 0:HEAD_DIM], z_ref[pl.ds(r0, t), HEAD_DIM:2 * HEAD_DIM],
                z_ref[pl.ds(r0, t), 2 * HEAD_DIM:3 * HEAD_DIM],
                atr_ref[pl.ds(d, 1), pl.ds(l0, t)], acol_ref[pl.ds(r0, t), d:d + 1],
                fcol_ref[pl.ds(r0, t), d:d + 1], mask, st)
            if d == 0:
                acc_ref[pl.ds(r0, t), :] = out
            else:
                acc_ref[pl.ds(r0, t), :] += out
            return st

        lax.fori_loop(0, n_chunks, body, state)

    for r0, nr in _row_blocks(n_rows):
        hs = acc_ref[pl.ds(r0, nr), :]
        o = z_ref[pl.ds(r0, nr), 3 * HEAD_DIM:4 * HEAD_DIM]
        out_ref[0, pl.ds(r0, nr), :] = (_sigmoid(o) * _rms(hs, ng_ref[0])).astype(BF16)


def _mlstm_call(h, gn, w, conv, gb, ng):
    bsz, n_rows, d = h.shape
    nw = w.shape[-1]
    n_real = n_rows - N_META
    return pl.pallas_call(
        _mlstm_kernel,
        grid=(bsz, N_HEADS),
        in_specs=[
            pl.BlockSpec((1, n_rows, d), lambda b, hd: (b, 0, 0)),
            pl.BlockSpec((1, d), lambda b, hd: (0, 0)),
            pl.BlockSpec((1, d, nw), lambda b, hd: (hd, 0, 0)),
            pl.BlockSpec((1, CONV_K, 2 * HEAD_DIM), lambda b, hd: (hd, 0, 0)),
            pl.BlockSpec((1, 1, 2 * LANES), lambda b, hd: (hd, 0, 0)),
            pl.BlockSpec((1, 1, HEAD_DIM), lambda b, hd: (hd, 0, 0)),
        ],
        out_specs=pl.BlockSpec((1, n_rows, HEAD_DIM), lambda b, hd: (b, 0, hd)),
        out_shape=jax.ShapeDtypeStruct((bsz, n_rows, N_HEADS * HEAD_DIM), BF16),
        scratch_shapes=[
            pltpu.VMEM((n_rows, d), BF16),
            pltpu.VMEM((n_rows, nw), F32),
            pltpu.VMEM((n_rows + 2 * CONV_PAD, 2 * HEAD_DIM), F32),
            pltpu.VMEM((n_rows, HEAD_DIM), F32),
            pltpu.VMEM((n_rows, LANES), F32),
            pltpu.VMEM((n_rows, LANES), F32),
            pltpu.VMEM((8, LANES), F32),
            pltpu.VMEM((8, n_real), F32),
        ],
        compiler_params=_cparams(2),
        name="mlstm_heads",
    )(h, gn, w, conv, gb, ng)


def _ret_chunk(qc, kc, vc, dmat, dq, dk, dchunk, r_st):
    qb, kb, vb = qc.astype(BF16), kc.astype(BF16), vc.astype(BF16)
    s = _dot_nt(qb, kb) * dmat
    out = _dot(s.astype(BF16), vb) + dq * _dot(qb, r_st.astype(BF16))
    r_new = dchunk * r_st + _dot_tn((kc * dk).astype(BF16), vb)
    return out, r_new


def _decay_tables(n, lg, forward):
    r = lax.broadcasted_iota(jnp.int32, (n, n), 0)
    c = lax.broadcasted_iota(jnp.int32, (n, n), 1)
    pos = lax.broadcasted_iota(jnp.int32, (n, 1), 0).astype(F32)
    if forward:
        dist, mask = (r - c).astype(F32), c <= r
        dq, dk = jnp.exp((pos + 1.0) * lg), jnp.exp((n - 1.0 - pos) * lg)
    else:
        dist, mask = (c - r).astype(F32), c >= r
        dq, dk = jnp.exp((n - pos) * lg), jnp.exp(pos * lg)
    dmat = jnp.exp(jnp.where(mask, dist * lg, NEG))
    return dmat, dq, dk, jnp.exp(n * lg)


def _ret_kernel(h_ref, gn_ref, w_ref, cos_ref, sin_ref, dl_ref, ng_ref, out_ref,
                un_ref, z_ref, acc_ref):
    n_rows = h_ref.shape[1]
    n_real = n_rows - N_META
    t = SCAN_T
    n_chunks = n_real // t

    @pl.when(pl.program_id(1) == 0)
    def _():
        _norm_to_scratch(h_ref, gn_ref, un_ref)

    _project(un_ref, w_ref, z_ref)

    for r0, nr in _row_blocks(n_rows):
        cs, sn = cos_ref[pl.ds(r0, nr), :], sin_ref[pl.ds(r0, nr), :]
        q = z_ref[pl.ds(r0, nr), 0:HEAD_DIM]
        z_ref[pl.ds(r0, nr), 0:HEAD_DIM] = q * cs + pltpu.roll(q, HEAD_DIM // 2, 1) * sn
        k = z_ref[pl.ds(r0, nr), HEAD_DIM:2 * HEAD_DIM]
        z_ref[pl.ds(r0, nr), HEAD_DIM:2 * HEAD_DIM] = (
            (k * cs + pltpu.roll(k, HEAD_DIM // 2, 1) * sn) * HEAD_DIM ** -0.5)

    lg_all = _log_sigmoid(dl_ref[0])
    for d in (0, 1):
        lg = lg_all[:, d:d + 1]
        dmat_m, _, dk_m, dch_m = _decay_tables(N_META, lg, True)
        dmat, dq, dk, dch = _decay_tables(t, lg, d == 0)
        r_st = jnp.zeros((HEAD_DIM, HEAD_DIM), F32)
        out_m, r_st = _ret_chunk(
            z_ref[pl.ds(0, N_META), 0:HEAD_DIM], z_ref[pl.ds(0, N_META), HEAD_DIM:2 * HEAD_DIM],
            z_ref[pl.ds(0, N_META), 2 * HEAD_DIM:3 * HEAD_DIM],
            dmat_m, jnp.zeros((N_META, 1), F32), dk_m, dch_m, r_st)
        if d == 0:
            acc_ref[pl.ds(0, N_META), :] = out_m
        else:
            acc_ref[pl.ds(0, N_META), :] += out_m

        def body(i, st, d=d, dmat=dmat, dq=dq, dk=dk, dch=dch):
            c = i if d == 0 else n_chunks - 1 - i
            r0 = pl.multiple_of(N_META + c * t, 8)
            out, st = _ret_chunk(
                z_ref[pl.ds(r0, t), 0:HEAD_DIM], z_ref[pl.ds(r0, t), HEAD_DIM:2 * HEAD_DIM],
                z_ref[pl.ds(r0, t), 2 * HEAD_DIM:3 * HEAD_DIM], dmat, dq, dk, dch, st)
            if d == 0:
                acc_ref[pl.ds(r0, t), :] = out
            else:
                acc_ref[pl.ds(r0, t), :] += out
            return st

        lax.fori_loop(0, n_chunks, body, r_st)

    for r0, nr in _row_blocks(n_rows):
        hs = acc_ref[pl.ds(r0, nr), :]
        g = z_ref[pl.ds(r0, nr), 3 * HEAD_DIM:4 * HEAD_DIM]
        out_ref[0, pl.ds(r0, nr), :] = (g * _sigmoid(g) * _rms(hs, ng_ref[0])).astype(BF16)


def _ret_call(h, gn, w, cos2, sin2, dl, ng):
    bsz, n_rows, d = h.shape
    nw = w.shape[-1]
    return pl.pallas_call(
        _ret_kernel,
        grid=(bsz, N_HEADS),
        in_specs=[
            pl.BlockSpec((1, n_rows, d), lambda b, hd: (b, 0, 0)),
            pl.BlockSpec((1, d), lambda b, hd: (0, 0)),
            pl.BlockSpec((1, d, nw), lambda b, hd: (hd, 0, 0)),
            pl.BlockSpec((n_rows, HEAD_DIM), lambda b, hd: (0, 0)),
            pl.BlockSpec((n_rows, HEAD_DIM), lambda b, hd: (0, 0)),
            pl.BlockSpec((1, 1, LANES), lambda b, hd: (hd, 0, 0)),
            pl.BlockSpec((1, 1, HEAD_DIM), lambda b, hd: (hd, 0, 0)),
        ],
        out_specs=pl.BlockSpec((1, n_rows, HEAD_DIM), lambda b, hd: (b, 0, hd)),
        out_shape=jax.ShapeDtypeStruct((bsz, n_rows, N_HEADS * HEAD_DIM), BF16),
        scratch_shapes=[
            pltpu.VMEM((n_rows, d), BF16),
            pltpu.VMEM((n_rows, nw), F32),
            pltpu.VMEM((n_rows, HEAD_DIM), F32),
        ],
        compiler_params=_cparams(2),
        name="retention_heads",
    )(h, gn, w, cos2, sin2, dl, ng)


def _head_pair_block_diag(qp):
    lane = lax.broadcasted_iota(jnp.int32, qp.shape, 1)
    zero = jnp.zeros_like(qp)
    return jnp.concatenate([jnp.where(lane < NA_HEAD_DIM, qp, zero),
                            jnp.where(lane >= NA_HEAD_DIM, qp, zero)], axis=0)


def _na_attend(qp, keys, values, bias_t):
    n = qp.shape[0]
    bd = _head_pair_block_diag(qp)
    scores = []
    for kb, bt in zip(keys, bias_t):
        s = _dot_nt(kb, bd)
        scores.append(s if bt is None else s + bt)
    m = scores[0].max(axis=0, keepdims=True)
    for s in scores[1:]:
        m = jnp.maximum(m, s.max(axis=0, keepdims=True))
    probs = [jnp.exp(s - m) for s in scores]
    den = probs[0].sum(axis=0, keepdims=True)
    for p in probs[1:]:
        den = den + p.sum(axis=0, keepdims=True)
    inv = 1.0 / den
    o2 = None
    for p, vb in zip(probs, values):
        part = _dot_tn((p * inv).astype(BF16), vb)
        o2 = part if o2 is None else o2 + part
    lane = lax.broadcasted_iota(jnp.int32, (n, 2 * NA_HEAD_DIM), 1)
    return jnp.where(lane < NA_HEAD_DIM, o2[0:n, :], o2[n:2 * n, :])


def _na_kernel(h_ref, gn_ref, w_ref, qg_ref, kg_ref, bias_ref, out_ref, un_ref, z_ref, q_ref, k_ref, vt_ref, ot_ref):
    n_rows = h_ref.shape[1]
    n_real = n_rows - N_META
    n_grid_rows = n_real // GRID_W
    dh = NA_HEAD_DIM
    pw = 2 * dh
    band = NA_WIN_ROWS * GRID_W

    @pl.when(pl.program_id(1) == 0)
    def _():
        _norm_to_scratch(h_ref, gn_ref, un_ref)

    _project(un_ref, w_ref, z_ref)

    r = lax.broadcasted_iota(jnp.int32, (pw, pw), 0)
    c = lax.broadcasted_iota(jnp.int32, (pw, pw), 1)
    half_ones = jnp.where((r < dh) == (c < dh), 1.0, 0.0).astype(BF16)

    def half_rms(x, g):
        sq = x * x
        hi = sq.astype(BF16)
        lo = (sq - hi.astype(F32)).astype(BF16)
        ssq = _dot(hi, half_ones) + _dot(lo, half_ones)
        return x * lax.rsqrt(ssq * (1.0 / dh) + EPS) * g

    for r0, nr in _row_blocks(n_rows):
        q_ref[pl.ds(r0, nr), :] = (half_rms(z_ref[pl.ds(r0, nr), 0:pw], qg_ref[...]) * dh ** -0.5).astype(BF16)
        k_ref[pl.ds(r0, nr), :] = half_rms(z_ref[pl.ds(r0, nr), pw:2 * pw], kg_ref[...]).astype(BF16)

    n_tblocks = n_real // LANES
    for cpy in (0, 1):
        for c in range(n_tblocks - cpy):
            vb = z_ref[pl.ds(N_META + cpy * GRID_W + c * LANES, LANES), 2 * pw:3 * pw]
            vt_ref[cpy, :, c * LANES:(c + 1) * LANES] = vb.T.astype(BF16)
    vt_ref[1, :, (n_tblocks - 1) * LANES:n_tblocks * LANES] = jnp.zeros((pw, LANES), BF16)
    v_meta = z_ref[pl.ds(0, N_META), 2 * pw:3 * pw].astype(BF16)
    vt_meta = z_ref[pl.ds(0, LANES), 2 * pw:3 * pw].T[:, 0:N_META].astype(BF16)

    k_meta = k_ref[pl.ds(0, N_META), :]
    out_ref[0, pl.ds(0, N_META), :] = _na_attend(
        q_ref[pl.ds(0, N_META), :], [k_meta], [v_meta], [None]).astype(BF16)

    sub = lax.broadcasted_iota(jnp.int32, (pw, pw), 0)
    lane = lax.broadcasted_iota(jnp.int32, (pw, pw), 1)

    def row_out_t(r):
        rs = jnp.clip(r - NA_WIN_ROWS // 2, 0, n_grid_rows - NA_WIN_ROWS)
        q0 = pl.multiple_of(N_META + r * GRID_W, 8)
        k0 = pl.multiple_of(N_META + rs * GRID_W, 8)
        bd = _head_pair_block_diag(q_ref[pl.ds(q0, GRID_W), :])
        s = _dot_nt(k_ref[pl.ds(k0, band), :], bd) + bias_ref[0, r - rs]
        sm = _dot_nt(k_meta, bd)
        m = jnp.maximum(s.max(axis=0, keepdims=True), sm.max(axis=0, keepdims=True))
        p, pm = jnp.exp(s - m), jnp.exp(sm - m)
        den = p.sum(axis=0, keepdims=True) + pm.sum(axis=0, keepdims=True)
        par = rs % 2
        l0 = pl.multiple_of((rs - par) * GRID_W, LANES)
        o_t = (_dot(vt_ref[par, :, pl.ds(l0, band)], p.astype(BF16))
               + _dot(vt_meta, pm.astype(BF16)))
        return o_t / den

    def pair_body(i, carry):
        oa, ob = row_out_t(2 * i), row_out_t(2 * i + 1)
        sel_a = jnp.where(sub < dh, oa, pltpu.roll(oa, dh, 1))
        sel_b = jnp.where(sub < dh, pltpu.roll(ob, dh, 1), ob)
        ot_ref[:, pl.ds(pl.multiple_of(i * LANES, LANES), LANES)] = jnp.where(lane < dh, sel_a, sel_b)
        return carry

    lax.fori_loop(0, n_grid_rows // 2, pair_body, 0, unroll=2)

    for c in range(n_tblocks):
        out_ref[0, pl.ds(N_META + c * LANES, LANES), :] = ot_ref[:, c * LANES:(c + 1) * LANES].T.astype(BF16)


def _na_call(h, gn, w, qg, kg, bias):
    bsz, n_rows, d = h.shape
    n_pairs, _, nw = w.shape
    pw = 2 * NA_HEAD_DIM
    band = NA_WIN_ROWS * GRID_W
    return pl.pallas_call(
        _na_kernel,
        grid=(bsz, n_pairs),
        in_specs=[
            pl.BlockSpec((1, n_rows, d), lambda b, hp: (b, 0, 0)),
            pl.BlockSpec((1, d), lambda b, hp: (0, 0)),
            pl.BlockSpec((1, d, nw), lambda b, hp: (hp, 0, 0)),
            pl.BlockSpec((1, pw), lambda b, hp: (0, 0)),
            pl.BlockSpec((1, pw), lambda b, hp: (0, 0)),
            pl.BlockSpec((1, NA_WIN_ROWS, band, pw), lambda b, hp: (hp, 0, 0, 0)),
        ],
        out_specs=pl.BlockSpec((1, n_rows, pw), lambda b, hp: (b, 0, hp)),
        out_shape=jax.ShapeDtypeStruct((bsz, n_rows, n_pairs * pw), BF16),
        scratch_shapes=[
            pltpu.VMEM((n_rows, d), BF16),
            pltpu.VMEM((n_rows, nw), F32),
            pltpu.VMEM((n_rows, pw), BF16),
            pltpu.VMEM((n_rows, pw), BF16),
            pltpu.VMEM((2, pw, n_rows - N_META), BF16),
            pltpu.VMEM((pw, n_rows - N_META), F32),
        ],
        compiler_params=_cparams(2),
        name="neighbourhood_attention",
    )(h, gn, w, qg, kg, bias)


def _lane_prefix_exclusive(x, tri_strict):
    n_blocks = x.shape[1] // LANES
    carry = jnp.zeros((x.shape[0], 1), F32)
    pieces = []
    for j in range(n_blocks):
        blk = x[:, j * LANES:(j + 1) * LANES]
        pieces.append(_dot(blk.astype(BF16), tri_strict) + carry)
        carry = carry + jnp.sum(blk, axis=-1, keepdims=True)
    return jnp.concatenate(pieces, axis=1)


def _aligned_row_blocks(n_rows, step=512):
    blocks, r0 = [], 0
    while r0 < n_rows:
        nr = min(step, n_rows - r0)
        blocks.append((r0, nr))
        r0 += nr
    return blocks


def _post_kernel(n_mix, cap, *refs):
    h_ref = refs[0]
    mix_refs = refs[1:1 + n_mix]
    wo_ref, fg_ref, wr_ref = refs[1 + n_mix:4 + n_mix]
    hn_ref, u2_ref, slot_ref, gate_ref = refs[4 + n_mix:8 + n_mix]
    lg_ref = refs[8 + n_mix]
    n_rows = h_ref.shape[1]
    n_pad = u2_ref.shape[1]

    lg_ref[...] = jnp.zeros(lg_ref.shape, F32)
    for r0, nr in _aligned_row_blocks(n_rows):
        acc = h_ref[0, pl.ds(r0, nr), :]
        k0 = 0
        for m_ref in mix_refs:
            kw = m_ref.shape[2]
            acc = acc + _dot(m_ref[0, pl.ds(r0, nr), :], wo_ref[pl.ds(k0, kw), :])
            k0 += kw
        hn_ref[0, pl.ds(r0, nr), :] = acc
        u = _rms(acc, fg_ref[...])
        u2_ref[0, pl.ds(r0, nr), :] = u.astype(BF16)
        lg_ref[:, pl.ds(r0, nr)] = lax.dot_general(
            wr_ref[...], u, (((1,), (1,)), ((), ())),
            preferred_element_type=F32, precision=lax.Precision.HIGHEST)
    u2_ref[0, pl.ds(n_rows, n_pad - n_rows), :] = jnp.zeros((n_pad - n_rows, u2_ref.shape[2]), BF16)

    logits = lg_ref[...]
    ex = jnp.exp(logits - jnp.max(logits, axis=0, keepdims=True))
    aff = ex / jnp.sum(ex, axis=0, keepdims=True)
    tok = lax.broadcasted_iota(jnp.int32, aff.shape, 1)
    aff = jnp.where(tok < n_rows, aff, -1.0)

    def count_ge(x):
        return jnp.sum(jnp.where(aff >= x, 1.0, 0.0), axis=-1, keepdims=True)

    capf = float(cap)
    tiny = jnp.full((aff.shape[0], 1), 2.0 ** -126, F32)
    ok0 = count_ge(tiny) >= capf
    p = tiny
    for j in (64, 32, 16, 8, 4, 2, 1):
        cand = p * (2.0 ** j)
        p = jnp.where(count_ge(cand) >= capf, cand, p)
    m = p
    for i in range(1, 24):
        cand = m + p * (2.0 ** -i)
        m = jnp.where(count_ge(cand) >= capf, cand, m)
    thr = jnp.where(ok0, m, 0.0)

    gt = aff > thr
    eq = aff == thr
    need = capf - jnp.sum(jnp.where(gt, 1.0, 0.0), axis=-1, keepdims=True)
    r = lax.broadcasted_iota(jnp.int32, (LANES, LANES), 0)
    c = lax.broadcasted_iota(jnp.int32, (LANES, LANES), 1)
    tri_strict = jnp.where(r < c, 1.0, 0.0).astype(BF16)
    eq_rank = _lane_prefix_exclusive(jnp.where(eq, 1.0, 0.0), tri_strict)
    sel = gt | (eq & (eq_rank < need))
    pos = _lane_prefix_exclusive(jnp.where(sel, 1.0, 0.0), tri_strict)
    slot_ref[0] = jnp.where(sel, pos, -1.0).astype(jnp.int32)
    gate_ref[0] = jnp.where(sel, aff, 0.0)


def _post_call(h, mixes, wo, fg, wr, cap):
    bsz, n_rows, d = h.shape
    n_pad = ((n_rows + LANES - 1) // LANES) * LANES
    n_mix = len(mixes)
    row_spec = lambda w: pl.BlockSpec((1, n_rows, w), lambda b: (b, 0, 0), pipeline_mode=pl.Buffered(1))
    return pl.pallas_call(
        functools.partial(_post_kernel, n_mix, cap),
        grid=(bsz,),
        in_specs=[row_spec(d)] + [row_spec(m.shape[2]) for m in mixes] + [
            pl.BlockSpec((d, d), lambda b: (0, 0)),
            pl.BlockSpec((1, d), lambda b: (0, 0)),
            pl.BlockSpec((N_EXPERTS, d), lambda b: (0, 0)),
        ],
        out_specs=[
            row_spec(d),
            pl.BlockSpec((1, n_pad, d), lambda b: (b, 0, 0), pipeline_mode=pl.Buffered(1)),
            pl.BlockSpec((1, N_EXPERTS, n_pad), lambda b: (b, 0, 0)),
            pl.BlockSpec((1, N_EXPERTS, n_pad), lambda b: (b, 0, 0)),
        ],
        out_shape=[
            jax.ShapeDtypeStruct((bsz, n_rows, d), F32),
            jax.ShapeDtypeStruct((bsz, n_pad, d), BF16),
            jax.ShapeDtypeStruct((bsz, N_EXPERTS, n_pad), jnp.int32),
            jax.ShapeDtypeStruct((bsz, N_EXPERTS, n_pad), F32),
        ],
        scratch_shapes=[pltpu.VMEM((N_EXPERTS, n_pad), F32)],
        compiler_params=_cparams(1),
        name="outproj_router",
    )(h, *mixes, wo, fg, wr)


def _gather_kernel(u2_ref, slot_ref, xs_ref):
    cap_pad = xs_ref.shape[2]
    n_pad = u2_ref.shape[1]
    srow_id = lax.broadcasted_iota(jnp.int32, (cap_pad, n_pad), 0)

    def body(e, carry):
        onehot = jnp.where(srow_id == slot_ref[0, pl.ds(e, 1), :], 1.0, 0.0).astype(BF16)
        xs_ref[0, e] = _dot(onehot, u2_ref[0]).astype(BF16)
        return carry

    lax.fori_loop(0, N_EXPERTS, body, 0)


def _gather_call(u2, slot, cap_pad):
    bsz, n_pad, d = u2.shape
    return pl.pallas_call(
        _gather_kernel,
        grid=(bsz,),
        in_specs=[
            pl.BlockSpec((1, n_pad, d), lambda b: (b, 0, 0)),
            pl.BlockSpec((1, N_EXPERTS, n_pad), lambda b: (b, 0, 0)),
        ],
        out_specs=pl.BlockSpec((1, N_EXPERTS, cap_pad, d), lambda b: (b, 0, 0, 0)),
        out_shape=jax.ShapeDtypeStruct((bsz, N_EXPERTS, cap_pad, d), BF16),
        compiler_params=_cparams(1),
        name="expert_gather",
    )(u2, slot)


def _expert_kernel(xs_ref, wg_ref, wu_ref, wd_ref, ys_ref, acc_ref):
    bsz, _, cap_pad, d = xs_ref.shape
    fc = pl.program_id(1)
    x = xs_ref[...].reshape(bsz * cap_pad, d)
    g = _dot(x, wg_ref[0].astype(BF16))
    u = _dot(x, wu_ref[0].astype(BF16))
    hdn = (g * _sigmoid(g) * u).astype(BF16)
    y = _dot(hdn, wd_ref[0].astype(BF16))

    @pl.when(fc == 0)
    def _():
        acc_ref[...] = y

    @pl.when(fc > 0)
    def _():
        acc_ref[...] += y

    @pl.when(fc == pl.num_programs(1) - 1)
    def _():
        ys_ref[...] = acc_ref[...].astype(BF16).reshape(ys_ref.shape)


def _expert_call(xs, wg, wu, wd, layer, f_chunks=4):
    bsz, n_e, cap_pad, d = xs.shape
    ff = wg.shape[-1]
    fb = ff // f_chunks
    return pl.pallas_call(
        _expert_kernel,
        grid=(n_e, f_chunks),
        in_specs=[
            pl.BlockSpec((bsz, 1, cap_pad, d), lambda e, f: (0, e, 0, 0)),
            pl.BlockSpec((1, None, d, fb), lambda e, f: (layer, e, 0, f)),
            pl.BlockSpec((1, None, d, fb), lambda e, f: (layer, e, 0, f)),
            pl.BlockSpec((1, None, fb, d), lambda e, f: (layer, e, f, 0)),
        ],
        out_specs=pl.BlockSpec((bsz, 1, cap_pad, d), lambda e, f: (0, e, 0, 0)),
        out_shape=jax.ShapeDtypeStruct((bsz, n_e, cap_pad, d), BF16),
        scratch_shapes=[pltpu.VMEM((bsz * cap_pad, d), F32)],
        compiler_params=_cparams(2),
        name="expert_swiglu",
    )(xs, wg, wu, wd)


def _combine_kernel(h_ref, ys_ref, slot_ref, gate_ref, out_ref):
    n_rows = h_ref.shape[1]
    cap_pad = ys_ref.shape[2]
    n_pad = slot_ref.shape[2]
    e = pl.program_id(1)

    @pl.when(e == 0)
    def _():
        out_ref[0] = h_ref[0]

    srow_id = lax.broadcasted_iota(jnp.int32, (cap_pad, n_pad), 0)
    hit = srow_id == slot_ref[0, pl.ds(e, 1), :]
    weights = jnp.where(hit, gate_ref[0, pl.ds(e, 1), :], 0.0).astype(BF16)
    for t0, tn in _aligned_row_blocks(n_pad, 1024):
        nr = min(tn, n_rows - t0)
        part = _dot_tn(weights[:, t0:t0 + tn], ys_ref[0, 0])
        out_ref[0, pl.ds(t0, nr), :] += part[0:nr, :]


def _combine_call(h, ys, slot, gate):
    bsz, n_rows, d = h.shape
    _, n_e, cap_pad, _ = ys.shape
    n_pad = slot.shape[2]
    return pl.pallas_call(
        _combine_kernel,
        grid=(bsz, n_e),
        in_specs=[
            pl.BlockSpec((1, n_rows, d), lambda b, e: (b, 0, 0)),
            pl.BlockSpec((1, 1, cap_pad, d), lambda b, e: (b, e, 0, 0)),
            pl.BlockSpec((1, n_e, n_pad), lambda b, e: (b, 0, 0)),
            pl.BlockSpec((1, n_e, n_pad), lambda b, e: (b, 0, 0)),
        ],
        out_specs=pl.BlockSpec((1, n_rows, d), lambda b, e: (b, 0, 0)),
        out_shape=jax.ShapeDtypeStruct((bsz, n_rows, d), F32),
        compiler_params=_cparams(2),
        name="expert_combine",
    )(h, ys, slot, gate)


def _even_weights(w_in, conv_w, gate_b, decay_logit):
    d = w_in.shape[0]
    mw = N_HEADS * HEAD_DIM
    mq, mk, mv, mo = (w_in[:, i * mw:(i + 1) * mw] for i in range(4))
    gates = w_in[:, 4 * mw:4 * mw + 4 * N_HEADS]
    r0 = 4 * mw + 4 * N_HEADS
    rq, rk, rv, rg = (w_in[:, r0 + i * mw:r0 + (i + 1) * mw] for i in range(4))

    def per_head(t):
        return t.reshape(d, N_HEADS, HEAD_DIM).transpose(1, 0, 2)

    def gate_cols(fw_off, bw_off):
        cols = jnp.stack([gates[:, fw_off:fw_off + N_HEADS], gates[:, bw_off:bw_off + N_HEADS]], axis=-1)
        cols = cols.transpose(1, 0, 2)
        return jnp.pad(cols, ((0, 0), (0, 0), (0, LANES - 2)))

    def gate_bias(fw_off, bw_off):
        b = jnp.stack([gate_b[fw_off:fw_off + N_HEADS], gate_b[bw_off:bw_off + N_HEADS]], axis=-1)
        return jnp.pad(b, ((0, 0), (0, LANES - 2)))

    w_m = jnp.concatenate([per_head(mq), per_head(mk), per_head(mv), per_head(mo),
                           gate_cols(0, 2 * N_HEADS), gate_cols(N_HEADS, 3 * N_HEADS)], axis=-1).astype(BF16)
    gb = jnp.concatenate([gate_bias(0, 2 * N_HEADS), gate_bias(N_HEADS, 3 * N_HEADS)], axis=-1)[:, None, :]
    conv = jnp.concatenate([conv_w[:, :mw].reshape(CONV_K, N_HEADS, HEAD_DIM),
                            conv_w[:, mw:].reshape(CONV_K, N_HEADS, HEAD_DIM)], axis=-1).transpose(1, 0, 2)
    w_r = jnp.concatenate([per_head(rq), per_head(rk), per_head(rv), per_head(rg)], axis=-1).astype(BF16)
    dl = jnp.pad(decay_logit.T, ((0, 0), (0, LANES - 2)))[:, None, :]
    return w_m, gb.astype(F32), conv.astype(F32), w_r, dl.astype(F32)


def _rotary_tables(n_rows):
    half = HEAD_DIM // 2
    inv = ROPE_BASE ** (-jnp.arange(half, dtype=F32) / half)
    ang = jnp.arange(n_rows, dtype=F32)[:, None] * inv[None, :]
    cos, sin = jnp.cos(ang), jnp.sin(ang)
    return jnp.concatenate([cos, cos], axis=-1), jnp.concatenate([-sin, sin], axis=-1)


def _na_weights(w_in):
    d = w_in.shape[0]
    n_heads = w_in.shape[1] // (3 * NA_HEAD_DIM)
    z = w_in.reshape(d, 3, n_heads // 2, 2 * NA_HEAD_DIM)
    return z.transpose(2, 0, 1, 3).reshape(n_heads // 2, d, 6 * NA_HEAD_DIM).astype(BF16)


def _na_bias_table(rpb):
    col = jnp.arange(GRID_W)
    col_start = jnp.clip(col - NA_WIN_COLS // 2, 0, GRID_W - NA_WIN_COLS)
    col_in = (col[None, :] >= col_start[:, None]) & (col[None, :] < col_start[:, None] + NA_WIN_COLS)
    dc_idx = jnp.clip(col[None, :] - col[:, None], -(NA_WIN_COLS - 1), NA_WIN_COLS - 1) + NA_WIN_COLS - 1
    rpb_cols = rpb.astype(F32)[:, :, dc_idx]
    off = jnp.arange(NA_WIN_ROWS)
    dr_idx = jnp.arange(NA_WIN_ROWS)[None, :] - off[:, None] + NA_WIN_ROWS - 1
    tbl = rpb_cols[:, dr_idx]
    tbl = jnp.where(col_in[None, None, None], tbl, NEG)
    n_pairs = rpb.shape[0] // 2
    tbl = tbl.reshape(n_pairs, 2, NA_WIN_ROWS, NA_WIN_ROWS, GRID_W, GRID_W).transpose(0, 2, 3, 5, 1, 4)
    return tbl.reshape(n_pairs, NA_WIN_ROWS, NA_WIN_ROWS * GRID_W, 2 * GRID_W)


def _ffn(h, mixes, wo, fg, wr, wg, wu, wd, layer):
    n_rows = h.shape[1]
    cap = CAP_FACTOR * n_rows // N_EXPERTS
    cap_pad = ((cap + BF16_ROWS - 1) // BF16_ROWS) * BF16_ROWS
    hn, u2, slot, gate = _post_call(h, mixes, wo.astype(BF16), fg[None, :], wr.T, cap)
    xs = _gather_call(u2, slot, cap_pad)
    ys = _expert_call(xs, wg, wu, wd, layer)
    return _combine_call(hn, ys, slot, gate)


def kernel(x, meta_tokens, attn_norm_g, ffn_norm_g, even_w_in, even_conv_w, even_gate_b, even_m_norm_g, even_ret_decay_logit, even_r_norm_g, even_w_out, odd_w_in, odd_q_norm_g, odd_k_norm_g, odd_rpb, odd_w_out, router_w, expert_w_gate, expert_w_up, expert_w_down):
    bsz = x.shape[0]
    depth = attn_norm_g.shape[0]
    meta = jnp.broadcast_to(meta_tokens.astype(x.dtype)[None], (bsz,) + meta_tokens.shape)
    h = jnp.concatenate([meta, x], axis=1)
    n_rows = h.shape[1]
    cos2, sin2 = _rotary_tables(n_rows)
    mw = N_HEADS * HEAD_DIM
    for layer in range(depth):
        j = layer // 2
        gn = attn_norm_g[layer][None, :]
        if layer % 2 == 0:
            w_m, gb, conv, w_r, dl = _even_weights(even_w_in[j], even_conv_w[j], even_gate_b[j],
                                                   even_ret_decay_logit[j])
            m_out = _mlstm_call(h, gn, w_m, conv, gb, even_m_norm_g[j].reshape(N_HEADS, 1, HEAD_DIM))
            r_out = _ret_call(h, gn, w_r, cos2, sin2, dl, even_r_norm_g[j].reshape(N_HEADS, 1, HEAD_DIM))
            mixes, wo = [m_out, r_out], even_w_out[j]
        else:
            a_out = _na_call(h, gn, _na_weights(odd_w_in[j]), jnp.tile(odd_q_norm_g[j], 2)[None, :],
                             jnp.tile(odd_k_norm_g[j], 2)[None, :], _na_bias_table(odd_rpb[j]))
            mixes, wo = [a_out], odd_w_out[j]
        h = _ffn(h, mixes, wo, ffn_norm_g[layer], router_w[layer],
                 expert_w_gate, expert_w_up, expert_w_down, layer)
    return h[:, N_META:]
```

```python
import functools

import jax
import jax.numpy as jnp
from jax import lax
from jax.experimental import pallas as pl
from jax.experimental.pallas import tpu as pltpu

F32 = jnp.float32
BF16 = jnp.bfloat16

LANES = 128
BF16_ROWS = 16
N_META = 16
GRID_W = 64
EPS = 1e-6
HEAD_DIM = 128
N_HEADS = 4
CONV_K = 5
CONV_PAD = 8
ROPE_BASE = 10000.0
NA_HEAD_DIM = 64
NA_WIN_ROWS = 8
NA_WIN_COLS = 16
N_EXPERTS = 16
CAP_FACTOR = 2
NEG = -1e30
SCAN_T = 256
VMEM_LIMIT = 56 * 1024 * 1024


def _cparams(n_axes):
    return pltpu.CompilerParams(
        dimension_semantics=("arbitrary",) * n_axes, vmem_limit_bytes=VMEM_LIMIT)


def _row_blocks(n_rows):
    for nb in (3, 2, 4, 6, 1):
        if n_rows % (8 * nb) == 0:
            step = n_rows // nb
            return [(i * step, step) for i in range(nb)]
    return [(0, n_rows)]


def _rms(x, g):
    return x * lax.rsqrt(jnp.mean(x * x, axis=-1, keepdims=True) + EPS) * g


def _sigmoid(x):
    return 1.0 / (1.0 + jnp.exp(-x))


def _log_sigmoid(x):
    return jnp.minimum(x, 0.0) - jnp.log(1.0 + jnp.exp(-jnp.abs(x)))


def _dot(a, b):
    return jnp.dot(a, b, preferred_element_type=F32)


def _dot_nt(a, b):
    return lax.dot_general(a, b, (((1,), (1,)), ((), ())), preferred_element_type=F32)


def _dot_tn(a, b):
    return lax.dot_general(a, b, (((0,), (0,)), ((), ())), preferred_element_type=F32)


def _split3(x):
    hi = x.astype(BF16)
    r1 = x - hi.astype(F32)
    mid = r1.astype(BF16)
    lo = (r1 - mid.astype(F32)).astype(BF16)
    return hi, mid, lo


def _tri_prefix(tri_bf16, x):
    hi, mid, lo = _split3(x)
    return _dot(tri_bf16, hi) + _dot(tri_bf16, mid) + _dot(tri_bf16, lo)


def _norm_to_scratch(h_ref, gn_ref, un_ref):
    n_rows = h_ref.shape[1]
    for r0, nr in _row_blocks(n_rows):
        x = h_ref[0, pl.ds(r0, nr), :]
        un_ref[pl.ds(r0, nr), :] = _rms(x, gn_ref[...]).astype(BF16)


def _project(un_ref, w_ref, z_ref):
    n_rows = un_ref.shape[0]
    for r0, nr in _row_blocks(n_rows):
        z_ref[pl.ds(r0, nr), :] = _dot(un_ref[pl.ds(r0, nr), :], w_ref[0])


def _tri_mask(n, lower):
    r = lax.broadcasted_iota(jnp.int32, (n, n), 0)
    c = lax.broadcasted_iota(jnp.int32, (n, n), 1)
    return (c <= r) if lower else (c >= r)


def _mlstm_chunk(qc, kc, vc, a_row, a_col, f_col, mask, state):
    c_st, n_st, g_prev = state
    qb, kb, vb = qc.astype(BF16), kc.astype(BF16), vc.astype(BF16)
    cm = jnp.max(jnp.where(mask, a_row, NEG), axis=-1, keepdims=True)
    g_col = jnp.maximum(g_prev, cm)
    dm = jnp.exp(jnp.where(mask, a_row - g_col, NEG))
    s = _dot_nt(qb, kb) * dm
    w_inter = jnp.exp(g_prev - g_col)
    num = w_inter * _dot(qb, c_st.astype(BF16)) + _dot(s.astype(BF16), vb)
    den = (w_inter * jnp.sum(qc * n_st, axis=-1, keepdims=True)
           + jnp.sum(s, axis=-1, keepdims=True))
    out = num / jnp.maximum(jnp.abs(den), jnp.exp(-(f_col + g_col)))
    g_end = jnp.maximum(g_prev, jnp.max(a_row, axis=-1, keepdims=True))
    kw = kc * jnp.exp(a_col - g_end)
    w_state = jnp.exp(g_prev - g_end)
    c_new = w_state * c_st + _dot_tn(kw.astype(BF16), vb)
    n_new = w_state * n_st + jnp.sum(kw, axis=0, keepdims=True)
    return out, (c_new, n_new, g_end)


def _mlstm_kernel(h_ref, gn_ref, w_ref, conv_ref, gb_ref, ng_ref, out_ref,
                  un_ref, z_ref, zc_ref, acc_ref, acol_ref, fcol_ref, atm_ref, atr_ref):
    n_rows = h_ref.shape[1]
    n_real = n_rows - N_META
    t = SCAN_T
    n_chunks = n_real // t

    @pl.when(pl.program_id(1) == 0)
    def _():
        _norm_to_scratch(h_ref, gn_ref, un_ref)

    _project(un_ref, w_ref, z_ref)

    zc_ref[pl.ds(0, CONV_PAD), :] = jnp.zeros((CONV_PAD, 2 * HEAD_DIM), F32)
    zc_ref[pl.ds(CONV_PAD + n_rows, CONV_PAD), :] = jnp.zeros((CONV_PAD, 2 * HEAD_DIM), F32)
    for r0, nr in _row_blocks(n_rows):
        zc_ref[pl.ds(CONV_PAD + r0, nr), :] = z_ref[pl.ds(r0, nr), 0:2 * HEAD_DIM]
    lane2 = lax.broadcasted_iota(jnp.int32, (1, 2 * HEAD_DIM), 1)
    qk_scale = jnp.where(lane2 >= HEAD_DIM, HEAD_DIM ** -0.5, 1.0).astype(F32)
    for r0, nr in _row_blocks(n_rows):
        acc = jnp.zeros((nr, 2 * HEAD_DIM), F32)
        for j in range(CONV_K):
            off = CONV_PAD - (CONV_K - 1) // 2 + j + r0
            acc = acc + zc_ref[pl.ds(off, nr), :] * conv_ref[0, pl.ds(j, 1), :]
        z_ref[pl.ds(r0, nr), 0:2 * HEAD_DIM] = acc * _sigmoid(acc) * qk_scale

    gi_off, gf_off = 4 * HEAD_DIM, 5 * HEAD_DIM
    bias_i = gb_ref[0, :, 0:LANES]
    bias_f = gb_ref[0, :, LANES:2 * LANES]
    lane = lax.broadcasted_iota(jnp.int32, (1, LANES), 1)
    tri_m = jnp.where(_tri_mask(N_META, True), 1.0, 0.0).astype(BF16)
    lf_m = _log_sigmoid(z_ref[pl.ds(0, N_META), gf_off:gf_off + LANES] + bias_f)
    f_meta = _tri_prefix(tri_m, lf_m)
    f_meta_end = f_meta[N_META - 1:N_META, :]
    tri_b = jnp.where(_tri_mask(LANES, True), 1.0, 0.0).astype(BF16)

    def prefix_body(c, carry):
        r0 = pl.multiple_of(N_META + c * LANES, 8)
        lf = _log_sigmoid(z_ref[pl.ds(r0, LANES), gf_off:gf_off + LANES] + bias_f)
        p = _tri_prefix(tri_b, lf) + carry
        fcol_ref[pl.ds(r0, LANES), :] = p
        return p[LANES - 1:LANES, :]

    total = lax.fori_loop(0, n_real // LANES, prefix_body, jnp.zeros((1, LANES), F32), unroll=4)

    fcol_ref[pl.ds(0, N_META), :] = f_meta
    acol_ref[pl.ds(0, N_META), :] = z_ref[pl.ds(0, N_META), gi_off:gi_off + LANES] + bias_i - f_meta

    def finish_body(c, carry):
        r0 = pl.multiple_of(N_META + c * LANES, 8)
        lf = _log_sigmoid(z_ref[pl.ds(r0, LANES), gf_off:gf_off + LANES] + bias_f)
        p = fcol_ref[pl.ds(r0, LANES), :]
        f = f_meta_end + jnp.where(lane == 0, p, total - p + lf)
        fcol_ref[pl.ds(r0, LANES), :] = f
        a = z_ref[pl.ds(r0, LANES), gi_off:gi_off + LANES] + bias_i - f
        acol_ref[pl.ds(r0, LANES), :] = a
        atr_ref[:, pl.ds(pl.multiple_of(c * LANES, LANES), LANES)] = a.T[0:8, :]
        return carry

    lax.fori_loop(0, n_real // LANES, finish_body, 0, unroll=4)
    atm_ref[...] = acol_ref[pl.ds(0, LANES), :].T[0:8, :]

    mask_m = _tri_mask(N_META, True)
    states = []
    for d in (0, 1):
        state = (jnp.zeros((HEAD_DIM, HEAD_DIM), F32), jnp.zeros((1, HEAD_DIM), F32),
                 jnp.zeros((1, 1), F32))
        out_m, state = _mlstm_chunk(
            z_ref[pl.ds(0, N_META), 0:HEAD_DIM], z_ref[pl.ds(0, N_META), HEAD_DIM:2 * HEAD_DIM],
            z_ref[pl.ds(0, N_META), 2 * HEAD_DIM:3 * HEAD_DIM],
            atm_ref[pl.ds(d, 1), 0:N_META], acol_ref[pl.ds(0, N_META), d:d + 1],
            fcol_ref[pl.ds(0, N_META), d:d + 1], mask_m, state)
        acc_ref[d, pl.ds(0, N_META), :] = out_m
        states.append(state)

    def body(i, carry):
        new = []
        for d in (0, 1):
            c = i if d == 0 else n_chunks - 1 - i
            r0 = pl.multiple_of(N_META + c * t, 8)
            l0 = pl.multiple_of(c * t, LANES)
            out, st = _mlstm_chunk(
                z_ref[pl.ds(r0, t), 0:HEAD_DIM], z_ref[pl.ds(r0, t), HEAD_DIM:2 * HEAD_DIM],
                z_ref[pl.ds(r0, t), 2 * HEAD_DIM:3 * HEAD_DIM],
                atr_ref[pl.ds(d, 1), pl.ds(l0, t)], acol_ref[pl.ds(r0, t), d:d + 1],
                fcol_ref[pl.ds(r0, t), d:d + 1], _tri_mask(t, d == 0), carry[d])
            acc_ref[d, pl.ds(r0, t), :] = out
            new.append(st)
        return tuple(new)

    lax.fori_loop(0, n_chunks, body, tuple(states))

    for r0, nr in _row_blocks(n_rows):
        hs = acc_ref[0, pl.ds(r0, nr), :] + acc_ref[1, pl.ds(r0, nr), :]
        o = z_ref[pl.ds(r0, nr), 3 * HEAD_DIM:4 * HEAD_DIM]
        out_ref[0, pl.ds(r0, nr), :] = (_sigmoid(o) * _rms(hs, ng_ref[0])).astype(BF16)


def _mlstm_call(h, gn, w, conv, gb, ng):
    bsz, n_rows, d = h.shape
    nw = w.shape[-1]
    n_real = n_rows - N_META
    return pl.pallas_call(
        _mlstm_kernel,
        grid=(bsz, N_HEADS),
        in_specs=[
            pl.BlockSpec((1, n_rows, d), lambda b, hd: (b, 0, 0)),
            pl.BlockSpec((1, d), lambda b, hd: (0, 0)),
            pl.BlockSpec((1, d, nw), lambda b, hd: (hd, 0, 0)),
            pl.BlockSpec((1, CONV_K, 2 * HEAD_DIM), lambda b, hd: (hd, 0, 0)),
            pl.BlockSpec((1, 1, 2 * LANES), lambda b, hd: (hd, 0, 0)),
            pl.BlockSpec((1, 1, HEAD_DIM), lambda b, hd: (hd, 0, 0)),
        ],
        out_specs=pl.BlockSpec((1, n_rows, HEAD_DIM), lambda b, hd: (b, 0, hd)),
        out_shape=jax.ShapeDtypeStruct((bsz, n_rows, N_HEADS * HEAD_DIM), BF16),
        scratch_shapes=[
            pltpu.VMEM((n_rows, d), BF16),
            pltpu.VMEM((n_rows, nw), F32),
            pltpu.VMEM((n_rows + 2 * CONV_PAD, 2 * HEAD_DIM), F32),
            pltpu.VMEM((2, n_rows, HEAD_DIM), F32),
            pltpu.VMEM((n_rows, LANES), F32),
            pltpu.VMEM((n_rows, LANES), F32),
            pltpu.VMEM((8, LANES), F32),
            pltpu.VMEM((8, n_real), F32),
        ],
        compiler_params=_cparams(2),
        name="mlstm_heads",
    )(h, gn, w, conv, gb, ng)


def _ret_chunk(qc, kc, vc, dmat, dq, dk, dchunk, r_st):
    qb, kb, vb = qc.astype(BF16), kc.astype(BF16), vc.astype(BF16)
    s = _dot_nt(qb, kb) * dmat
    out = _dot(s.astype(BF16), vb) + dq * _dot(qb, r_st.astype(BF16))
    r_new = dchunk * r_st + _dot_tn((kc * dk).astype(BF16), vb)
    return out, r_new


def _decay_tables(n, lg, forward):
    r = lax.broadcasted_iota(jnp.int32, (n, n), 0)
    c = lax.broadcasted_iota(jnp.int32, (n, n), 1)
    pos = lax.broadcasted_iota(jnp.int32, (n, 1), 0).astype(F32)
    if forward:
        dist, mask = (r - c).astype(F32), c <= r
        dq, dk = jnp.exp((pos + 1.0) * lg), jnp.exp((n - 1.0 - pos) * lg)
    else:
        dist, mask = (c - r).astype(F32), c >= r
        dq, dk = jnp.exp((n - pos) * lg), jnp.exp(pos * lg)
    dmat = jnp.exp(jnp.where(mask, dist * lg, NEG))
    return dmat, dq, dk, jnp.exp(n * lg)


def _ret_kernel(h_ref, gn_ref, w_ref, cos_ref, sin_ref, dl_ref, ng_ref, out_ref,
                un_ref, z_ref, acc_ref, dmat_ref, dqk_ref):
    n_rows = h_ref.shape[1]
    n_real = n_rows - N_META
    t = SCAN_T
    n_chunks = n_real // t

    @pl.when(pl.program_id(1) == 0)
    def _():
        _norm_to_scratch(h_ref, gn_ref, un_ref)

    _project(un_ref, w_ref, z_ref)

    for r0, nr in _row_blocks(n_rows):
        cs, sn = cos_ref[pl.ds(r0, nr), :], sin_ref[pl.ds(r0, nr), :]
        q = z_ref[pl.ds(r0, nr), 0:HEAD_DIM]
        z_ref[pl.ds(r0, nr), 0:HEAD_DIM] = q * cs + pltpu.roll(q, HEAD_DIM // 2, 1) * sn
        k = z_ref[pl.ds(r0, nr), HEAD_DIM:2 * HEAD_DIM]
        z_ref[pl.ds(r0, nr), HEAD_DIM:2 * HEAD_DIM] = (
            (k * cs + pltpu.roll(k, HEAD_DIM // 2, 1) * sn) * HEAD_DIM ** -0.5)

    lg_all = _log_sigmoid(dl_ref[0])
    states, dchunks = [], []
    for d in (0, 1):
        lg = lg_all[:, d:d + 1]
        dmat_m, _, dk_m, dch_m = _decay_tables(N_META, lg, True)
        dmat, dq, dk, dch = _decay_tables(t, lg, d == 0)
        dmat_ref[d] = dmat
        dqk_ref[d] = jnp.concatenate([jnp.broadcast_to(dq, (t, LANES)), jnp.broadcast_to(dk, (t, LANES))], axis=1)
        dchunks.append(dch)
        out_m, r_st = _ret_chunk(
            z_ref[pl.ds(0, N_META), 0:HEAD_DIM], z_ref[pl.ds(0, N_META), HEAD_DIM:2 * HEAD_DIM],
            z_ref[pl.ds(0, N_META), 2 * HEAD_DIM:3 * HEAD_DIM],
            dmat_m, jnp.zeros((N_META, 1), F32), dk_m, dch_m, jnp.zeros((HEAD_DIM, HEAD_DIM), F32))
        acc_ref[d, pl.ds(0, N_META), :] = out_m
        states.append(r_st)

    def body(i, carry):
        new = []
        for d in (0, 1):
            c = i if d == 0 else n_chunks - 1 - i
            r0 = pl.multiple_of(N_META + c * t, 8)
            out, st = _ret_chunk(
                z_ref[pl.ds(r0, t), 0:HEAD_DIM], z_ref[pl.ds(r0, t), HEAD_DIM:2 * HEAD_DIM],
                z_ref[pl.ds(r0, t), 2 * HEAD_DIM:3 * HEAD_DIM],
                dmat_ref[d], dqk_ref[d, :, 0:LANES], dqk_ref[d, :, LANES:2 * LANES], dchunks[d], carry[d])
            acc_ref[d, pl.ds(r0, t), :] = out
            new.append(st)
        return tuple(new)

    lax.fori_loop(0, n_chunks, body, tuple(states))

    for r0, nr in _row_blocks(n_rows):
        hs = acc_ref[0, pl.ds(r0, nr), :] + acc_ref[1, pl.ds(r0, nr), :]
        g = z_ref[pl.ds(r0, nr), 3 * HEAD_DIM:4 * HEAD_DIM]
        out_ref[0, pl.ds(r0, nr), :] = (g * _sigmoid(g) * _rms(hs, ng_ref[0])).astype(BF16)


def _ret_call(h, gn, w, cos2, sin2, dl, ng):
    bsz, n_rows, d = h.shape
    nw = w.shape[-1]
    return pl.pallas_call(
        _ret_kernel,
        grid=(bsz, N_HEADS),
        in_specs=[
            pl.BlockSpec((1, n_rows, d), lambda b, hd: (b, 0, 0)),
            pl.BlockSpec((1, d), lambda b, hd: (0, 0)),
            pl.BlockSpec((1, d, nw), lambda b, hd: (hd, 0, 0)),
            pl.BlockSpec((n_rows, HEAD_DIM), lambda b, hd: (0, 0)),
            pl.BlockSpec((n_rows, HEAD_DIM), lambda b, hd: (0, 0)),
            pl.BlockSpec((1, 1, LANES), lambda b, hd: (hd, 0, 0)),
            pl.BlockSpec((1, 1, HEAD_DIM), lambda b, hd: (hd, 0, 0)),
        ],
        out_specs=pl.BlockSpec((1, n_rows, HEAD_DIM), lambda b, hd: (b, 0, hd)),
        out_shape=jax.ShapeDtypeStruct((bsz, n_rows, N_HEADS * HEAD_DIM), BF16),
        scratch_shapes=[
            pltpu.VMEM((n_rows, d), BF16),
            pltpu.VMEM((n_rows, nw), F32),
            pltpu.VMEM((2, n_rows, HEAD_DIM), F32),
            pltpu.VMEM((2, SCAN_T, SCAN_T), F32),
            pltpu.VMEM((2, SCAN_T, 2 * LANES), F32),
        ],
        compiler_params=_cparams(2),
        name="retention_heads",
    )(h, gn, w, cos2, sin2, dl, ng)


def _head_pair_block_diag(qp):
    lane = lax.broadcasted_iota(jnp.int32, qp.shape, 1)
    zero = jnp.zeros_like(qp)
    return jnp.concatenate([jnp.where(lane < NA_HEAD_DIM, qp, zero),
                            jnp.where(lane >= NA_HEAD_DIM, qp, zero)], axis=0)


def _na_attend(qp, keys, values, bias_t):
    n = qp.shape[0]
    bd = _head_pair_block_diag(qp)
    scores = []
    for kb, bt in zip(keys, bias_t):
        s = _dot_nt(kb, bd)
        scores.append(s if bt is None else s + bt)
    m = scores[0].max(axis=0, keepdims=True)
    for s in scores[1:]:
        m = jnp.maximum(m, s.max(axis=0, keepdims=True))
    probs = [jnp.exp(s - m) for s in scores]
    den = probs[0].sum(axis=0, keepdims=True)
    for p in probs[1:]:
        den = den + p.sum(axis=0, keepdims=True)
    inv = 1.0 / den
    o2 = None
    for p, vb in zip(probs, values):
        part = _dot_tn((p * inv).astype(BF16), vb)
        o2 = part if o2 is None else o2 + part
    lane = lax.broadcasted_iota(jnp.int32, (n, 2 * NA_HEAD_DIM), 1)
    return jnp.where(lane < NA_HEAD_DIM, o2[0:n, :], o2[n:2 * n, :])


def _na_kernel(h_ref, gn_ref, w_ref, qkg_ref, bias_ref, out_ref,
               un_ref, z_ref, q_ref, k_ref, vt_ref, ot_ref, s_ref, p_ref, den_ref):
    n_rows = h_ref.shape[1]
    n_real = n_rows - N_META
    n_grid_rows = n_real // GRID_W
    dh = NA_HEAD_DIM
    pw = 2 * dh
    band = NA_WIN_ROWS * GRID_W

    @pl.when(pl.program_id(1) == 0)
    def _():
        _norm_to_scratch(h_ref, gn_ref, un_ref)

    _project(un_ref, w_ref, z_ref)

    r = lax.broadcasted_iota(jnp.int32, (2 * pw, 2 * pw), 0)
    c = lax.broadcasted_iota(jnp.int32, (2 * pw, 2 * pw), 1)
    head_ones = jnp.where(r // dh == c // dh, 1.0, 0.0).astype(BF16)
    for r0, nr in _row_blocks(n_rows):
        x = z_ref[pl.ds(r0, nr), 0:2 * pw]
        sq = x * x
        hi = sq.astype(BF16)
        lo = (sq - hi.astype(F32)).astype(BF16)
        ssq = _dot(hi, head_ones) + _dot(lo, head_ones)
        y = (x * lax.rsqrt(ssq * (1.0 / dh) + EPS) * qkg_ref[...]).astype(BF16)
        q_ref[pl.ds(r0, nr), :] = y[:, 0:pw]
        k_ref[pl.ds(r0, nr), :] = y[:, pw:2 * pw]

    n_tblocks = n_real // LANES
    for cpy in (0, 1):
        for c in range(n_tblocks - cpy):
            vb = z_ref[pl.ds(N_META + cpy * GRID_W + c * LANES, LANES), 2 * pw:3 * pw]
            vt_ref[cpy, :, c * LANES:(c + 1) * LANES] = vb.T.astype(BF16)
    vt_ref[1, :, (n_tblocks - 1) * LANES:n_tblocks * LANES] = jnp.zeros((pw, LANES), BF16)
    v_meta = z_ref[pl.ds(0, N_META), 2 * pw:3 * pw].astype(BF16)
    vt_meta = z_ref[pl.ds(0, LANES), 2 * pw:3 * pw].T[:, 0:N_META].astype(BF16)

    k_meta = k_ref[pl.ds(0, N_META), :]
    out_ref[0, pl.ds(0, N_META), :] = _na_attend(
        q_ref[pl.ds(0, N_META), :], [k_meta], [v_meta], [None]).astype(BF16)

    sub = lax.broadcasted_iota(jnp.int32, (pw, pw), 0)
    lane = lax.broadcasted_iota(jnp.int32, (pw, pw), 1)

    def row_start(r):
        return jnp.clip(r - NA_WIN_ROWS // 2, 0, n_grid_rows - NA_WIN_ROWS)

    def stage_scores(r, slot):
        rs = row_start(r)
        q0 = pl.multiple_of(N_META + r * GRID_W, 8)
        k0 = pl.multiple_of(N_META + rs * GRID_W, 8)
        bd = _head_pair_block_diag(q_ref[pl.ds(q0, GRID_W), :])
        bias = bias_ref[0, pl.ds(NA_WIN_ROWS - 1 - (r - rs), NA_WIN_ROWS)].reshape(band, pw)
        s_ref[slot, pl.ds(0, band), :] = _dot_nt(k_ref[pl.ds(k0, band), :], bd) + bias
        s_ref[slot, pl.ds(band, N_META), :] = _dot_nt(k_meta, bd)

    def stage_softmax(slot):
        s = s_ref[slot]
        p = jnp.exp(s - s.max(axis=0, keepdims=True))
        den_ref[slot] = p.sum(axis=0, keepdims=True)
        p_ref[slot] = p.astype(BF16)

    def stage_values(r, slot):
        rs = row_start(r)
        par = rs % 2
        l0 = pl.multiple_of((rs - par) * GRID_W, LANES)
        o_t = (_dot(vt_ref[par, :, pl.ds(l0, band)], p_ref[slot, pl.ds(0, band), :])
               + _dot(vt_meta, p_ref[slot, pl.ds(band, N_META), :]))
        ot_ref[r] = o_t / den_ref[slot]

    stage_scores(0, 0)
    stage_softmax(0)
    stage_scores(1, 1)

    def pipe_body(j, carry):
        i0 = 2 + 2 * j
        stage_values(i0 - 2, 0)
        stage_softmax(1)
        stage_scores(i0, 0)
        stage_values(i0 - 1, 1)
        stage_softmax(0)
        stage_scores(i0 + 1, 1)
        return carry

    lax.fori_loop(0, (n_grid_rows - 2) // 2, pipe_body, 0)
    stage_values(n_grid_rows - 2, 0)
    stage_softmax(1)
    stage_values(n_grid_rows - 1, 1)

    for c in range(n_tblocks):
        oa, ob = ot_ref[2 * c], ot_ref[2 * c + 1]
        sel_a = jnp.where(sub < dh, oa, pltpu.roll(oa, dh, 1))
        sel_b = jnp.where(sub < dh, pltpu.roll(ob, dh, 1), ob)
        out_ref[0, pl.ds(N_META + c * LANES, LANES), :] = jnp.where(lane < dh, sel_a, sel_b).T.astype(BF16)


def _na_call(h, gn, w, qg, kg, bias):
    bsz, n_rows, d = h.shape
    n_pairs, _, nw = w.shape
    pw = 2 * NA_HEAD_DIM
    band = NA_WIN_ROWS * GRID_W
    qkg = jnp.concatenate([jnp.tile(qg, 2) * NA_HEAD_DIM ** -0.5, jnp.tile(kg, 2)])[None, :].astype(F32)
    return pl.pallas_call(
        _na_kernel,
        grid=(bsz, n_pairs),
        in_specs=[
            pl.BlockSpec((1, n_rows, d), lambda b, hp: (b, 0, 0)),
            pl.BlockSpec((1, d), lambda b, hp: (0, 0)),
            pl.BlockSpec((1, d, nw), lambda b, hp: (hp, 0, 0)),
            pl.BlockSpec((1, 2 * pw), lambda b, hp: (0, 0)),
            pl.BlockSpec((1,) + bias.shape[1:], lambda b, hp: (hp, 0, 0, 0)),
        ],
        out_specs=pl.BlockSpec((1, n_rows, pw), lambda b, hp: (b, 0, hp)),
        out_shape=jax.ShapeDtypeStruct((bsz, n_rows, n_pairs * pw), BF16),
        scratch_shapes=[
            pltpu.VMEM((n_rows, d), BF16),
            pltpu.VMEM((n_rows, nw), F32),
            pltpu.VMEM((n_rows, pw), BF16),
            pltpu.VMEM((n_rows, pw), BF16),
            pltpu.VMEM((2, pw, n_rows - N_META), BF16),
            pltpu.VMEM(((n_rows - N_META) // GRID_W, pw, pw), F32),
            pltpu.VMEM((2, band + N_META, pw), F32),
            pltpu.VMEM((2, band + N_META, pw), BF16),
            pltpu.VMEM((2, 1, pw), F32),
        ],
        compiler_params=_cparams(2),
        name="neighbourhood_attention",
    )(h, gn, w, qkg, bias)


def _lane_prefix_exclusive(x, tri_strict):
    n_blocks = x.shape[1] // LANES
    carry = jnp.zeros((x.shape[0], 1), F32)
    pieces = []
    for j in range(n_blocks):
        blk = x[:, j * LANES:(j + 1) * LANES]
        pieces.append(_dot(blk.astype(BF16), tri_strict) + carry)
        carry = carry + jnp.sum(blk, axis=-1, keepdims=True)
    return jnp.concatenate(pieces, axis=1)


def _aligned_row_blocks(n_rows, step=512):
    blocks, r0 = [], 0
    while r0 < n_rows:
        nr = min(step, n_rows - r0)
        blocks.append((r0, nr))
        r0 += nr
    return blocks


def _post_kernel(n_mix, cap, *refs):
    h_ref = refs[0]
    mix_refs = refs[1:1 + n_mix]
    wo_ref, fg_ref, wr_ref = refs[1 + n_mix:4 + n_mix]
    hn_ref, u2_ref, slot_ref, gate_ref = refs[4 + n_mix:8 + n_mix]
    lgc_ref = refs[8 + n_mix]
    n_rows = h_ref.shape[1]
    n_pad = u2_ref.shape[1]

    for r0, nr in _row_blocks(n_rows):
        acc = h_ref[0, pl.ds(r0, nr), :]
        k0 = 0
        for m_ref in mix_refs:
            kw = m_ref.shape[2]
            acc = acc + _dot(m_ref[0, pl.ds(r0, nr), :], wo_ref[pl.ds(k0, kw), :])
            k0 += kw
        hn_ref[0, pl.ds(r0, nr), :] = acc
        u = _rms(acc, fg_ref[...])
        u_hi = u.astype(BF16)
        u2_ref[0, pl.ds(r0, nr), :] = u_hi
        u_lo = (u - u_hi.astype(F32)).astype(BF16)
        hh_hl = _dot(u_hi, wr_ref[...])
        lgc_ref[pl.ds(r0, nr), :] = (hh_hl[:, 0:LANES] + hh_hl[:, LANES:2 * LANES]
                                     + _dot(u_lo, wr_ref[:, 0:LANES]))
    u2_ref[0, pl.ds(n_rows, n_pad - n_rows), :] = jnp.zeros((n_pad - n_rows, u2_ref.shape[2]), BF16)
    lgc_ref[pl.ds(n_rows, n_pad - n_rows), :] = jnp.zeros((n_pad - n_rows, LANES), F32)

    logits = jnp.concatenate(
        [lgc_ref[pl.ds(c * LANES, LANES), :].T[0:N_EXPERTS, :] for c in range(n_pad // LANES)], axis=1)
    ex = jnp.exp(logits - jnp.max(logits, axis=0, keepdims=True))
    aff = ex / jnp.sum(ex, axis=0, keepdims=True)
    tok = lax.broadcasted_iota(jnp.int32, aff.shape, 1)
    aff = jnp.where(tok < n_rows, aff, -1.0)

    def count_ge(x):
        return jnp.sum(jnp.where(aff >= x, 1.0, 0.0), axis=-1, keepdims=True)

    capf = float(cap)
    tiny = jnp.full((aff.shape[0], 1), 2.0 ** -126, F32)
    ok0 = count_ge(tiny) >= capf
    p = tiny
    for j in (64, 32, 16, 8, 4, 2, 1):
        cand = p * (2.0 ** j)
        p = jnp.where(count_ge(cand) >= capf, cand, p)
    m = p
    for i in range(1, 24):
        cand = m + p * (2.0 ** -i)
        m = jnp.where(count_ge(cand) >= capf, cand, m)
    thr = jnp.where(ok0, m, 0.0)

    gt = aff > thr
    eq = aff == thr
    need = capf - jnp.sum(jnp.where(gt, 1.0, 0.0), axis=-1, keepdims=True)
    r = lax.broadcasted_iota(jnp.int32, (LANES, LANES), 0)
    c = lax.broadcasted_iota(jnp.int32, (LANES, LANES), 1)
    tri_strict = jnp.where(r < c, 1.0, 0.0).astype(BF16)
    eq_rank = _lane_prefix_exclusive(jnp.where(eq, 1.0, 0.0), tri_strict)
    sel = gt | (eq & (eq_rank < need))
    pos = _lane_prefix_exclusive(jnp.where(sel, 1.0, 0.0), tri_strict)
    slot_ref[0] = jnp.where(sel, pos, -1.0).astype(jnp.int32)
    gate_ref[0] = jnp.where(sel, aff, 0.0)


def _post_call(h, mixes, wo, fg, wr, cap):
    bsz, n_rows, d = h.shape
    n_pad = ((n_rows + LANES - 1) // LANES) * LANES
    n_mix = len(mixes)
    row_spec = lambda w: pl.BlockSpec((1, n_rows, w), lambda b: (b, 0, 0), pipeline_mode=pl.Buffered(1))
    return pl.pallas_call(
        functools.partial(_post_kernel, n_mix, cap),
        grid=(bsz,),
        in_specs=[row_spec(d)] + [row_spec(m.shape[2]) for m in mixes] + [
            pl.BlockSpec((d, d), lambda b: (0, 0)),
            pl.BlockSpec((1, d), lambda b: (0, 0)),
            pl.BlockSpec((d, 2 * LANES), lambda b: (0, 0)),
        ],
        out_specs=[
            row_spec(d),
            pl.BlockSpec((1, n_pad, d), lambda b: (b, 0, 0), pipeline_mode=pl.Buffered(1)),
            pl.BlockSpec((1, N_EXPERTS, n_pad), lambda b: (b, 0, 0)),
            pl.BlockSpec((1, N_EXPERTS, n_pad), lambda b: (b, 0, 0)),
        ],
        out_shape=[
            jax.ShapeDtypeStruct((bsz, n_rows, d), F32),
            jax.ShapeDtypeStruct((bsz, n_pad, d), BF16),
            jax.ShapeDtypeStruct((bsz, N_EXPERTS, n_pad), jnp.int32),
            jax.ShapeDtypeStruct((bsz, N_EXPERTS, n_pad), F32),
        ],
        scratch_shapes=[pltpu.VMEM((n_pad, LANES), F32)],
        compiler_params=_cparams(1),
        name="outproj_router",
    )(h, *mixes, wo, fg, wr)


def _gather_kernel(u2_ref, slot_ref, xs_ref):
    cap_pad = xs_ref.shape[2]
    n_pad = u2_ref.shape[1]
    srow_id = lax.broadcasted_iota(jnp.int32, (cap_pad, n_pad), 0)

    def body(e, carry):
        onehot = jnp.where(srow_id == slot_ref[0, pl.ds(e, 1), :], 1.0, 0.0).astype(BF16)
        xs_ref[0, e] = _dot(onehot, u2_ref[0]).astype(BF16)
        return carry

    lax.fori_loop(0, N_EXPERTS, body, 0)


def _gather_call(u2, slot, cap_pad):
    bsz, n_pad, d = u2.shape
    return pl.pallas_call(
        _gather_kernel,
        grid=(bsz,),
        in_specs=[
            pl.BlockSpec((1, n_pad, d), lambda b: (b, 0, 0)),
            pl.BlockSpec((1, N_EXPERTS, n_pad), lambda b: (b, 0, 0)),
        ],
        out_specs=pl.BlockSpec((1, N_EXPERTS, cap_pad, d), lambda b: (b, 0, 0, 0)),
        out_shape=jax.ShapeDtypeStruct((bsz, N_EXPERTS, cap_pad, d), BF16),
        compiler_params=_cparams(1),
        name="expert_gather",
    )(u2, slot)


def _expert_kernel(xs_ref, wg_ref, wu_ref, wd_ref, ys_ref, acc_ref):
    bsz, _, cap_pad, d = xs_ref.shape
    fc = pl.program_id(1)
    x = xs_ref[...].reshape(bsz * cap_pad, d)
    g = _dot(x, wg_ref[0].astype(BF16))
    u = _dot(x, wu_ref[0].astype(BF16))
    hdn = (g * _sigmoid(g) * u).astype(BF16)
    y = _dot(hdn, wd_ref[0].astype(BF16))

    @pl.when(fc == 0)
    def _():
        acc_ref[...] = y

    @pl.when(fc > 0)
    def _():
        acc_ref[...] += y

    @pl.when(fc == pl.num_programs(1) - 1)
    def _():
        ys_ref[...] = acc_ref[...].astype(BF16).reshape(ys_ref.shape)


def _expert_call(xs, wg, wu, wd, layer, f_chunks=4):
    bsz, n_e, cap_pad, d = xs.shape
    ff = wg.shape[-1]
    fb = ff // f_chunks
    return pl.pallas_call(
        _expert_kernel,
        grid=(n_e, f_chunks),
        in_specs=[
            pl.BlockSpec((bsz, 1, cap_pad, d), lambda e, f: (0, e, 0, 0)),
            pl.BlockSpec((1, None, d, fb), lambda e, f: (layer, e, 0, f)),
            pl.BlockSpec((1, None, d, fb), lambda e, f: (layer, e, 0, f)),
            pl.BlockSpec((1, None, fb, d), lambda e, f: (layer, e, f, 0)),
        ],
        out_specs=pl.BlockSpec((bsz, 1, cap_pad, d), lambda e, f: (0, e, 0, 0)),
        out_shape=jax.ShapeDtypeStruct((bsz, n_e, cap_pad, d), BF16),
        scratch_shapes=[pltpu.VMEM((bsz * cap_pad, d), F32)],
        compiler_params=_cparams(2),
        name="expert_swiglu",
    )(xs, wg, wu, wd)


def _combine_kernel(h_ref, ys_ref, slot_ref, gate_ref, out_ref):
    n_rows = h_ref.shape[1]
    n_out = out_ref.shape[1]
    skip = n_rows - n_out
    _, e_grp, cap_pad, d = ys_ref.shape
    n_pad = slot_ref.shape[2]
    g = pl.program_id(1)

    @pl.when(g == 0)
    def _():
        out_ref[0] = h_ref[0, pl.ds(skip, n_out), :]

    srow_id = lax.broadcasted_iota(jnp.int32, (cap_pad, n_pad), 0)
    pieces = []
    for j in range(e_grp):
        e = g * e_grp + j
        hit = srow_id == slot_ref[0, pl.ds(e, 1), :]
        pieces.append(jnp.where(hit, gate_ref[0, pl.ds(e, 1), :], 0.0).astype(BF16))
    weights = jnp.concatenate(pieces, axis=0)
    ys = ys_ref[0].reshape(e_grp * cap_pad, d)
    for t0, tn in _aligned_row_blocks(n_pad, 1024):
        lo, hi = max(t0, skip), min(t0 + tn, n_rows)
        if hi <= lo:
            continue
        part = _dot_tn(weights[:, t0:t0 + tn], ys)
        out_ref[0, pl.ds(lo - skip, hi - lo), :] += part[lo - t0:hi - t0, :]


def _combine_call(h, ys, slot, gate, n_out, e_grp=4):
    bsz, n_rows, d = h.shape
    _, n_e, cap_pad, _ = ys.shape
    n_pad = slot.shape[2]
    return pl.pallas_call(
        _combine_kernel,
        grid=(bsz, n_e // e_grp),
        in_specs=[
            pl.BlockSpec((1, n_rows, d), lambda b, g: (b, 0, 0)),
            pl.BlockSpec((1, e_grp, cap_pad, d), lambda b, g: (b, g, 0, 0)),
            pl.BlockSpec((1, n_e, n_pad), lambda b, g: (b, 0, 0)),
            pl.BlockSpec((1, n_e, n_pad), lambda b, g: (b, 0, 0)),
        ],
        out_specs=pl.BlockSpec((1, n_out, d), lambda b, g: (b, 0, 0)),
        out_shape=jax.ShapeDtypeStruct((bsz, n_out, d), F32),
        compiler_params=_cparams(2),
        name="expert_combine",
    )(h, ys, slot, gate)


def _even_weights(w_in, conv_w, gate_b, decay_logit):
    d = w_in.shape[0]
    mw = N_HEADS * HEAD_DIM
    mq, mk, mv, mo = (w_in[:, i * mw:(i + 1) * mw] for i in range(4))
    gates = w_in[:, 4 * mw:4 * mw + 4 * N_HEADS]
    r0 = 4 * mw + 4 * N_HEADS
    rq, rk, rv, rg = (w_in[:, r0 + i * mw:r0 + (i + 1) * mw] for i in range(4))

    def per_head(t):
        return t.reshape(d, N_HEADS, HEAD_DIM).transpose(1, 0, 2)

    def gate_cols(fw_off, bw_off):
        cols = jnp.stack([gates[:, fw_off:fw_off + N_HEADS], gates[:, bw_off:bw_off + N_HEADS]], axis=-1)
        cols = cols.transpose(1, 0, 2)
        return jnp.pad(cols, ((0, 0), (0, 0), (0, LANES - 2)))

    def gate_bias(fw_off, bw_off):
        b = jnp.stack([gate_b[fw_off:fw_off + N_HEADS], gate_b[bw_off:bw_off + N_HEADS]], axis=-1)
        return jnp.pad(b, ((0, 0), (0, LANES - 2)))

    w_m = jnp.concatenate([per_head(mq), per_head(mk), per_head(mv), per_head(mo),
                           gate_cols(0, 2 * N_HEADS), gate_cols(N_HEADS, 3 * N_HEADS)], axis=-1).astype(BF16)
    gb = jnp.concatenate([gate_bias(0, 2 * N_HEADS), gate_bias(N_HEADS, 3 * N_HEADS)], axis=-1)[:, None, :]
    conv = jnp.concatenate([conv_w[:, :mw].reshape(CONV_K, N_HEADS, HEAD_DIM),
                            conv_w[:, mw:].reshape(CONV_K, N_HEADS, HEAD_DIM)], axis=-1).transpose(1, 0, 2)
    w_r = jnp.concatenate([per_head(rq), per_head(rk), per_head(rv), per_head(rg)], axis=-1).astype(BF16)
    dl = jnp.pad(decay_logit.T, ((0, 0), (0, LANES - 2)))[:, None, :]
    return w_m, gb.astype(F32), conv.astype(F32), w_r, dl.astype(F32)


def _rotary_tables(n_rows):
    half = HEAD_DIM // 2
    inv = ROPE_BASE ** (-jnp.arange(half, dtype=F32) / half)
    ang = jnp.arange(n_rows, dtype=F32)[:, None] * inv[None, :]
    cos, sin = jnp.cos(ang), jnp.sin(ang)
    return jnp.concatenate([cos, cos], axis=-1), jnp.concatenate([-sin, sin], axis=-1)


def _na_weights(w_in):
    d = w_in.shape[0]
    n_heads = w_in.shape[1] // (3 * NA_HEAD_DIM)
    z = w_in.reshape(d, 3, n_heads // 2, 2 * NA_HEAD_DIM)
    return z.transpose(2, 0, 1, 3).reshape(n_heads // 2, d, 6 * NA_HEAD_DIM).astype(BF16)


def _na_bias_table(rpb):
    col = jnp.arange(GRID_W)
    col_start = jnp.clip(col - NA_WIN_COLS // 2, 0, GRID_W - NA_WIN_COLS)
    col_in = (col[None, :] >= col_start[:, None]) & (col[None, :] < col_start[:, None] + NA_WIN_COLS)
    dc_idx = jnp.clip(col[None, :] - col[:, None], -(NA_WIN_COLS - 1), NA_WIN_COLS - 1) + NA_WIN_COLS - 1
    rpb_cols = rpb.astype(F32)[:, :, dc_idx]
    tbl = jnp.where(col_in[None, None], rpb_cols, NEG)
    n_pairs, n_dr = rpb.shape[0] // 2, rpb.shape[1]
    tbl = tbl.reshape(n_pairs, 2, n_dr, GRID_W, GRID_W).transpose(0, 2, 4, 1, 3)
    return tbl.reshape(n_pairs, n_dr, GRID_W, 2 * GRID_W)


def _ffn(h, mixes, wo, fg, wr, wg, wu, wd, layer, n_out):
    n_rows = h.shape[1]
    cap = CAP_FACTOR * n_rows // N_EXPERTS
    cap_pad = ((cap + BF16_ROWS - 1) // BF16_ROWS) * BF16_ROWS
    wr_hi = wr.astype(BF16)
    wr_lo = (wr - wr_hi.astype(F32)).astype(BF16)
    pad = ((0, 0), (0, LANES - N_EXPERTS))
    wr_pieces = jnp.concatenate([jnp.pad(wr_hi, pad), jnp.pad(wr_lo, pad)], axis=1)
    hn, u2, slot, gate = _post_call(h, mixes, wo.astype(BF16), fg[None, :], wr_pieces, cap)
    xs = _gather_call(u2, slot, cap_pad)
    ys = _expert_call(xs, wg, wu, wd, layer)
    return _combine_call(hn, ys, slot, gate, n_out)


def kernel(x, meta_tokens, attn_norm_g, ffn_norm_g, even_w_in, even_conv_w, even_gate_b, even_m_norm_g, even_ret_decay_logit, even_r_norm_g, even_w_out, odd_w_in, odd_q_norm_g, odd_k_norm_g, odd_rpb, odd_w_out, router_w, expert_w_gate, expert_w_up, expert_w_down):
    bsz = x.shape[0]
    depth = attn_norm_g.shape[0]
    meta = jnp.broadcast_to(meta_tokens.astype(x.dtype)[None], (bsz,) + meta_tokens.shape)
    h = jnp.concatenate([meta, x], axis=1)
    n_rows = h.shape[1]
    cos2, sin2 = _rotary_tables(n_rows)
    for layer in range(depth):
        j = layer // 2
        gn = attn_norm_g[layer][None, :]
        if layer % 2 == 0:
            w_m, gb, conv, w_r, dl = _even_weights(even_w_in[j], even_conv_w[j], even_gate_b[j],
                                                   even_ret_decay_logit[j])
            m_out = _mlstm_call(h, gn, w_m, conv, gb, even_m_norm_g[j].reshape(N_HEADS, 1, HEAD_DIM))
            r_out = _ret_call(h, gn, w_r, cos2, sin2, dl, even_r_norm_g[j].reshape(N_HEADS, 1, HEAD_DIM))
            mixes, wo = [m_out, r_out], even_w_out[j]
        else:
            a_out = _na_call(h, gn, _na_weights(odd_w_in[j]), odd_q_norm_g[j], odd_k_norm_g[j],
                             _na_bias_table(odd_rpb[j]))
            mixes, wo = [a_out], odd_w_out[j]
        n_out = n_rows - N_META if layer == depth - 1 else n_rows
        h = _ffn(h, mixes, wo, ffn_norm_g[layer], router_w[layer],
                 expert_w_gate, expert_w_up, expert_w_down, layer, n_out)
    return h
```

```python
import functools

import jax
import jax.numpy as jnp
import numpy as np
from jax import lax
from jax.experimental import pallas as pl
from jax.experimental.pallas import tpu as pltpu

F32 = jnp.float32
BF16 = jnp.bfloat16

LANES = 128
BF16_ROWS = 16
N_META = 16
GRID_W = 64
EPS = 1e-6
HEAD_DIM = 128
N_HEADS = 4
CONV_K = 5
CONV_PAD = 8
ROPE_BASE = 10000.0
NA_HEAD_DIM = 64
NA_WIN_ROWS = 8
NA_WIN_COLS = 16
N_EXPERTS = 16
CAP_FACTOR = 2
NEG = -1e30
SCAN_T = 256
VMEM_LIMIT = 56 * 1024 * 1024


def _cparams(n_axes):
    return pltpu.CompilerParams(
        dimension_semantics=("arbitrary",) * n_axes, vmem_limit_bytes=VMEM_LIMIT)


def _row_blocks(n_rows):
    for nb in (3, 2, 4, 6, 1):
        if n_rows % (8 * nb) == 0:
            step = n_rows // nb
            return [(i * step, step) for i in range(nb)]
    return [(0, n_rows)]


def _rms(x, g):
    return x * lax.rsqrt(jnp.mean(x * x, axis=-1, keepdims=True) + EPS) * g


def _sigmoid(x):
    return 1.0 / (1.0 + jnp.exp(-x))


def _log_sigmoid(x):
    return jnp.minimum(x, 0.0) - jnp.log(1.0 + jnp.exp(-jnp.abs(x)))


def _dot(a, b):
    return jnp.dot(a, b, preferred_element_type=F32)


def _dot_nt(a, b):
    return lax.dot_general(a, b, (((1,), (1,)), ((), ())), preferred_element_type=F32)


def _dot_tn(a, b):
    return lax.dot_general(a, b, (((0,), (0,)), ((), ())), preferred_element_type=F32)


def _split3(x):
    hi = x.astype(BF16)
    r1 = x - hi.astype(F32)
    mid = r1.astype(BF16)
    lo = (r1 - mid.astype(F32)).astype(BF16)
    return hi, mid, lo


def _tri_prefix(tri_bf16, x):
    hi, mid, lo = _split3(x)
    return _dot(tri_bf16, hi) + _dot(tri_bf16, mid) + _dot(tri_bf16, lo)


def _norm_to_scratch(h_ref, gn_ref, un_ref):
    n_rows = h_ref.shape[1]
    for r0, nr in _row_blocks(n_rows):
        x = h_ref[0, pl.ds(r0, nr), :]
        un_ref[pl.ds(r0, nr), :] = _rms(x, gn_ref[...]).astype(BF16)


def _project(un_ref, w_ref, z_ref):
    n_rows = un_ref.shape[0]
    for r0, nr in _row_blocks(n_rows):
        z_ref[pl.ds(r0, nr), :] = _dot(un_ref[pl.ds(r0, nr), :], w_ref[0])


def _tri_mask(n, lower):
    r = lax.broadcasted_iota(jnp.int32, (n, n), 0)
    c = lax.broadcasted_iota(jnp.int32, (n, n), 1)
    return (c <= r) if lower else (c >= r)


def _mlstm_chunk(qc, kt, v_aug, a_row, f_col, mask, state):
    c_aug, g_prev = state
    qb = qc.astype(BF16)
    cm = jnp.max(jnp.where(mask, a_row, NEG), axis=-1, keepdims=True)
    g_col = jnp.maximum(g_prev, cm)
    dm = jnp.exp(jnp.where(mask, a_row - g_col, NEG))
    s = _dot(qb, kt.astype(BF16)) * dm
    w_inter = jnp.exp(g_prev - g_col)
    tot = w_inter * _dot(qb, c_aug.astype(BF16)) + _dot(s.astype(BF16), v_aug)
    den = tot[:, HEAD_DIM:2 * HEAD_DIM][:, 0:1]
    out = tot[:, 0:HEAD_DIM] / jnp.maximum(jnp.abs(den), jnp.exp(-(f_col + g_col)))
    g_end = jnp.maximum(g_prev, jnp.max(a_row, axis=-1, keepdims=True))
    ktw = (kt * jnp.exp(a_row - g_end)).astype(BF16)
    c_new = jnp.exp(g_prev - g_end) * c_aug + _dot(ktw, v_aug)
    return out, (c_new, g_end)


def _mlstm_kernel(h_ref, gn_ref, w_ref, conv_ref, gb_ref, ng_ref, out_ref,
                  un_ref, z_ref, zc_ref, acc_ref, acol_ref, fcol_ref, atm_ref, atr_ref, kt_ref, vaug_ref):
    n_rows = h_ref.shape[1]
    n_real = n_rows - N_META
    t = SCAN_T
    n_chunks = n_real // t

    @pl.when(pl.program_id(1) == 0)
    def _():
        _norm_to_scratch(h_ref, gn_ref, un_ref)

    _project(un_ref, w_ref, z_ref)

    zc_ref[pl.ds(0, CONV_PAD), :] = jnp.zeros((CONV_PAD, 2 * HEAD_DIM), F32)
    zc_ref[pl.ds(CONV_PAD + n_rows, CONV_PAD), :] = jnp.zeros((CONV_PAD, 2 * HEAD_DIM), F32)
    for r0, nr in _row_blocks(n_rows):
        zc_ref[pl.ds(CONV_PAD + r0, nr), :] = z_ref[pl.ds(r0, nr), 0:2 * HEAD_DIM]
    lane2 = lax.broadcasted_iota(jnp.int32, (1, 2 * HEAD_DIM), 1)
    qk_scale = jnp.where(lane2 >= HEAD_DIM, HEAD_DIM ** -0.5, 1.0).astype(F32)
    for r0, nr in _row_blocks(n_rows):
        acc = jnp.zeros((nr, 2 * HEAD_DIM), F32)
        for j in range(CONV_K):
            off = CONV_PAD - (CONV_K - 1) // 2 + j + r0
            acc = acc + zc_ref[pl.ds(off, nr), :] * conv_ref[0, pl.ds(j, 1), :]
        z_ref[pl.ds(r0, nr), 0:2 * HEAD_DIM] = acc * _sigmoid(acc) * qk_scale

    gi_off, gf_off = 4 * HEAD_DIM, 5 * HEAD_DIM
    bias_i = gb_ref[0, :, 0:LANES]
    bias_f = gb_ref[0, :, LANES:2 * LANES]
    lane = lax.broadcasted_iota(jnp.int32, (1, LANES), 1)
    tri_m = jnp.where(_tri_mask(N_META, True), 1.0, 0.0).astype(BF16)
    lf_m = _log_sigmoid(z_ref[pl.ds(0, N_META), gf_off:gf_off + LANES] + bias_f)
    f_meta = _tri_prefix(tri_m, lf_m)
    f_meta_end = f_meta[N_META - 1:N_META, :]
    tri_b = jnp.where(_tri_mask(LANES, True), 1.0, 0.0).astype(BF16)

    def prefix_body(c, carry):
        r0 = pl.multiple_of(N_META + c * LANES, 8)
        lf = _log_sigmoid(z_ref[pl.ds(r0, LANES), gf_off:gf_off + LANES] + bias_f)
        p = _tri_prefix(tri_b, lf) + carry
        fcol_ref[pl.ds(r0, LANES), :] = p
        return p[LANES - 1:LANES, :]

    total = lax.fori_loop(0, n_real // LANES, prefix_body, jnp.zeros((1, LANES), F32), unroll=4)

    fcol_ref[pl.ds(0, N_META), :] = f_meta
    acol_ref[pl.ds(0, N_META), :] = z_ref[pl.ds(0, N_META), gi_off:gi_off + LANES] + bias_i - f_meta

    def finish_body(c, carry):
        r0 = pl.multiple_of(N_META + c * LANES, 8)
        lf = _log_sigmoid(z_ref[pl.ds(r0, LANES), gf_off:gf_off + LANES] + bias_f)
        p = fcol_ref[pl.ds(r0, LANES), :]
        f = f_meta_end + jnp.where(lane == 0, p, total - p + lf)
        fcol_ref[pl.ds(r0, LANES), :] = f
        a = z_ref[pl.ds(r0, LANES), gi_off:gi_off + LANES] + bias_i - f
        acol_ref[pl.ds(r0, LANES), :] = a
        l0 = pl.multiple_of(c * LANES, LANES)
        atr_ref[:, pl.ds(l0, LANES)] = a.T[0:8, :]
        kt_ref[:, pl.ds(l0, LANES)] = z_ref[pl.ds(r0, LANES), HEAD_DIM:2 * HEAD_DIM].T
        vaug_ref[pl.ds(r0, LANES), :] = jnp.concatenate(
            [z_ref[pl.ds(r0, LANES), 2 * HEAD_DIM:3 * HEAD_DIM], ones_col], axis=1).astype(BF16)
        return carry

    ones_col = jnp.where(lax.broadcasted_iota(jnp.int32, (LANES, LANES), 1) == 0, 1.0, 0.0)
    lax.fori_loop(0, n_real // LANES, finish_body, 0, unroll=4)
    atm_ref[...] = acol_ref[pl.ds(0, LANES), :].T[0:8, :]
    kt_meta = z_ref[pl.ds(0, LANES), HEAD_DIM:2 * HEAD_DIM].T[:, 0:N_META]
    vaug_meta = jnp.concatenate([z_ref[pl.ds(0, N_META), 2 * HEAD_DIM:3 * HEAD_DIM],
                                 jnp.where(lax.broadcasted_iota(jnp.int32, (N_META, LANES), 1) == 0, 1.0, 0.0)],
                                axis=1).astype(BF16)

    mask_m = _tri_mask(N_META, True)
    states = []
    for d in (0, 1):
        state = (jnp.zeros((HEAD_DIM, 2 * HEAD_DIM), F32), jnp.zeros((1, 1), F32))
        out_m, state = _mlstm_chunk(
            z_ref[pl.ds(0, N_META), 0:HEAD_DIM], kt_meta, vaug_meta,
            atm_ref[pl.ds(d, 1), 0:N_META], fcol_ref[pl.ds(0, N_META), d:d + 1], mask_m, state)
        acc_ref[d, pl.ds(0, N_META), :] = out_m
        states.append(state)

    def body(i, carry):
        new = []
        for d in (0, 1):
            c = i if d == 0 else n_chunks - 1 - i
            r0 = pl.multiple_of(N_META + c * t, 8)
            l0 = pl.multiple_of(c * t, LANES)
            out, st = _mlstm_chunk(
                z_ref[pl.ds(r0, t), 0:HEAD_DIM], kt_ref[:, pl.ds(l0, t)], vaug_ref[pl.ds(r0, t), :],
                atr_ref[pl.ds(d, 1), pl.ds(l0, t)], fcol_ref[pl.ds(r0, t), d:d + 1],
                _tri_mask(t, d == 0), carry[d])
            acc_ref[d, pl.ds(r0, t), :] = out
            new.append(st)
        return tuple(new)

    lax.fori_loop(0, n_chunks, body, tuple(states))

    for r0, nr in _row_blocks(n_rows):
        hs = acc_ref[0, pl.ds(r0, nr), :] + acc_ref[1, pl.ds(r0, nr), :]
        o = z_ref[pl.ds(r0, nr), 3 * HEAD_DIM:4 * HEAD_DIM]
        out_ref[0, pl.ds(r0, nr), :] = (_sigmoid(o) * _rms(hs, ng_ref[0])).astype(BF16)


def _mlstm_call(h, gn, w, conv, gb, ng):
    bsz, n_rows, d = h.shape
    nw = w.shape[-1]
    n_real = n_rows - N_META
    return pl.pallas_call(
        _mlstm_kernel,
        grid=(bsz, N_HEADS),
        in_specs=[
            pl.BlockSpec((1, n_rows, d), lambda b, hd: (b, 0, 0)),
            pl.BlockSpec((1, d), lambda b, hd: (0, 0)),
            pl.BlockSpec((1, d, nw), lambda b, hd: (hd, 0, 0)),
            pl.BlockSpec((1, CONV_K, 2 * HEAD_DIM), lambda b, hd: (hd, 0, 0)),
            pl.BlockSpec((1, 1, 2 * LANES), lambda b, hd: (hd, 0, 0)),
            pl.BlockSpec((1, 1, HEAD_DIM), lambda b, hd: (hd, 0, 0)),
        ],
        out_specs=pl.BlockSpec((1, n_rows, HEAD_DIM), lambda b, hd: (b, 0, hd)),
        out_shape=jax.ShapeDtypeStruct((bsz, n_rows, N_HEADS * HEAD_DIM), BF16),
        scratch_shapes=[
            pltpu.VMEM((n_rows, d), BF16),
            pltpu.VMEM((n_rows, nw), F32),
            pltpu.VMEM((n_rows + 2 * CONV_PAD, 2 * HEAD_DIM), F32),
            pltpu.VMEM((2, n_rows, HEAD_DIM), F32),
            pltpu.VMEM((n_rows, LANES), F32),
            pltpu.VMEM((n_rows, LANES), F32),
            pltpu.VMEM((8, LANES), F32),
            pltpu.VMEM((8, n_real), F32),
            pltpu.VMEM((HEAD_DIM, n_real), F32),
            pltpu.VMEM((n_rows, 2 * HEAD_DIM), BF16),
        ],
        compiler_params=_cparams(2),
        name="mlstm_heads",
    )(h, gn, w, conv, gb, ng)


def _ret_chunk(qc, kt, vc, dmat, dq, dk_row, dchunk, r_st):
    qb, vb = qc.astype(BF16), vc.astype(BF16)
    s = _dot(qb, kt.astype(BF16)) * dmat
    out = _dot(s.astype(BF16), vb) + dq * _dot(qb, r_st.astype(BF16))
    r_new = dchunk * r_st + _dot((kt * dk_row).astype(BF16), vb)
    return out, r_new


def _decay_tables(n, lg, forward):
    r = lax.broadcasted_iota(jnp.int32, (n, n), 0)
    c = lax.broadcasted_iota(jnp.int32, (n, n), 1)
    pos_q = lax.broadcasted_iota(jnp.int32, (n, 1), 0).astype(F32)
    pos_k = lax.broadcasted_iota(jnp.int32, (1, n), 1).astype(F32)
    if forward:
        dist, mask = (r - c).astype(F32), c <= r
        dq, dk = jnp.exp((pos_q + 1.0) * lg), jnp.exp((n - 1.0 - pos_k) * lg)
    else:
        dist, mask = (c - r).astype(F32), c >= r
        dq, dk = jnp.exp((n - pos_q) * lg), jnp.exp(pos_k * lg)
    dmat = jnp.exp(jnp.where(mask, dist * lg, NEG))
    return dmat, dq, dk, jnp.exp(n * lg)


def _ret_kernel(h_ref, gn_ref, w_ref, cos_ref, sin_ref, dl_ref, ng_ref, out_ref,
                un_ref, z_ref, acc_ref, dmat_ref, dq_ref, kt_ref):
    n_rows = h_ref.shape[1]
    n_real = n_rows - N_META
    t = SCAN_T
    n_chunks = n_real // t

    @pl.when(pl.program_id(1) == 0)
    def _():
        _norm_to_scratch(h_ref, gn_ref, un_ref)

    _project(un_ref, w_ref, z_ref)

    for r0, nr in _row_blocks(n_rows):
        cs, sn = cos_ref[pl.ds(r0, nr), :], sin_ref[pl.ds(r0, nr), :]
        q = z_ref[pl.ds(r0, nr), 0:HEAD_DIM]
        z_ref[pl.ds(r0, nr), 0:HEAD_DIM] = q * cs + pltpu.roll(q, HEAD_DIM // 2, 1) * sn
        k = z_ref[pl.ds(r0, nr), HEAD_DIM:2 * HEAD_DIM]
        z_ref[pl.ds(r0, nr), HEAD_DIM:2 * HEAD_DIM] = (
            (k * cs + pltpu.roll(k, HEAD_DIM // 2, 1) * sn) * HEAD_DIM ** -0.5)

    for c in range(n_real // LANES):
        kt_ref[:, c * LANES:(c + 1) * LANES] = z_ref[pl.ds(N_META + c * LANES, LANES), HEAD_DIM:2 * HEAD_DIM].T
    kt_meta = z_ref[pl.ds(0, LANES), HEAD_DIM:2 * HEAD_DIM].T[:, 0:N_META]

    lg_all = _log_sigmoid(dl_ref[0])
    states, dk_rows, dchunks = [], [], []
    for d in (0, 1):
        lg = lg_all[:, d:d + 1]
        dmat_m, _, dk_m, dch_m = _decay_tables(N_META, lg, True)
        dmat, dq, dk, dch = _decay_tables(t, lg, d == 0)
        dmat_ref[d] = dmat
        dq_ref[d] = jnp.broadcast_to(dq, (t, LANES))
        dk_rows.append(dk)
        dchunks.append(dch)
        out_m, r_st = _ret_chunk(
            z_ref[pl.ds(0, N_META), 0:HEAD_DIM], kt_meta, z_ref[pl.ds(0, N_META), 2 * HEAD_DIM:3 * HEAD_DIM],
            dmat_m, jnp.zeros((N_META, 1), F32), dk_m, dch_m, jnp.zeros((HEAD_DIM, HEAD_DIM), F32))
        acc_ref[d, pl.ds(0, N_META), :] = out_m
        states.append(r_st)

    def body(i, carry):
        new = []
        for d in (0, 1):
            c = i if d == 0 else n_chunks - 1 - i
            r0 = pl.multiple_of(N_META + c * t, 8)
            l0 = pl.multiple_of(c * t, LANES)
            out, st = _ret_chunk(
                z_ref[pl.ds(r0, t), 0:HEAD_DIM], kt_ref[:, pl.ds(l0, t)],
                z_ref[pl.ds(r0, t), 2 * HEAD_DIM:3 * HEAD_DIM],
                dmat_ref[d], dq_ref[d], dk_rows[d], dchunks[d], carry[d])
            acc_ref[d, pl.ds(r0, t), :] = out
            new.append(st)
        return tuple(new)

    lax.fori_loop(0, n_chunks, body, tuple(states))

    for r0, nr in _row_blocks(n_rows):
        hs = acc_ref[0, pl.ds(r0, nr), :] + acc_ref[1, pl.ds(r0, nr), :]
        g = z_ref[pl.ds(r0, nr), 3 * HEAD_DIM:4 * HEAD_DIM]
        out_ref[0, pl.ds(r0, nr), :] = (g * _sigmoid(g) * _rms(hs, ng_ref[0])).astype(BF16)


def _ret_call(h, gn, w, cos2, sin2, dl, ng):
    bsz, n_rows, d = h.shape
    nw = w.shape[-1]
    return pl.pallas_call(
        _ret_kernel,
        grid=(bsz, N_HEADS),
        in_specs=[
            pl.BlockSpec((1, n_rows, d), lambda b, hd: (b, 0, 0)),
            pl.BlockSpec((1, d), lambda b, hd: (0, 0)),
            pl.BlockSpec((1, d, nw), lambda b, hd: (hd, 0, 0)),
            pl.BlockSpec((n_rows, HEAD_DIM), lambda b, hd: (0, 0)),
            pl.BlockSpec((n_rows, HEAD_DIM), lambda b, hd: (0, 0)),
            pl.BlockSpec((1, 1, LANES), lambda b, hd: (hd, 0, 0)),
            pl.BlockSpec((1, 1, HEAD_DIM), lambda b, hd: (hd, 0, 0)),
        ],
        out_specs=pl.BlockSpec((1, n_rows, HEAD_DIM), lambda b, hd: (b, 0, hd)),
        out_shape=jax.ShapeDtypeStruct((bsz, n_rows, N_HEADS * HEAD_DIM), BF16),
        scratch_shapes=[
            pltpu.VMEM((n_rows, d), BF16),
            pltpu.VMEM((n_rows, nw), F32),
            pltpu.VMEM((2, n_rows, HEAD_DIM), F32),
            pltpu.VMEM((2, SCAN_T, SCAN_T), F32),
            pltpu.VMEM((2, SCAN_T, LANES), F32),
            pltpu.VMEM((HEAD_DIM, n_rows - N_META), F32),
        ],
        compiler_params=_cparams(2),
        name="retention_heads",
    )(h, gn, w, cos2, sin2, dl, ng)


def _head_pair_block_diag(qp):
    lane = lax.broadcasted_iota(jnp.int32, qp.shape, 1)
    zero = jnp.zeros_like(qp)
    return jnp.concatenate([jnp.where(lane < NA_HEAD_DIM, qp, zero),
                            jnp.where(lane >= NA_HEAD_DIM, qp, zero)], axis=0)


def _na_attend(qp, keys, values, bias_t):
    n = qp.shape[0]
    bd = _head_pair_block_diag(qp)
    scores = []
    for kb, bt in zip(keys, bias_t):
        s = _dot_nt(kb, bd)
        scores.append(s if bt is None else s + bt)
    m = scores[0].max(axis=0, keepdims=True)
    for s in scores[1:]:
        m = jnp.maximum(m, s.max(axis=0, keepdims=True))
    probs = [jnp.exp(s - m) for s in scores]
    den = probs[0].sum(axis=0, keepdims=True)
    for p in probs[1:]:
        den = den + p.sum(axis=0, keepdims=True)
    inv = 1.0 / den
    o2 = None
    for p, vb in zip(probs, values):
        part = _dot_tn((p * inv).astype(BF16), vb)
        o2 = part if o2 is None else o2 + part
    lane = lax.broadcasted_iota(jnp.int32, (n, 2 * NA_HEAD_DIM), 1)
    return jnp.where(lane < NA_HEAD_DIM, o2[0:n, :], o2[n:2 * n, :])


def _na_kernel(h_ref, gn_ref, w_ref, qkg_ref, bias_ref, out_ref,
               un_ref, z_ref, q_ref, k_ref, vt_ref, ot_ref, s_ref, p_ref, den_ref):
    n_rows = h_ref.shape[1]
    n_real = n_rows - N_META
    n_grid_rows = n_real // GRID_W
    dh = NA_HEAD_DIM
    pw = 2 * dh
    band = NA_WIN_ROWS * GRID_W

    @pl.when(pl.program_id(1) == 0)
    def _():
        _norm_to_scratch(h_ref, gn_ref, un_ref)

    _project(un_ref, w_ref, z_ref)

    r = lax.broadcasted_iota(jnp.int32, (2 * pw, 2 * pw), 0)
    c = lax.broadcasted_iota(jnp.int32, (2 * pw, 2 * pw), 1)
    head_ones = jnp.where(r // dh == c // dh, 1.0, 0.0).astype(BF16)
    for r0, nr in _row_blocks(n_rows):
        x = z_ref[pl.ds(r0, nr), 0:2 * pw]
        sq = x * x
        hi = sq.astype(BF16)
        lo = (sq - hi.astype(F32)).astype(BF16)
        ssq = _dot(hi, head_ones) + _dot(lo, head_ones)
        y = (x * lax.rsqrt(ssq * (1.0 / dh) + EPS) * qkg_ref[...]).astype(BF16)
        q_ref[pl.ds(r0, nr), :] = y[:, 0:pw]
        k_ref[pl.ds(r0, nr), :] = y[:, pw:2 * pw]

    n_tblocks = n_real // LANES
    for cpy in (0, 1):
        for c in range(n_tblocks - cpy):
            vb = z_ref[pl.ds(N_META + cpy * GRID_W + c * LANES, LANES), 2 * pw:3 * pw]
            vt_ref[cpy, :, c * LANES:(c + 1) * LANES] = vb.T.astype(BF16)
    vt_ref[1, :, (n_tblocks - 1) * LANES:n_tblocks * LANES] = jnp.zeros((pw, LANES), BF16)
    v_meta = z_ref[pl.ds(0, N_META), 2 * pw:3 * pw].astype(BF16)
    vt_meta = z_ref[pl.ds(0, LANES), 2 * pw:3 * pw].T[:, 0:N_META].astype(BF16)

    k_meta = k_ref[pl.ds(0, N_META), :]
    out_ref[0, pl.ds(0, N_META), :] = _na_attend(
        q_ref[pl.ds(0, N_META), :], [k_meta], [v_meta], [None]).astype(BF16)

    sub = lax.broadcasted_iota(jnp.int32, (pw, pw), 0)
    lane = lax.broadcasted_iota(jnp.int32, (pw, pw), 1)

    def row_start(r):
        return jnp.clip(r - NA_WIN_ROWS // 2, 0, n_grid_rows - NA_WIN_ROWS)

    def stage_scores(r, slot):
        rs = row_start(r)
        q0 = pl.multiple_of(N_META + r * GRID_W, 8)
        k0 = pl.multiple_of(N_META + rs * GRID_W, 8)
        bd = _head_pair_block_diag(q_ref[pl.ds(q0, GRID_W), :])
        bias = bias_ref[0, pl.ds(NA_WIN_ROWS - 1 - (r - rs), NA_WIN_ROWS)].reshape(band, pw)
        s_ref[slot, pl.ds(0, band), :] = _dot_nt(k_ref[pl.ds(k0, band), :], bd) + bias
        s_ref[slot, pl.ds(band, N_META), :] = _dot_nt(k_meta, bd)

    def stage_softmax(slot):
        s = s_ref[slot]
        p = jnp.exp(s - s.max(axis=0, keepdims=True))
        den_ref[slot] = p.sum(axis=0, keepdims=True)
        p_ref[slot] = p.astype(BF16)

    def stage_values(r, slot):
        rs = row_start(r)
        par = rs % 2
        l0 = pl.multiple_of((rs - par) * GRID_W, LANES)
        o_t = (_dot(vt_ref[par, :, pl.ds(l0, band)], p_ref[slot, pl.ds(0, band), :])
               + _dot(vt_meta, p_ref[slot, pl.ds(band, N_META), :]))
        ot_ref[r] = o_t / den_ref[slot]

    stage_scores(0, 0)
    stage_softmax(0)
    stage_scores(1, 1)

    def pipe_body(j, carry):
        i0 = 2 + 2 * j
        stage_values(i0 - 2, 0)
        stage_softmax(1)
        stage_scores(i0, 0)
        stage_values(i0 - 1, 1)
        stage_softmax(0)
        stage_scores(i0 + 1, 1)
        return carry

    lax.fori_loop(0, (n_grid_rows - 2) // 2, pipe_body, 0)
    stage_values(n_grid_rows - 2, 0)
    stage_softmax(1)
    stage_values(n_grid_rows - 1, 1)

    for c in range(n_tblocks):
        oa, ob = ot_ref[2 * c], ot_ref[2 * c + 1]
        sel_a = jnp.where(sub < dh, oa, pltpu.roll(oa, dh, 1))
        sel_b = jnp.where(sub < dh, pltpu.roll(ob, dh, 1), ob)
        out_ref[0, pl.ds(N_META + c * LANES, LANES), :] = jnp.where(lane < dh, sel_a, sel_b).T.astype(BF16)


def _na_call(h, gn, w, qg, kg, bias):
    bsz, n_rows, d = h.shape
    n_pairs, _, nw = w.shape
    pw = 2 * NA_HEAD_DIM
    band = NA_WIN_ROWS * GRID_W
    qkg = jnp.concatenate([jnp.tile(qg, 2) * NA_HEAD_DIM ** -0.5, jnp.tile(kg, 2)])[None, :].astype(F32)
    return pl.pallas_call(
        _na_kernel,
        grid=(bsz, n_pairs),
        in_specs=[
            pl.BlockSpec((1, n_rows, d), lambda b, hp: (b, 0, 0)),
            pl.BlockSpec((1, d), lambda b, hp: (0, 0)),
            pl.BlockSpec((1, d, nw), lambda b, hp: (hp, 0, 0)),
            pl.BlockSpec((1, 2 * pw), lambda b, hp: (0, 0)),
            pl.BlockSpec((1,) + bias.shape[1:], lambda b, hp: (hp, 0, 0, 0)),
        ],
        out_specs=pl.BlockSpec((1, n_rows, pw), lambda b, hp: (b, 0, hp)),
        out_shape=jax.ShapeDtypeStruct((bsz, n_rows, n_pairs * pw), BF16),
        scratch_shapes=[
            pltpu.VMEM((n_rows, d), BF16),
            pltpu.VMEM((n_rows, nw), F32),
            pltpu.VMEM((n_rows, pw), BF16),
            pltpu.VMEM((n_rows, pw), BF16),
            pltpu.VMEM((2, pw, n_rows - N_META), BF16),
            pltpu.VMEM(((n_rows - N_META) // GRID_W, pw, pw), F32),
            pltpu.VMEM((2, band + N_META, pw), F32),
            pltpu.VMEM((2, band + N_META, pw), BF16),
            pltpu.VMEM((2, 1, pw), F32),
        ],
        compiler_params=_cparams(2),
        name="neighbourhood_attention",
    )(h, gn, w, qkg, bias)


def _lane_prefix_exclusive(x, tri_strict):
    n_blocks = x.shape[1] // LANES
    carry = jnp.zeros((x.shape[0], 1), F32)
    pieces = []
    for j in range(n_blocks):
        blk = x[:, j * LANES:(j + 1) * LANES]
        pieces.append(_dot(blk.astype(BF16), tri_strict) + carry)
        carry = carry + jnp.sum(blk, axis=-1, keepdims=True)
    return jnp.concatenate(pieces, axis=1)


def _aligned_row_blocks(n_rows, step=512):
    blocks, r0 = [], 0
    while r0 < n_rows:
        nr = min(step, n_rows - r0)
        blocks.append((r0, nr))
        r0 += nr
    return blocks


def _post_kernel(n_mix, cap, *refs):
    h_ref = refs[0]
    mix_refs = refs[1:1 + n_mix]
    wo_ref, fg_ref, wr_ref = refs[1 + n_mix:4 + n_mix]
    hn_ref, u2_ref, slot_ref, gate_ref = refs[4 + n_mix:8 + n_mix]
    lgc_ref = refs[8 + n_mix]
    n_rows = h_ref.shape[1]
    n_pad = u2_ref.shape[1]

    for r0, nr in _row_blocks(n_rows):
        acc = h_ref[0, pl.ds(r0, nr), :]
        k0 = 0
        for m_ref in mix_refs:
            kw = m_ref.shape[2]
            acc = acc + _dot(m_ref[0, pl.ds(r0, nr), :], wo_ref[pl.ds(k0, kw), :])
            k0 += kw
        hn_ref[0, pl.ds(r0, nr), :] = acc
        u = _rms(acc, fg_ref[...])
        u_hi = u.astype(BF16)
        u2_ref[0, pl.ds(r0, nr), :] = u_hi
        u_lo = (u - u_hi.astype(F32)).astype(BF16)
        hh_hl = _dot(u_hi, wr_ref[...])
        lgc_ref[pl.ds(r0, nr), :] = (hh_hl[:, 0:LANES] + hh_hl[:, LANES:2 * LANES]
                                     + _dot(u_lo, wr_ref[:, 0:LANES]))
    u2_ref[0, pl.ds(n_rows, n_pad - n_rows), :] = jnp.zeros((n_pad - n_rows, u2_ref.shape[2]), BF16)
    lgc_ref[pl.ds(n_rows, n_pad - n_rows), :] = jnp.zeros((n_pad - n_rows, LANES), F32)

    logits = jnp.concatenate(
        [lgc_ref[pl.ds(c * LANES, LANES), :].T[0:N_EXPERTS, :] for c in range(n_pad // LANES)], axis=1)
    ex = jnp.exp(logits - jnp.max(logits, axis=0, keepdims=True))
    aff = ex / jnp.sum(ex, axis=0, keepdims=True)
    tok = lax.broadcasted_iota(jnp.int32, aff.shape, 1)
    aff = jnp.where(tok < n_rows, aff, -1.0)

    def count_ge(x):
        return jnp.sum(jnp.where(aff >= x, 1.0, 0.0), axis=-1, keepdims=True)

    capf = float(cap)
    tiny = jnp.full((aff.shape[0], 1), 2.0 ** -126, F32)
    ok0 = count_ge(tiny) >= capf
    p = tiny
    for j in (64, 32, 16, 8, 4, 2, 1):
        cand = p * (2.0 ** j)
        p = jnp.where(count_ge(cand) >= capf, cand, p)
    m = p
    for i in range(1, 24):
        cand = m + p * (2.0 ** -i)
        m = jnp.where(count_ge(cand) >= capf, cand, m)
    thr = jnp.where(ok0, m, 0.0)

    gt = aff > thr
    eq = aff == thr
    need = capf - jnp.sum(jnp.where(gt, 1.0, 0.0), axis=-1, keepdims=True)
    r = lax.broadcasted_iota(jnp.int32, (LANES, LANES), 0)
    c = lax.broadcasted_iota(jnp.int32, (LANES, LANES), 1)
    tri_strict = jnp.where(r < c, 1.0, 0.0).astype(BF16)
    eq_rank = _lane_prefix_exclusive(jnp.where(eq, 1.0, 0.0), tri_strict)
    sel = gt | (eq & (eq_rank < need))
    pos = _lane_prefix_exclusive(jnp.where(sel, 1.0, 0.0), tri_strict)
    slot_ref[0] = jnp.where(sel, pos, -1.0).astype(jnp.int32)
    gate_ref[0] = jnp.where(sel, aff, 0.0)


def _post_call(h, mixes, wo, fg, wr, cap):
    bsz, n_rows, d = h.shape
    n_pad = ((n_rows + LANES - 1) // LANES) * LANES
    n_mix = len(mixes)
    row_spec = lambda w: pl.BlockSpec((1, n_rows, w), lambda b: (b, 0, 0), pipeline_mode=pl.Buffered(1))
    return pl.pallas_call(
        functools.partial(_post_kernel, n_mix, cap),
        grid=(bsz,),
        in_specs=[row_spec(d)] + [row_spec(m.shape[2]) for m in mixes] + [
            pl.BlockSpec((d, d), lambda b: (0, 0)),
            pl.BlockSpec((1, d), lambda b: (0, 0)),
            pl.BlockSpec((d, 2 * LANES), lambda b: (0, 0)),
        ],
        out_specs=[
            row_spec(d),
            pl.BlockSpec((1, n_pad, d), lambda b: (b, 0, 0), pipeline_mode=pl.Buffered(1)),
            pl.BlockSpec((1, N_EXPERTS, n_pad), lambda b: (b, 0, 0)),
            pl.BlockSpec((1, N_EXPERTS, n_pad), lambda b: (b, 0, 0)),
        ],
        out_shape=[
            jax.ShapeDtypeStruct((bsz, n_rows, d), F32),
            jax.ShapeDtypeStruct((bsz, n_pad, d), BF16),
            jax.ShapeDtypeStruct((bsz, N_EXPERTS, n_pad), jnp.int32),
            jax.ShapeDtypeStruct((bsz, N_EXPERTS, n_pad), F32),
        ],
        scratch_shapes=[pltpu.VMEM((n_pad, LANES), F32)],
        compiler_params=_cparams(1),
        name="outproj_router",
    )(h, *mixes, wo, fg, wr)


def _gather_kernel(u2_ref, slot_ref, xs_ref):
    cap_pad = xs_ref.shape[2]
    n_pad = u2_ref.shape[1]
    srow_id = lax.broadcasted_iota(jnp.int32, (cap_pad, n_pad), 0)

    def body(e, carry):
        onehot = jnp.where(srow_id == slot_ref[0, pl.ds(e, 1), :], 1.0, 0.0).astype(BF16)
        xs_ref[0, e] = _dot(onehot, u2_ref[0]).astype(BF16)
        return carry

    lax.fori_loop(0, N_EXPERTS, body, 0)


def _gather_call(u2, slot, cap_pad):
    bsz, n_pad, d = u2.shape
    return pl.pallas_call(
        _gather_kernel,
        grid=(bsz,),
        in_specs=[
            pl.BlockSpec((1, n_pad, d), lambda b: (b, 0, 0)),
            pl.BlockSpec((1, N_EXPERTS, n_pad), lambda b: (b, 0, 0)),
        ],
        out_specs=pl.BlockSpec((1, N_EXPERTS, cap_pad, d), lambda b: (b, 0, 0, 0)),
        out_shape=jax.ShapeDtypeStruct((bsz, N_EXPERTS, cap_pad, d), BF16),
        compiler_params=_cparams(1),
        name="expert_gather",
    )(u2, slot)


def _expert_kernel(xs_ref, wg_ref, wu_ref, wd_ref, ys_ref, wgb_ref, wub_ref, wdb_ref):
    bb, _, cap_pad, d = xs_ref.shape

    @pl.when(pl.program_id(1) == 0)
    def _():
        wgb_ref[...] = wg_ref[0].astype(BF16)
        wub_ref[...] = wu_ref[0].astype(BF16)
        wdb_ref[...] = wd_ref[0].astype(BF16)

    x = xs_ref[...].reshape(bb * cap_pad, d)
    g = _dot(x, wgb_ref[...])
    u = _dot(x, wub_ref[...])
    hdn = (g * _sigmoid(g) * u).astype(BF16)
    ys_ref[...] = _dot(hdn, wdb_ref[...]).astype(BF16).reshape(ys_ref.shape)


def _expert_call(xs, wg, wu, wd, layer):
    bsz, n_e, cap_pad, d = xs.shape
    ff = wg.shape[-1]
    seq_blocks = 4 if bsz % 4 == 0 else 1
    bb = bsz // seq_blocks
    return pl.pallas_call(
        _expert_kernel,
        grid=(n_e, seq_blocks),
        in_specs=[
            pl.BlockSpec((bb, 1, cap_pad, d), lambda e, m: (m, e, 0, 0)),
            pl.BlockSpec((1, None, d, ff), lambda e, m: (layer, e, 0, 0)),
            pl.BlockSpec((1, None, d, ff), lambda e, m: (layer, e, 0, 0)),
            pl.BlockSpec((1, None, ff, d), lambda e, m: (layer, e, 0, 0)),
        ],
        out_specs=pl.BlockSpec((bb, 1, cap_pad, d), lambda e, m: (m, e, 0, 0)),
        out_shape=jax.ShapeDtypeStruct((bsz, n_e, cap_pad, d), BF16),
        scratch_shapes=[pltpu.VMEM((d, ff), BF16), pltpu.VMEM((d, ff), BF16), pltpu.VMEM((ff, d), BF16)],
        compiler_params=_cparams(2),
        name="expert_swiglu",
    )(xs, wg, wu, wd)


def _combine_kernel(h_ref, ys_ref, slot_ref, gate_ref, out_ref):
    n_rows = h_ref.shape[1]
    n_out = out_ref.shape[1]
    skip = n_rows - n_out
    _, e_grp, cap_pad, d = ys_ref.shape
    n_pad = slot_ref.shape[2]
    g = pl.program_id(1)

    @pl.when(g == 0)
    def _():
        out_ref[0] = h_ref[0, pl.ds(skip, n_out), :]

    srow_id = lax.broadcasted_iota(jnp.int32, (cap_pad, n_pad), 0)
    pieces = []
    for j in range(e_grp):
        e = g * e_grp + j
        hit = srow_id == slot_ref[0, pl.ds(e, 1), :]
        pieces.append(jnp.where(hit, gate_ref[0, pl.ds(e, 1), :], 0.0).astype(BF16))
    weights = jnp.concatenate(pieces, axis=0)
    ys = ys_ref[0].reshape(e_grp * cap_pad, d)
    for t0, tn in _aligned_row_blocks(n_pad, 1024):
        lo, hi = max(t0, skip), min(t0 + tn, n_rows)
        if hi <= lo:
            continue
        part = _dot_tn(weights[:, t0:t0 + tn], ys)
        out_ref[0, pl.ds(lo - skip, hi - lo), :] += part[lo - t0:hi - t0, :]


def _combine_call(h, ys, slot, gate, n_out, e_grp=4):
    bsz, n_rows, d = h.shape
    _, n_e, cap_pad, _ = ys.shape
    n_pad = slot.shape[2]
    return pl.pallas_call(
        _combine_kernel,
        grid=(bsz, n_e // e_grp),
        in_specs=[
            pl.BlockSpec((1, n_rows, d), lambda b, g: (b, 0, 0)),
            pl.BlockSpec((1, e_grp, cap_pad, d), lambda b, g: (b, g, 0, 0)),
            pl.BlockSpec((1, n_e, n_pad), lambda b, g: (b, 0, 0)),
            pl.BlockSpec((1, n_e, n_pad), lambda b, g: (b, 0, 0)),
        ],
        out_specs=pl.BlockSpec((1, n_out, d), lambda b, g: (b, 0, 0)),
        out_shape=jax.ShapeDtypeStruct((bsz, n_out, d), F32),
        compiler_params=_cparams(2),
        name="expert_combine",
    )(h, ys, slot, gate)


def _even_weights(w_in, conv_w, gate_b, decay_logit):
    d = w_in.shape[0]
    mw = N_HEADS * HEAD_DIM
    mq, mk, mv, mo = (w_in[:, i * mw:(i + 1) * mw] for i in range(4))
    gates = w_in[:, 4 * mw:4 * mw + 4 * N_HEADS]
    r0 = 4 * mw + 4 * N_HEADS
    rq, rk, rv, rg = (w_in[:, r0 + i * mw:r0 + (i + 1) * mw] for i in range(4))

    def per_head(t):
        return t.reshape(d, N_HEADS, HEAD_DIM).transpose(1, 0, 2)

    def gate_cols(fw_off, bw_off):
        cols = jnp.stack([gates[:, fw_off:fw_off + N_HEADS], gates[:, bw_off:bw_off + N_HEADS]], axis=-1)
        cols = cols.transpose(1, 0, 2)
        return jnp.pad(cols, ((0, 0), (0, 0), (0, LANES - 2)))

    def gate_bias(fw_off, bw_off):
        b = jnp.stack([gate_b[fw_off:fw_off + N_HEADS], gate_b[bw_off:bw_off + N_HEADS]], axis=-1)
        return jnp.pad(b, ((0, 0), (0, LANES - 2)))

    w_m = jnp.concatenate([per_head(mq), per_head(mk), per_head(mv), per_head(mo),
                           gate_cols(0, 2 * N_HEADS), gate_cols(N_HEADS, 3 * N_HEADS)], axis=-1).astype(BF16)
    gb = jnp.concatenate([gate_bias(0, 2 * N_HEADS), gate_bias(N_HEADS, 3 * N_HEADS)], axis=-1)[:, None, :]
    conv = jnp.concatenate([conv_w[:, :mw].reshape(CONV_K, N_HEADS, HEAD_DIM),
                            conv_w[:, mw:].reshape(CONV_K, N_HEADS, HEAD_DIM)], axis=-1).transpose(1, 0, 2)
    w_r = jnp.concatenate([per_head(rq), per_head(rk), per_head(rv), per_head(rg)], axis=-1).astype(BF16)
    dl = jnp.pad(decay_logit.T, ((0, 0), (0, LANES - 2)))[:, None, :]
    return w_m, gb.astype(F32), conv.astype(F32), w_r, dl.astype(F32)


def _rotary_tables(n_rows):
    half = HEAD_DIM // 2
    inv = ROPE_BASE ** (-jnp.arange(half, dtype=F32) / half)
    ang = jnp.arange(n_rows, dtype=F32)[:, None] * inv[None, :]
    cos, sin = jnp.cos(ang), jnp.sin(ang)
    return jnp.concatenate([cos, cos], axis=-1), jnp.concatenate([-sin, sin], axis=-1)


def _na_weights(w_in):
    d = w_in.shape[0]
    n_heads = w_in.shape[1] // (3 * NA_HEAD_DIM)
    z = w_in.reshape(d, 3, n_heads // 2, 2 * NA_HEAD_DIM)
    return z.transpose(2, 0, 1, 3).reshape(n_heads // 2, d, 6 * NA_HEAD_DIM).astype(BF16)


def _na_bias_table(rpb):
    col = np.arange(GRID_W)
    col_start = np.clip(col - NA_WIN_COLS // 2, 0, GRID_W - NA_WIN_COLS)
    col_in = (col[None, :] >= col_start[:, None]) & (col[None, :] < col_start[:, None] + NA_WIN_COLS)
    dc_idx = np.clip(col[None, :] - col[:, None], -(NA_WIN_COLS - 1), NA_WIN_COLS - 1) + NA_WIN_COLS - 1
    selector = (np.arange(rpb.shape[2])[:, None, None] == dc_idx[None]).astype(np.float32)
    rpb_cols = jnp.einsum('hrd,dqk->hrqk', rpb.astype(F32), selector,
                          precision=lax.Precision.HIGHEST)
    tbl = jnp.where(col_in[None, None], rpb_cols, NEG)
    n_pairs, n_dr = rpb.shape[0] // 2, rpb.shape[1]
    tbl = tbl.reshape(n_pairs, 2, n_dr, GRID_W, GRID_W).transpose(0, 2, 4, 1, 3)
    return tbl.reshape(n_pairs, n_dr, GRID_W, 2 * GRID_W)


def _ffn(h, mixes, wo, fg, wr, wg, wu, wd, layer, n_out):
    n_rows = h.shape[1]
    cap = CAP_FACTOR * n_rows // N_EXPERTS
    cap_pad = ((cap + BF16_ROWS - 1) // BF16_ROWS) * BF16_ROWS
    wr_hi = wr.astype(BF16)
    wr_lo = (wr - wr_hi.astype(F32)).astype(BF16)
    pad = ((0, 0), (0, LANES - N_EXPERTS))
    wr_pieces = jnp.concatenate([jnp.pad(wr_hi, pad), jnp.pad(wr_lo, pad)], axis=1)
    hn, u2, slot, gate = _post_call(h, mixes, wo.astype(BF16), fg[None, :], wr_pieces, cap)
    xs = _gather_call(u2, slot, cap_pad)
    ys = _expert_call(xs, wg, wu, wd, layer)
    return _combine_call(hn, ys, slot, gate, n_out)


def kernel(x, meta_tokens, attn_norm_g, ffn_norm_g, even_w_in, even_conv_w, even_gate_b, even_m_norm_g, even_ret_decay_logit, even_r_norm_g, even_w_out, odd_w_in, odd_q_norm_g, odd_k_norm_g, odd_rpb, odd_w_out, router_w, expert_w_gate, expert_w_up, expert_w_down):
    bsz = x.shape[0]
    depth = attn_norm_g.shape[0]
    meta = jnp.broadcast_to(meta_tokens.astype(x.dtype)[None], (bsz,) + meta_tokens.shape)
    h = jnp.concatenate([meta, x], axis=1)
    n_rows = h.shape[1]
    cos2, sin2 = _rotary_tables(n_rows)
    for layer in range(depth):
        j = layer // 2
        gn = attn_norm_g[layer][None, :]
        if layer % 2 == 0:
            w_m, gb, conv, w_r, dl = _even_weights(even_w_in[j], even_conv_w[j], even_gate_b[j],
                                                   even_ret_decay_logit[j])
            m_out = _mlstm_call(h, gn, w_m, conv, gb, even_m_norm_g[j].reshape(N_HEADS, 1, HEAD_DIM))
            r_out = _ret_call(h, gn, w_r, cos2, sin2, dl, even_r_norm_g[j].reshape(N_HEADS, 1, HEAD_DIM))
            mixes, wo = [m_out, r_out], even_w_out[j]
        else:
            a_out = _na_call(h, gn, _na_weights(odd_w_in[j]), odd_q_norm_g[j], odd_k_norm_g[j],
                             _na_bias_table(odd_rpb[j]))
            mixes, wo = [a_out], odd_w_out[j]
        n_out = n_rows - N_META if layer == depth - 1 else n_rows
        h = _ffn(h, mixes, wo, ffn_norm_g[layer], router_w[layer],
                 expert_w_gate, expert_w_up, expert_w_down, layer, n_out)
    return h
```

```python
import functools

import jax
import jax.numpy as jnp
import numpy as np
from jax import lax
from jax.experimental import pallas as pl
from jax.experimental.pallas import tpu as pltpu

F32 = jnp.float32
BF16 = jnp.bfloat16

LANES = 128
BF16_ROWS = 16
MXU_DEPTH = 256
N_META = 16
GRID_W = 64
EPS = 1e-6
HEAD_DIM = 128
N_HEADS = 4
CONV_K = 5
CONV_PAD = 8
ROPE_BASE = 10000.0
NA_HEAD_DIM = 64
NA_WIN_ROWS = 8
NA_WIN_COLS = 16
NA_PAIRS = 2
N_EXPERTS = 16
CAP_FACTOR = 2
GATHER_GROUP = 2
NEG = -1e30
SCAN_T = 256
VMEM_LIMIT = 56 * 1024 * 1024


def _cparams(n_axes):
    return pltpu.CompilerParams(
        dimension_semantics=("arbitrary",) * n_axes, vmem_limit_bytes=VMEM_LIMIT)


def _row_blocks(n_rows):
    for nb in (3, 2, 4, 6, 1):
        if n_rows % (8 * nb) == 0:
            step = n_rows // nb
            return [(i * step, step) for i in range(nb)]
    return [(0, n_rows)]


def _rms(x, g):
    return x * lax.rsqrt(jnp.mean(x * x, axis=-1, keepdims=True) + EPS) * g


def _sigmoid(x):
    return 1.0 / (1.0 + jnp.exp(-x))


def _log_sigmoid(x):
    return jnp.minimum(x, 0.0) - jnp.log(1.0 + jnp.exp(-jnp.abs(x)))


def _dot(a, b):
    return jnp.dot(a, b, preferred_element_type=F32)


def _dot_nt(a, b):
    return lax.dot_general(a, b, (((1,), (1,)), ((), ())), preferred_element_type=F32)


def _dot_tn(a, b):
    return lax.dot_general(a, b, (((0,), (0,)), ((), ())), preferred_element_type=F32)


def _split3(x):
    hi = x.astype(BF16)
    r1 = x - hi.astype(F32)
    mid = r1.astype(BF16)
    lo = (r1 - mid.astype(F32)).astype(BF16)
    return hi, mid, lo


def _tri_prefix(tri_bf16, x):
    hi, mid, lo = _split3(x)
    return _dot(tri_bf16, hi) + _dot(tri_bf16, mid) + _dot(tri_bf16, lo)


def _norm_to_scratch(h_ref, gn_ref, un_ref):
    n_rows = h_ref.shape[1]
    for r0, nr in _row_blocks(n_rows):
        x = h_ref[0, pl.ds(r0, nr), :]
        un_ref[pl.ds(r0, nr), :] = _rms(x, gn_ref[...]).astype(BF16)


def _project(un_ref, w_ref, z_ref):
    n_rows = un_ref.shape[0]
    for r0, nr in _row_blocks(n_rows):
        z_ref[pl.ds(r0, nr), :] = _dot(un_ref[pl.ds(r0, nr), :], w_ref[0])


def _tri_mask(n, lower):
    r = lax.broadcasted_iota(jnp.int32, (n, n), 0)
    c = lax.broadcasted_iota(jnp.int32, (n, n), 1)
    return (c <= r) if lower else (c >= r)


def _mlstm_chunk(qc, kt, v_aug, a_row, f_col, mask, state):
    c_aug, g_prev = state
    qb = qc.astype(BF16)
    cm = jnp.max(jnp.where(mask, a_row, NEG), axis=-1, keepdims=True)
    g_col = jnp.maximum(g_prev, cm)
    dm = jnp.exp(jnp.where(mask, a_row - g_col, NEG))
    s = _dot(qb, kt.astype(BF16)) * dm
    w_inter = jnp.exp(g_prev - g_col)
    tot = w_inter * _dot(qb, c_aug.astype(BF16)) + _dot(s.astype(BF16), v_aug)
    den = tot[:, HEAD_DIM:2 * HEAD_DIM][:, 0:1]
    out = tot[:, 0:HEAD_DIM] / jnp.maximum(jnp.abs(den), jnp.exp(-(f_col + g_col)))
    g_end = jnp.maximum(g_prev, jnp.max(a_row, axis=-1, keepdims=True))
    ktw = (kt * jnp.exp(a_row - g_end)).astype(BF16)
    c_new = jnp.exp(g_prev - g_end) * c_aug + _dot(ktw, v_aug)
    return out, (c_new, g_end)


def _mlstm_kernel(h_ref, gn_ref, w_ref, conv_ref, gb_ref, ng_ref, out_ref,
                  un_ref, z_ref, zc_ref, acc_ref, acol_ref, fcol_ref, atm_ref, atr_ref, kt_ref, vaug_ref):
    n_rows = h_ref.shape[1]
    n_real = n_rows - N_META
    t = SCAN_T
    n_chunks = n_real // t

    @pl.when(pl.program_id(1) == 0)
    def _():
        _norm_to_scratch(h_ref, gn_ref, un_ref)

    _project(un_ref, w_ref, z_ref)

    zc_ref[pl.ds(0, CONV_PAD), :] = jnp.zeros((CONV_PAD, 2 * HEAD_DIM), F32)
    zc_ref[pl.ds(CONV_PAD + n_rows, CONV_PAD), :] = jnp.zeros((CONV_PAD, 2 * HEAD_DIM), F32)
    for r0, nr in _row_blocks(n_rows):
        zc_ref[pl.ds(CONV_PAD + r0, nr), :] = z_ref[pl.ds(r0, nr), 0:2 * HEAD_DIM]
    lane2 = lax.broadcasted_iota(jnp.int32, (1, 2 * HEAD_DIM), 1)
    qk_scale = jnp.where(lane2 >= HEAD_DIM, HEAD_DIM ** -0.5, 1.0).astype(F32)
    for r0, nr in _row_blocks(n_rows):
        acc = jnp.zeros((nr, 2 * HEAD_DIM), F32)
        for j in range(CONV_K):
            off = CONV_PAD - (CONV_K - 1) // 2 + j + r0
            acc = acc + zc_ref[pl.ds(off, nr), :] * conv_ref[0, pl.ds(j, 1), :]
        z_ref[pl.ds(r0, nr), 0:2 * HEAD_DIM] = acc * _sigmoid(acc) * qk_scale

    gi_off, gf_off = 4 * HEAD_DIM, 5 * HEAD_DIM
    bias_i = gb_ref[0, :, 0:LANES]
    bias_f = gb_ref[0, :, LANES:2 * LANES]
    lane = lax.broadcasted_iota(jnp.int32, (1, LANES), 1)
    tri_m = jnp.where(_tri_mask(N_META, True), 1.0, 0.0).astype(BF16)
    lf_m = _log_sigmoid(z_ref[pl.ds(0, N_META), gf_off:gf_off + LANES] + bias_f)
    f_meta = _tri_prefix(tri_m, lf_m)
    f_meta_end = f_meta[N_META - 1:N_META, :]
    tri_b = jnp.where(_tri_mask(LANES, True), 1.0, 0.0).astype(BF16)

    def prefix_body(c, carry):
        r0 = pl.multiple_of(N_META + c * LANES, 8)
        lf = _log_sigmoid(z_ref[pl.ds(r0, LANES), gf_off:gf_off + LANES] + bias_f)
        p = _tri_prefix(tri_b, lf) + carry
        fcol_ref[pl.ds(r0, LANES), :] = p
        return p[LANES - 1:LANES, :]

    total = lax.fori_loop(0, n_real // LANES, prefix_body, jnp.zeros((1, LANES), F32), unroll=4)

    fcol_ref[pl.ds(0, N_META), :] = f_meta
    acol_ref[pl.ds(0, N_META), :] = z_ref[pl.ds(0, N_META), gi_off:gi_off + LANES] + bias_i - f_meta

    def finish_body(c, carry):
        r0 = pl.multiple_of(N_META + c * LANES, 8)
        lf = _log_sigmoid(z_ref[pl.ds(r0, LANES), gf_off:gf_off + LANES] + bias_f)
        p = fcol_ref[pl.ds(r0, LANES), :]
        f = f_meta_end + jnp.where(lane == 0, p, total - p + lf)
        fcol_ref[pl.ds(r0, LANES), :] = f
        a = z_ref[pl.ds(r0, LANES), gi_off:gi_off + LANES] + bias_i - f
        acol_ref[pl.ds(r0, LANES), :] = a
        l0 = pl.multiple_of(c * LANES, LANES)
        atr_ref[:, pl.ds(l0, LANES)] = a.T[0:8, :]
        kt_ref[:, pl.ds(l0, LANES)] = z_ref[pl.ds(r0, LANES), HEAD_DIM:2 * HEAD_DIM].T
        vaug_ref[pl.ds(r0, LANES), :] = jnp.concatenate(
            [z_ref[pl.ds(r0, LANES), 2 * HEAD_DIM:3 * HEAD_DIM], ones_col], axis=1).astype(BF16)
        return carry

    ones_col = jnp.where(lax.broadcasted_iota(jnp.int32, (LANES, LANES), 1) == 0, 1.0, 0.0)
    lax.fori_loop(0, n_real // LANES, finish_body, 0, unroll=4)
    atm_ref[...] = acol_ref[pl.ds(0, LANES), :].T[0:8, :]
    kt_meta = z_ref[pl.ds(0, LANES), HEAD_DIM:2 * HEAD_DIM].T[:, 0:N_META]
    vaug_meta = jnp.concatenate([z_ref[pl.ds(0, N_META), 2 * HEAD_DIM:3 * HEAD_DIM],
                                 jnp.where(lax.broadcasted_iota(jnp.int32, (N_META, LANES), 1) == 0, 1.0, 0.0)],
                                axis=1).astype(BF16)

    mask_m = _tri_mask(N_META, True)
    states = []
    for d in (0, 1):
        state = (jnp.zeros((HEAD_DIM, 2 * HEAD_DIM), F32), jnp.zeros((1, 1), F32))
        out_m, state = _mlstm_chunk(
            z_ref[pl.ds(0, N_META), 0:HEAD_DIM], kt_meta, vaug_meta,
            atm_ref[pl.ds(d, 1), 0:N_META], fcol_ref[pl.ds(0, N_META), d:d + 1], mask_m, state)
        acc_ref[d, pl.ds(0, N_META), :] = out_m
        states.append(state)

    def body(i, carry):
        new = []
        for d in (0, 1):
            c = i if d == 0 else n_chunks - 1 - i
            r0 = pl.multiple_of(N_META + c * t, 8)
            l0 = pl.multiple_of(c * t, LANES)
            out, st = _mlstm_chunk(
                z_ref[pl.ds(r0, t), 0:HEAD_DIM], kt_ref[:, pl.ds(l0, t)], vaug_ref[pl.ds(r0, t), :],
                atr_ref[pl.ds(d, 1), pl.ds(l0, t)], fcol_ref[pl.ds(r0, t), d:d + 1],
                _tri_mask(t, d == 0), carry[d])
            acc_ref[d, pl.ds(r0, t), :] = out
            new.append(st)
        return tuple(new)

    lax.fori_loop(0, n_chunks, body, tuple(states))

    for r0, nr in _row_blocks(n_rows):
        hs = acc_ref[0, pl.ds(r0, nr), :] + acc_ref[1, pl.ds(r0, nr), :]
        o = z_ref[pl.ds(r0, nr), 3 * HEAD_DIM:4 * HEAD_DIM]
        out_ref[0, pl.ds(r0, nr), :] = (_sigmoid(o) * _rms(hs, ng_ref[0])).astype(BF16)


def _mlstm_call(h, gn, w, conv, gb, ng):
    bsz, n_rows, d = h.shape
    nw = w.shape[-1]
    n_real = n_rows - N_META
    return pl.pallas_call(
        _mlstm_kernel,
        grid=(bsz, N_HEADS),
        in_specs=[
            pl.BlockSpec((1, n_rows, d), lambda b, hd: (b, 0, 0)),
            pl.BlockSpec((1, d), lambda b, hd: (0, 0)),
            pl.BlockSpec((1, d, nw), lambda b, hd: (hd, 0, 0)),
            pl.BlockSpec((1, CONV_K, 2 * HEAD_DIM), lambda b, hd: (hd, 0, 0)),
            pl.BlockSpec((1, 1, 2 * LANES), lambda b, hd: (hd, 0, 0)),
            pl.BlockSpec((1, 1, HEAD_DIM), lambda b, hd: (hd, 0, 0)),
        ],
        out_specs=pl.BlockSpec((1, n_rows, HEAD_DIM), lambda b, hd: (b, 0, hd)),
        out_shape=jax.ShapeDtypeStruct((bsz, n_rows, N_HEADS * HEAD_DIM), BF16),
        scratch_shapes=[
            pltpu.VMEM((n_rows, d), BF16),
            pltpu.VMEM((n_rows, nw), F32),
            pltpu.VMEM((n_rows + 2 * CONV_PAD, 2 * HEAD_DIM), F32),
            pltpu.VMEM((2, n_rows, HEAD_DIM), F32),
            pltpu.VMEM((n_rows, LANES), F32),
            pltpu.VMEM((n_rows, LANES), F32),
            pltpu.VMEM((8, LANES), F32),
            pltpu.VMEM((8, n_real), F32),
            pltpu.VMEM((HEAD_DIM, n_real), F32),
            pltpu.VMEM((n_rows, 2 * HEAD_DIM), BF16),
        ],
        compiler_params=_cparams(2),
        name="mlstm_heads",
    )(h, gn, w, conv, gb, ng)


def _ret_chunk(qc, kt, vc, dmat, dq, dk_row, dchunk, r_st):
    qb, vb = qc.astype(BF16), vc.astype(BF16)
    s = _dot(qb, kt.astype(BF16)) * dmat
    out = _dot(s.astype(BF16), vb) + dq * _dot(qb, r_st.astype(BF16))
    r_new = dchunk * r_st + _dot((kt * dk_row).astype(BF16), vb)
    return out, r_new


def _decay_tables(n, lg, forward):
    r = lax.broadcasted_iota(jnp.int32, (n, n), 0)
    c = lax.broadcasted_iota(jnp.int32, (n, n), 1)
    pos_q = lax.broadcasted_iota(jnp.int32, (n, 1), 0).astype(F32)
    pos_k = lax.broadcasted_iota(jnp.int32, (1, n), 1).astype(F32)
    if forward:
        dist, mask = (r - c).astype(F32), c <= r
        dq, dk = jnp.exp((pos_q + 1.0) * lg), jnp.exp((n - 1.0 - pos_k) * lg)
    else:
        dist, mask = (c - r).astype(F32), c >= r
        dq, dk = jnp.exp((n - pos_q) * lg), jnp.exp(pos_k * lg)
    dmat = jnp.exp(jnp.where(mask, dist * lg, NEG))
    return dmat, dq, dk, jnp.exp(n * lg)


def _ret_kernel(h_ref, gn_ref, w_ref, cos_ref, sin_ref, dl_ref, ng_ref, out_ref,
                un_ref, z_ref, acc_ref, dmat_ref, dq_ref, kt_ref):
    n_rows = h_ref.shape[1]
    n_real = n_rows - N_META
    t = SCAN_T
    n_chunks = n_real // t

    @pl.when(pl.program_id(1) == 0)
    def _():
        _norm_to_scratch(h_ref, gn_ref, un_ref)

    _project(un_ref, w_ref, z_ref)

    for r0, nr in _row_blocks(n_rows):
        cs, sn = cos_ref[pl.ds(r0, nr), :], sin_ref[pl.ds(r0, nr), :]
        q = z_ref[pl.ds(r0, nr), 0:HEAD_DIM]
        z_ref[pl.ds(r0, nr), 0:HEAD_DIM] = q * cs + pltpu.roll(q, HEAD_DIM // 2, 1) * sn
        k = z_ref[pl.ds(r0, nr), HEAD_DIM:2 * HEAD_DIM]
        z_ref[pl.ds(r0, nr), HEAD_DIM:2 * HEAD_DIM] = (
            (k * cs + pltpu.roll(k, HEAD_DIM // 2, 1) * sn) * HEAD_DIM ** -0.5)

    for c in range(n_real // LANES):
        kt_ref[:, c * LANES:(c + 1) * LANES] = z_ref[pl.ds(N_META + c * LANES, LANES), HEAD_DIM:2 * HEAD_DIM].T
    kt_meta = z_ref[pl.ds(0, LANES), HEAD_DIM:2 * HEAD_DIM].T[:, 0:N_META]

    lg_all = _log_sigmoid(dl_ref[0])
    states, dk_rows, dchunks = [], [], []
    for d in (0, 1):
        lg = lg_all[:, d:d + 1]
        dmat_m, _, dk_m, dch_m = _decay_tables(N_META, lg, True)
        dmat, dq, dk, dch = _decay_tables(t, lg, d == 0)
        dmat_ref[d] = dmat
        dq_ref[d] = jnp.broadcast_to(dq, (t, LANES))
        dk_rows.append(dk)
        dchunks.append(dch)
        out_m, r_st = _ret_chunk(
            z_ref[pl.ds(0, N_META), 0:HEAD_DIM], kt_meta, z_ref[pl.ds(0, N_META), 2 * HEAD_DIM:3 * HEAD_DIM],
            dmat_m, jnp.zeros((N_META, 1), F32), dk_m, dch_m, jnp.zeros((HEAD_DIM, HEAD_DIM), F32))
        acc_ref[d, pl.ds(0, N_META), :] = out_m
        states.append(r_st)

    def body(i, carry):
        new = []
        for d in (0, 1):
            c = i if d == 0 else n_chunks - 1 - i
            r0 = pl.multiple_of(N_META + c * t, 8)
            l0 = pl.multiple_of(c * t, LANES)
            out, st = _ret_chunk(
                z_ref[pl.ds(r0, t), 0:HEAD_DIM], kt_ref[:, pl.ds(l0, t)],
                z_ref[pl.ds(r0, t), 2 * HEAD_DIM:3 * HEAD_DIM],
                dmat_ref[d], dq_ref[d], dk_rows[d], dchunks[d], carry[d])
            acc_ref[d, pl.ds(r0, t), :] = out
            new.append(st)
        return tuple(new)

    lax.fori_loop(0, n_chunks, body, tuple(states))

    for r0, nr in _row_blocks(n_rows):
        hs = acc_ref[0, pl.ds(r0, nr), :] + acc_ref[1, pl.ds(r0, nr), :]
        g = z_ref[pl.ds(r0, nr), 3 * HEAD_DIM:4 * HEAD_DIM]
        out_ref[0, pl.ds(r0, nr), :] = (g * _sigmoid(g) * _rms(hs, ng_ref[0])).astype(BF16)


def _ret_call(h, gn, w, cos2, sin2, dl, ng):
    bsz, n_rows, d = h.shape
    nw = w.shape[-1]
    return pl.pallas_call(
        _ret_kernel,
        grid=(bsz, N_HEADS),
        in_specs=[
            pl.BlockSpec((1, n_rows, d), lambda b, hd: (b, 0, 0)),
            pl.BlockSpec((1, d), lambda b, hd: (0, 0)),
            pl.BlockSpec((1, d, nw), lambda b, hd: (hd, 0, 0)),
            pl.BlockSpec((n_rows, HEAD_DIM), lambda b, hd: (0, 0)),
            pl.BlockSpec((n_rows, HEAD_DIM), lambda b, hd: (0, 0)),
            pl.BlockSpec((1, 1, LANES), lambda b, hd: (hd, 0, 0)),
            pl.BlockSpec((1, 1, HEAD_DIM), lambda b, hd: (hd, 0, 0)),
        ],
        out_specs=pl.BlockSpec((1, n_rows, HEAD_DIM), lambda b, hd: (b, 0, hd)),
        out_shape=jax.ShapeDtypeStruct((bsz, n_rows, N_HEADS * HEAD_DIM), BF16),
        scratch_shapes=[
            pltpu.VMEM((n_rows, d), BF16),
            pltpu.VMEM((n_rows, nw), F32),
            pltpu.VMEM((2, n_rows, HEAD_DIM), F32),
            pltpu.VMEM((2, SCAN_T, SCAN_T), F32),
            pltpu.VMEM((2, SCAN_T, LANES), F32),
            pltpu.VMEM((HEAD_DIM, n_rows - N_META), F32),
        ],
        compiler_params=_cparams(2),
        name="retention_heads",
    )(h, gn, w, cos2, sin2, dl, ng)


def _head_pair_block_diag(qp):
    lane = lax.broadcasted_iota(jnp.int32, qp.shape, 1)
    zero = jnp.zeros_like(qp)
    return jnp.concatenate([jnp.where(lane < NA_HEAD_DIM, qp, zero),
                            jnp.where(lane >= NA_HEAD_DIM, qp, zero)], axis=0)


def _na_attend(qp, keys, values, bias_t):
    n = qp.shape[0]
    bd = _head_pair_block_diag(qp)
    scores = []
    for kb, bt in zip(keys, bias_t):
        s = _dot_nt(kb, bd)
        scores.append(s if bt is None else s + bt)
    m = scores[0].max(axis=0, keepdims=True)
    for s in scores[1:]:
        m = jnp.maximum(m, s.max(axis=0, keepdims=True))
    probs = [jnp.exp(s - m) for s in scores]
    den = probs[0].sum(axis=0, keepdims=True)
    for p in probs[1:]:
        den = den + p.sum(axis=0, keepdims=True)
    inv = 1.0 / den
    o2 = None
    for p, vb in zip(probs, values):
        part = _dot_tn((p * inv).astype(BF16), vb)
        o2 = part if o2 is None else o2 + part
    lane = lax.broadcasted_iota(jnp.int32, (n, 2 * NA_HEAD_DIM), 1)
    return jnp.where(lane < NA_HEAD_DIM, o2[0:n, :], o2[n:2 * n, :])


def _na_kernel(h_ref, gn_ref, w_ref, qkg_ref, bias_ref, out_ref,
               un_ref, z_ref, q_ref, k_ref, vt_ref, ot_ref, s_ref, p_ref, den_ref):
    n_rows = h_ref.shape[1]
    n_real = n_rows - N_META
    n_grid_rows = n_real // GRID_W
    dh = NA_HEAD_DIM
    pw = 2 * dh
    band = NA_WIN_ROWS * GRID_W
    pairs = range(NA_PAIRS)
    v_off = NA_PAIRS * 2 * pw

    @pl.when(pl.program_id(1) == 0)
    def _():
        _norm_to_scratch(h_ref, gn_ref, un_ref)

    _project(un_ref, w_ref, z_ref)

    r = lax.broadcasted_iota(jnp.int32, (2 * pw, 2 * pw), 0)
    c = lax.broadcasted_iota(jnp.int32, (2 * pw, 2 * pw), 1)
    head_ones = jnp.where(r // dh == c // dh, 1.0, 0.0).astype(BF16)
    for p in pairs:
        for r0, nr in _row_blocks(n_rows):
            x = z_ref[pl.ds(r0, nr), p * 2 * pw:(p + 1) * 2 * pw]
            sq = x * x
            hi = sq.astype(BF16)
            lo = (sq - hi.astype(F32)).astype(BF16)
            ssq = _dot(hi, head_ones) + _dot(lo, head_ones)
            y = (x * lax.rsqrt(ssq * (1.0 / dh) + EPS) * qkg_ref[...]).astype(BF16)
            q_ref[p, pl.ds(r0, nr), :] = y[:, 0:pw]
            k_ref[p, pl.ds(r0, nr), :] = y[:, pw:2 * pw]

    n_tblocks = n_real // LANES
    k_meta, vt_meta = [], []
    for p in pairs:
        v_cols = slice(v_off + p * pw, v_off + (p + 1) * pw)
        for cpy in (0, 1):
            for c in range(n_tblocks - cpy):
                vb = z_ref[pl.ds(N_META + cpy * GRID_W + c * LANES, LANES), v_cols]
                vt_ref[p, cpy, :, c * LANES:(c + 1) * LANES] = vb.T.astype(BF16)
        vt_ref[p, 1, :, (n_tblocks - 1) * LANES:n_tblocks * LANES] = jnp.zeros((pw, LANES), BF16)
        vt_meta.append(z_ref[pl.ds(0, LANES), v_cols].T[:, 0:N_META].astype(BF16))
        k_meta.append(k_ref[p, pl.ds(0, N_META), :])
        out_ref[0, pl.ds(0, N_META), p * pw:(p + 1) * pw] = _na_attend(
            q_ref[p, pl.ds(0, N_META), :], [k_meta[p]], [z_ref[pl.ds(0, N_META), v_cols].astype(BF16)],
            [None]).astype(BF16)

    sub = lax.broadcasted_iota(jnp.int32, (pw, pw), 0)
    lane = lax.broadcasted_iota(jnp.int32, (pw, pw), 1)

    def row_start(r):
        return jnp.clip(r - NA_WIN_ROWS // 2, 0, n_grid_rows - NA_WIN_ROWS)

    def stage_scores(r, slot):
        rs = row_start(r)
        q0 = pl.multiple_of(N_META + r * GRID_W, 8)
        k0 = pl.multiple_of(N_META + rs * GRID_W, 8)
        for p in pairs:
            bd = _head_pair_block_diag(q_ref[p, pl.ds(q0, GRID_W), :])
            bias = bias_ref[p, pl.ds(NA_WIN_ROWS - 1 - (r - rs), NA_WIN_ROWS)].reshape(band, pw)
            s_ref[p, slot, pl.ds(0, band), :] = _dot_nt(k_ref[p, pl.ds(k0, band), :], bd) + bias
            s_ref[p, slot, pl.ds(band, N_META), :] = _dot_nt(k_meta[p], bd)

    def stage_softmax(slot):
        for p in pairs:
            s = s_ref[p, slot]
            e = jnp.exp(s - s.max(axis=0, keepdims=True))
            den_ref[p, slot] = e.sum(axis=0, keepdims=True)
            p_ref[p, slot] = e.astype(BF16)

    def stage_values(r, slot):
        rs = row_start(r)
        par = rs % 2
        l0 = pl.multiple_of((rs - par) * GRID_W, LANES)
        for p in pairs:
            o_t = (_dot(vt_ref[p, par, :, pl.ds(l0, band)], p_ref[p, slot, pl.ds(0, band), :])
                   + _dot(vt_meta[p], p_ref[p, slot, pl.ds(band, N_META), :]))
            ot_ref[p, r] = o_t / den_ref[p, slot]

    stage_scores(0, 0)
    stage_softmax(0)
    stage_scores(1, 1)

    def pipe_body(j, carry):
        i0 = 2 + 2 * j
        stage_values(i0 - 2, 0)
        stage_softmax(1)
        stage_scores(i0, 0)
        stage_values(i0 - 1, 1)
        stage_softmax(0)
        stage_scores(i0 + 1, 1)
        return carry

    lax.fori_loop(0, (n_grid_rows - 2) // 2, pipe_body, 0)
    stage_values(n_grid_rows - 2, 0)
    stage_softmax(1)
    stage_values(n_grid_rows - 1, 1)

    for p in pairs:
        for c in range(n_tblocks):
            oa, ob = ot_ref[p, 2 * c], ot_ref[p, 2 * c + 1]
            sel_a = jnp.where(sub < dh, oa, pltpu.roll(oa, dh, 1))
            sel_b = jnp.where(sub < dh, pltpu.roll(ob, dh, 1), ob)
            out_ref[0, pl.ds(N_META + c * LANES, LANES), p * pw:(p + 1) * pw] = (
                jnp.where(lane < dh, sel_a, sel_b).T.astype(BF16))


def _na_call(h, gn, w, qg, kg, bias):
    bsz, n_rows, d = h.shape
    n_steps, _, nw = w.shape
    pw = 2 * NA_HEAD_DIM
    band = NA_WIN_ROWS * GRID_W
    n_real = n_rows - N_META
    qkg = jnp.concatenate([jnp.tile(qg, 2) * NA_HEAD_DIM ** -0.5, jnp.tile(kg, 2)])[None, :].astype(F32)
    return pl.pallas_call(
        _na_kernel,
        grid=(bsz, n_steps),
        in_specs=[
            pl.BlockSpec((1, n_rows, d), lambda b, s: (b, 0, 0)),
            pl.BlockSpec((1, d), lambda b, s: (0, 0)),
            pl.BlockSpec((1, d, nw), lambda b, s: (s, 0, 0)),
            pl.BlockSpec((1, 2 * pw), lambda b, s: (0, 0)),
            pl.BlockSpec((NA_PAIRS,) + bias.shape[1:], lambda b, s: (s, 0, 0, 0)),
        ],
        out_specs=pl.BlockSpec((1, n_rows, NA_PAIRS * pw), lambda b, s: (b, 0, s)),
        out_shape=jax.ShapeDtypeStruct((bsz, n_rows, n_steps * NA_PAIRS * pw), BF16),
        scratch_shapes=[
            pltpu.VMEM((n_rows, d), BF16),
            pltpu.VMEM((n_rows, nw), F32),
            pltpu.VMEM((NA_PAIRS, n_rows, pw), BF16),
            pltpu.VMEM((NA_PAIRS, n_rows, pw), BF16),
            pltpu.VMEM((NA_PAIRS, 2, pw, n_real), BF16),
            pltpu.VMEM((NA_PAIRS, n_real // GRID_W, pw, pw), F32),
            pltpu.VMEM((NA_PAIRS, 2, band + N_META, pw), F32),
            pltpu.VMEM((NA_PAIRS, 2, band + N_META, pw), BF16),
            pltpu.VMEM((NA_PAIRS, 2, 1, pw), F32),
        ],
        compiler_params=_cparams(2),
        name="neighbourhood_attention",
    )(h, gn, w, qkg, bias)


def _lane_prefix_exclusive(x, tri_strict):
    n_blocks = x.shape[1] // LANES
    carry = jnp.zeros((x.shape[0], 1), F32)
    pieces = []
    for j in range(n_blocks):
        blk = x[:, j * LANES:(j + 1) * LANES]
        pieces.append(_dot(blk.astype(BF16), tri_strict) + carry)
        carry = carry + jnp.sum(blk, axis=-1, keepdims=True)
    return jnp.concatenate(pieces, axis=1)


def _aligned_row_blocks(n_rows, step=512):
    blocks, r0 = [], 0
    while r0 < n_rows:
        nr = min(step, n_rows - r0)
        blocks.append((r0, nr))
        r0 += nr
    return blocks


def _post_kernel(n_mix, cap, *refs):
    h_ref = refs[0]
    mix_refs = refs[1:1 + n_mix]
    wo_ref, fg_ref, wr_ref = refs[1 + n_mix:4 + n_mix]
    hn_ref, u2_ref, slot_ref, gate_ref = refs[4 + n_mix:8 + n_mix]
    lgc_ref = refs[8 + n_mix]
    n_rows = h_ref.shape[1]
    n_pad = u2_ref.shape[1]

    for r0, nr in _row_blocks(n_rows):
        acc = h_ref[0, pl.ds(r0, nr), :]
        k0 = 0
        for m_ref in mix_refs:
            kw = m_ref.shape[2]
            acc = acc + _dot(m_ref[0, pl.ds(r0, nr), :], wo_ref[pl.ds(k0, kw), :])
            k0 += kw
        hn_ref[0, pl.ds(r0, nr), :] = acc
        u = _rms(acc, fg_ref[...])
        u_hi = u.astype(BF16)
        u2_ref[0, pl.ds(r0, nr), :] = u_hi
        u_lo = (u - u_hi.astype(F32)).astype(BF16)
        hh_hl = _dot(u_hi, wr_ref[...])
        lgc_ref[pl.ds(r0, nr), :] = (hh_hl[:, 0:LANES] + hh_hl[:, LANES:2 * LANES]
                                     + _dot(u_lo, wr_ref[:, 0:LANES]))
    u2_ref[0, pl.ds(n_rows, n_pad - n_rows), :] = jnp.zeros((n_pad - n_rows, u2_ref.shape[2]), BF16)
    lgc_ref[pl.ds(n_rows, n_pad - n_rows), :] = jnp.zeros((n_pad - n_rows, LANES), F32)

    logits = jnp.concatenate(
        [lgc_ref[pl.ds(c * LANES, LANES), :].T[0:N_EXPERTS, :] for c in range(n_pad // LANES)], axis=1)
    ex = jnp.exp(logits - jnp.max(logits, axis=0, keepdims=True))
    aff = ex / jnp.sum(ex, axis=0, keepdims=True)
    tok = lax.broadcasted_iota(jnp.int32, aff.shape, 1)
    aff = jnp.where(tok < n_rows, aff, -1.0)

    def count_ge(x):
        return jnp.sum(jnp.where(aff >= x, 1.0, 0.0), axis=-1, keepdims=True)

    capf = float(cap)
    tiny = jnp.full((aff.shape[0], 1), 2.0 ** -126, F32)
    ok0 = count_ge(tiny) >= capf
    p = tiny
    for j in (64, 32, 16, 8, 4, 2, 1):
        cand = p * (2.0 ** j)
        p = jnp.where(count_ge(cand) >= capf, cand, p)
    m = p
    for i in range(1, 24):
        cand = m + p * (2.0 ** -i)
        m = jnp.where(count_ge(cand) >= capf, cand, m)
    thr = jnp.where(ok0, m, 0.0)

    gt = aff > thr
    eq = aff == thr
    need = capf - jnp.sum(jnp.where(gt, 1.0, 0.0), axis=-1, keepdims=True)
    r = lax.broadcasted_iota(jnp.int32, (LANES, LANES), 0)
    c = lax.broadcasted_iota(jnp.int32, (LANES, LANES), 1)
    tri_strict = jnp.where(r < c, 1.0, 0.0).astype(BF16)
    eq_rank = _lane_prefix_exclusive(jnp.where(eq, 1.0, 0.0), tri_strict)
    sel = gt | (eq & (eq_rank < need))
    pos = _lane_prefix_exclusive(jnp.where(sel, 1.0, 0.0), tri_strict)
    slot_ref[0] = jnp.where(sel, pos, -1.0).astype(jnp.int32)
    gate_ref[0] = jnp.where(sel, aff, 0.0)


def _post_call(h, mixes, wo, fg, wr, cap):
    bsz, n_rows, d = h.shape
    n_pad = ((n_rows + LANES - 1) // LANES) * LANES
    n_mix = len(mixes)
    row_spec = lambda w: pl.BlockSpec((1, n_rows, w), lambda b: (b, 0, 0), pipeline_mode=pl.Buffered(1))
    return pl.pallas_call(
        functools.partial(_post_kernel, n_mix, cap),
        grid=(bsz,),
        in_specs=[row_spec(d)] + [row_spec(m.shape[2]) for m in mixes] + [
            pl.BlockSpec((d, d), lambda b: (0, 0)),
            pl.BlockSpec((1, d), lambda b: (0, 0)),
            pl.BlockSpec((d, 2 * LANES), lambda b: (0, 0)),
        ],
        out_specs=[
            row_spec(d),
            pl.BlockSpec((1, n_pad, d), lambda b: (b, 0, 0), pipeline_mode=pl.Buffered(1)),
            pl.BlockSpec((1, N_EXPERTS, n_pad), lambda b: (b, 0, 0)),
            pl.BlockSpec((1, N_EXPERTS, n_pad), lambda b: (b, 0, 0)),
        ],
        out_shape=[
            jax.ShapeDtypeStruct((bsz, n_rows, d), F32),
            jax.ShapeDtypeStruct((bsz, n_pad, d), BF16),
            jax.ShapeDtypeStruct((bsz, N_EXPERTS, n_pad), jnp.int32),
            jax.ShapeDtypeStruct((bsz, N_EXPERTS, n_pad), F32),
        ],
        scratch_shapes=[pltpu.VMEM((n_pad, LANES), F32)],
        compiler_params=_cparams(1),
        name="outproj_router",
    )(h, *mixes, wo, fg, wr)


def _gather_kernel(u2_ref, slot_ref, xs_ref):
    cap_pad = xs_ref.shape[2]
    n_pad = u2_ref.shape[1]
    srow_id = lax.broadcasted_iota(jnp.int32, (cap_pad, n_pad), 0)

    def body(i, carry):
        pieces = [jnp.where(srow_id == slot_ref[0, pl.ds(i * GATHER_GROUP + j, 1), :], 1.0, 0.0).astype(BF16)
                  for j in range(GATHER_GROUP)]
        rows = _dot(jnp.concatenate(pieces, axis=0), u2_ref[0]).astype(BF16)
        for j in range(GATHER_GROUP):
            xs_ref[0, i * GATHER_GROUP + j] = rows[j * cap_pad:(j + 1) * cap_pad, :]
        return carry

    lax.fori_loop(0, N_EXPERTS // GATHER_GROUP, body, 0)


def _gather_call(u2, slot, cap_pad):
    bsz, n_pad, d = u2.shape
    return pl.pallas_call(
        _gather_kernel,
        grid=(bsz,),
        in_specs=[
            pl.BlockSpec((1, n_pad, d), lambda b: (b, 0, 0)),
            pl.BlockSpec((1, N_EXPERTS, n_pad), lambda b: (b, 0, 0)),
        ],
        out_specs=pl.BlockSpec((1, N_EXPERTS, cap_pad, d), lambda b: (b, 0, 0, 0)),
        out_shape=jax.ShapeDtypeStruct((bsz, N_EXPERTS, cap_pad, d), BF16),
        compiler_params=_cparams(1),
        name="expert_gather",
    )(u2, slot)


def _expert_kernel(xs_ref, wg_ref, wu_ref, wd_ref, ys_ref, wgb_ref, wub_ref, wdb_ref):
    bb, _, cap_pad, d = xs_ref.shape

    @pl.when(pl.program_id(1) == 0)
    def _():
        wgb_ref[...] = wg_ref[0].astype(BF16)
        wub_ref[...] = wu_ref[0].astype(BF16)
        wdb_ref[...] = wd_ref[0].astype(BF16)

    x = xs_ref[...].reshape(bb * cap_pad, d)
    g = _dot(x, wgb_ref[...])
    u = _dot(x, wub_ref[...])
    hdn = (g * _sigmoid(g) * u).astype(BF16)
    ys_ref[...] = _dot(hdn, wdb_ref[...]).astype(BF16).reshape(ys_ref.shape)


def _expert_call(xs, wg, wu, wd, layer):
    bsz, n_e, cap_pad, d = xs.shape
    ff = wg.shape[-1]
    seq_blocks = 4 if bsz % 4 == 0 else 1
    bb = bsz // seq_blocks
    return pl.pallas_call(
        _expert_kernel,
        grid=(n_e, seq_blocks),
        in_specs=[
            pl.BlockSpec((bb, 1, cap_pad, d), lambda e, m: (m, e, 0, 0)),
            pl.BlockSpec((1, None, d, ff), lambda e, m: (layer, e, 0, 0)),
            pl.BlockSpec((1, None, d, ff), lambda e, m: (layer, e, 0, 0)),
            pl.BlockSpec((1, None, ff, d), lambda e, m: (layer, e, 0, 0)),
        ],
        out_specs=pl.BlockSpec((bb, 1, cap_pad, d), lambda e, m: (m, e, 0, 0)),
        out_shape=jax.ShapeDtypeStruct((bsz, n_e, cap_pad, d), BF16),
        scratch_shapes=[pltpu.VMEM((d, ff), BF16), pltpu.VMEM((d, ff), BF16), pltpu.VMEM((ff, d), BF16)],
        compiler_params=_cparams(2),
        name="expert_swiglu",
    )(xs, wg, wu, wd)


def _combine_kernel(h_ref, ym_ref, yt_ref, slot_ref, gate_ref, out_ref):
    n_rows = h_ref.shape[1]
    n_out = out_ref.shape[1]
    skip = n_rows - n_out
    _, e_grp, n_main, d = ym_ref.shape
    _, n_e, n_tail, _ = yt_ref.shape
    n_pad = slot_ref.shape[2]
    g = pl.program_id(1)
    n_main_steps = n_e // e_grp

    @pl.when(g == 0)
    def _():
        out_ref[0] = h_ref[0, pl.ds(skip, n_out), :]

    def weights_for(e, first_slot, n_slots):
        srow_id = lax.broadcasted_iota(jnp.int32, (n_slots, n_pad), 0) + first_slot
        hit = srow_id == slot_ref[0, pl.ds(e, 1), :]
        return jnp.where(hit, gate_ref[0, pl.ds(e, 1), :], 0.0).astype(BF16)

    def accumulate(weights, ys):
        for t0, tn in _aligned_row_blocks(n_pad, 1024):
            lo, hi = max(t0, skip), min(t0 + tn, n_rows)
            if hi <= lo:
                continue
            part = _dot_tn(weights[:, t0:t0 + tn], ys)
            out_ref[0, pl.ds(lo - skip, hi - lo), :] += part[lo - t0:hi - t0, :]

    @pl.when(g < n_main_steps)
    def _():
        weights = jnp.concatenate([weights_for(g * e_grp + j, 0, n_main) for j in range(e_grp)], axis=0)
        accumulate(weights, ym_ref[0].reshape(e_grp * n_main, d))

    @pl.when(g == n_main_steps)
    def _():
        weights = jnp.concatenate([weights_for(e, n_main, n_tail) for e in range(n_e)], axis=0)
        accumulate(weights, yt_ref[0].reshape(n_e * n_tail, d))


def _combine_call(h, ys, slot, gate, n_out, e_grp=4):
    bsz, n_rows, d = h.shape
    _, n_e, cap_pad, _ = ys.shape
    n_pad = slot.shape[2]
    n_main_steps = n_e // e_grp
    n_tail = cap_pad - MXU_DEPTH
    return pl.pallas_call(
        _combine_kernel,
        grid=(bsz, n_main_steps + 1),
        in_specs=[
            pl.BlockSpec((1, n_rows, d), lambda b, g: (b, 0, 0)),
            pl.BlockSpec((1, e_grp, MXU_DEPTH, d), lambda b, g: (b, jnp.minimum(g, n_main_steps - 1), 0, 0)),
            pl.BlockSpec((1, n_e, n_tail, d), lambda b, g: (b, 0, MXU_DEPTH // n_tail, 0)),
            pl.BlockSpec((1, n_e, n_pad), lambda b, g: (b, 0, 0)),
            pl.BlockSpec((1, n_e, n_pad), lambda b, g: (b, 0, 0)),
        ],
        out_specs=pl.BlockSpec((1, n_out, d), lambda b, g: (b, 0, 0)),
        out_shape=jax.ShapeDtypeStruct((bsz, n_out, d), F32),
        compiler_params=_cparams(2),
        name="expert_combine",
    )(h, ys, ys, slot, gate)


def _even_weights(w_in, conv_w, gate_b, decay_logit):
    d = w_in.shape[0]
    mw = N_HEADS * HEAD_DIM
    mq, mk, mv, mo = (w_in[:, i * mw:(i + 1) * mw] for i in range(4))
    gates = w_in[:, 4 * mw:4 * mw + 4 * N_HEADS]
    r0 = 4 * mw + 4 * N_HEADS
    rq, rk, rv, rg = (w_in[:, r0 + i * mw:r0 + (i + 1) * mw] for i in range(4))

    def per_head(t):
        return t.reshape(d, N_HEADS, HEAD_DIM).transpose(1, 0, 2)

    def gate_cols(fw_off, bw_off):
        cols = jnp.stack([gates[:, fw_off:fw_off + N_HEADS], gates[:, bw_off:bw_off + N_HEADS]], axis=-1)
        cols = cols.transpose(1, 0, 2)
        return jnp.pad(cols, ((0, 0), (0, 0), (0, LANES - 2)))

    def gate_bias(fw_off, bw_off):
        b = jnp.stack([gate_b[fw_off:fw_off + N_HEADS], gate_b[bw_off:bw_off + N_HEADS]], axis=-1)
        return jnp.pad(b, ((0, 0), (0, LANES - 2)))

    w_m = jnp.concatenate([per_head(mq), per_head(mk), per_head(mv), per_head(mo),
                           gate_cols(0, 2 * N_HEADS), gate_cols(N_HEADS, 3 * N_HEADS)], axis=-1).astype(BF16)
    gb = jnp.concatenate([gate_bias(0, 2 * N_HEADS), gate_bias(N_HEADS, 3 * N_HEADS)], axis=-1)[:, None, :]
    conv = jnp.concatenate([conv_w[:, :mw].reshape(CONV_K, N_HEADS, HEAD_DIM),
                            conv_w[:, mw:].reshape(CONV_K, N_HEADS, HEAD_DIM)], axis=-1).transpose(1, 0, 2)
    w_r = jnp.concatenate([per_head(rq), per_head(rk), per_head(rv), per_head(rg)], axis=-1).astype(BF16)
    dl = jnp.pad(decay_logit.T, ((0, 0), (0, LANES - 2)))[:, None, :]
    return w_m, gb.astype(F32), conv.astype(F32), w_r, dl.astype(F32)


def _rotary_tables(n_rows):
    half = HEAD_DIM // 2
    inv = ROPE_BASE ** (-jnp.arange(half, dtype=F32) / half)
    ang = jnp.arange(n_rows, dtype=F32)[:, None] * inv[None, :]
    cos, sin = jnp.cos(ang), jnp.sin(ang)
    return jnp.concatenate([cos, cos], axis=-1), jnp.concatenate([-sin, sin], axis=-1)


def _na_weights(w_in):
    d = w_in.shape[0]
    n_heads = w_in.shape[1] // (3 * NA_HEAD_DIM)
    pw = 2 * NA_HEAD_DIM
    n_steps = n_heads // (2 * NA_PAIRS)
    z = w_in.reshape(d, 3, n_steps, NA_PAIRS, pw)
    qk = z[:, 0:2].transpose(2, 0, 3, 1, 4).reshape(n_steps, d, NA_PAIRS * 2 * pw)
    v = z[:, 2].transpose(1, 0, 2, 3).reshape(n_steps, d, NA_PAIRS * pw)
    return jnp.concatenate([qk, v], axis=-1).astype(BF16)


def _na_bias_table(rpb):
    col = np.arange(GRID_W)
    col_start = np.clip(col - NA_WIN_COLS // 2, 0, GRID_W - NA_WIN_COLS)
    col_in = (col[None, :] >= col_start[:, None]) & (col[None, :] < col_start[:, None] + NA_WIN_COLS)
    dc_idx = np.clip(col[None, :] - col[:, None], -(NA_WIN_COLS - 1), NA_WIN_COLS - 1) + NA_WIN_COLS - 1
    selector = (np.arange(rpb.shape[2])[:, None, None] == dc_idx[None]).astype(np.float32)
    rpb_cols = jnp.einsum('hrd,dqk->hrqk', rpb.astype(F32), selector,
                          precision=lax.Precision.HIGHEST)
    tbl = jnp.where(col_in[None, None], rpb_cols, NEG)
    n_pairs, n_dr = rpb.shape[0] // 2, rpb.shape[1]
    tbl = tbl.reshape(n_pairs, 2, n_dr, GRID_W, GRID_W).transpose(0, 2, 4, 1, 3)
    return tbl.reshape(n_pairs, n_dr, GRID_W, 2 * GRID_W)


def _ffn(h, mixes, wo, fg, wr, wg, wu, wd, layer, n_out):
    n_rows = h.shape[1]
    cap = CAP_FACTOR * n_rows // N_EXPERTS
    cap_pad = ((cap + BF16_ROWS - 1) // BF16_ROWS) * BF16_ROWS
    wr_hi = wr.astype(BF16)
    wr_lo = (wr - wr_hi.astype(F32)).astype(BF16)
    pad = ((0, 0), (0, LANES - N_EXPERTS))
    wr_pieces = jnp.concatenate([jnp.pad(wr_hi, pad), jnp.pad(wr_lo, pad)], axis=1)
    hn, u2, slot, gate = _post_call(h, mixes, wo.astype(BF16), fg[None, :], wr_pieces, cap)
    xs = _gather_call(u2, slot, cap_pad)
    ys = _expert_call(xs, wg, wu, wd, layer)
    return _combine_call(hn, ys, slot, gate, n_out)


def kernel(x, meta_tokens, attn_norm_g, ffn_norm_g, even_w_in, even_conv_w, even_gate_b, even_m_norm_g, even_ret_decay_logit, even_r_norm_g, even_w_out, odd_w_in, odd_q_norm_g, odd_k_norm_g, odd_rpb, odd_w_out, router_w, expert_w_gate, expert_w_up, expert_w_down):
    bsz = x.shape[0]
    depth = attn_norm_g.shape[0]
    meta = jnp.broadcast_to(meta_tokens.astype(x.dtype)[None], (bsz,) + meta_tokens.shape)
    h = jnp.concatenate([meta, x], axis=1)
    n_rows = h.shape[1]
    cos2, sin2 = _rotary_tables(n_rows)
    for layer in range(depth):
        j = layer // 2
        gn = attn_norm_g[layer][None, :]
        if layer % 2 == 0:
            w_m, gb, conv, w_r, dl = _even_weights(even_w_in[j], even_conv_w[j], even_gate_b[j],
                                                   even_ret_decay_logit[j])
            m_out = _mlstm_call(h, gn, w_m, conv, gb, even_m_norm_g[j].reshape(N_HEADS, 1, HEAD_DIM))
            r_out = _ret_call(h, gn, w_r, cos2, sin2, dl, even_r_norm_g[j].reshape(N_HEADS, 1, HEAD_DIM))
            mixes, wo = [m_out, r_out], even_w_out[j]
        else:
            a_out = _na_call(h, gn, _na_weights(odd_w_in[j]), odd_q_norm_g[j], odd_k_norm_g[j],
                             _na_bias_table(odd_rpb[j]))
            mixes, wo = [a_out], odd_w_out[j]
        n_out = n_rows - N_META if layer == depth - 1 else n_rows
        h = _ffn(h, mixes, wo, ffn_norm_g[layer], router_w[layer],
                 expert_w_gate, expert_w_up, expert_w_down, layer, n_out)
    return h
```

```python
import functools

import jax
import jax.numpy as jnp
import numpy as np
from jax import lax
from jax.experimental import pallas as pl
from jax.experimental.pallas import tpu as pltpu

F32 = jnp.float32
BF16 = jnp.bfloat16

LANES = 128
BF16_ROWS = 16
MXU_DEPTH = 256
N_META = 16
GRID_W = 64
EPS = 1e-6
HEAD_DIM = 128
N_HEADS = 4
CONV_K = 5
CONV_PAD = 8
ROPE_BASE = 10000.0
NA_HEAD_DIM = 64
NA_WIN_ROWS = 8
NA_WIN_COLS = 16
NA_PAIRS = 2
N_EXPERTS = 16
CAP_FACTOR = 2
GATHER_GROUP = 2
NEG = -1e30
SCAN_T = 256
VMEM_LIMIT = 56 * 1024 * 1024


def _cparams(n_axes):
    return pltpu.CompilerParams(
        dimension_semantics=("arbitrary",) * n_axes, vmem_limit_bytes=VMEM_LIMIT)


def _row_blocks(n_rows):
    for nb in (3, 2, 4, 6, 1):
        if n_rows % (8 * nb) == 0:
            step = n_rows // nb
            return [(i * step, step) for i in range(nb)]
    return [(0, n_rows)]


def _rms(x, g):
    return x * lax.rsqrt(jnp.mean(x * x, axis=-1, keepdims=True) + EPS) * g


def _sigmoid(x):
    return 1.0 / (1.0 + jnp.exp(-x))


def _log_sigmoid(x):
    return jnp.minimum(x, 0.0) - jnp.log(1.0 + jnp.exp(-jnp.abs(x)))


def _dot(a, b):
    return jnp.dot(a, b, preferred_element_type=F32)


def _dot_nt(a, b):
    return lax.dot_general(a, b, (((1,), (1,)), ((), ())), preferred_element_type=F32)


def _dot_tn(a, b):
    return lax.dot_general(a, b, (((0,), (0,)), ((), ())), preferred_element_type=F32)


def _split3(x):
    hi = x.astype(BF16)
    r1 = x - hi.astype(F32)
    mid = r1.astype(BF16)
    lo = (r1 - mid.astype(F32)).astype(BF16)
    return hi, mid, lo


def _tri_prefix(tri_bf16, x):
    hi, mid, lo = _split3(x)
    return _dot(tri_bf16, hi) + _dot(tri_bf16, mid) + _dot(tri_bf16, lo)


def _norm_to_scratch(h_ref, gn_ref, un_ref):
    n_rows = h_ref.shape[1]
    for r0, nr in _row_blocks(n_rows):
        x = h_ref[0, pl.ds(r0, nr), :]
        un_ref[pl.ds(r0, nr), :] = _rms(x, gn_ref[...]).astype(BF16)


def _project(un_ref, w_ref, z_ref, c0=0, c1=None):
    n_rows = un_ref.shape[0]
    c1 = z_ref.shape[1] if c1 is None else c1
    for r0, nr in _row_blocks(n_rows):
        z_ref[pl.ds(r0, nr), c0:c1] = _dot(un_ref[pl.ds(r0, nr), :], w_ref[0, :, c0:c1])


def _tri_mask(n, lower):
    r = lax.broadcasted_iota(jnp.int32, (n, n), 0)
    c = lax.broadcasted_iota(jnp.int32, (n, n), 1)
    return (c <= r) if lower else (c >= r)


def _mlstm_chunk(qc, kt, v_aug, a_row, f_col, mask, state):
    c_aug, g_prev = state
    qb = qc.astype(BF16)
    cm = jnp.max(jnp.where(mask, a_row, NEG), axis=-1, keepdims=True)
    g_col = jnp.maximum(g_prev, cm)
    dm = jnp.exp(jnp.where(mask, a_row - g_col, NEG))
    s = _dot(qb, kt.astype(BF16)) * dm
    w_inter = jnp.exp(g_prev - g_col)
    tot = w_inter * _dot(qb, c_aug.astype(BF16)) + _dot(s.astype(BF16), v_aug)
    den = tot[:, HEAD_DIM:2 * HEAD_DIM][:, 0:1]
    out = tot[:, 0:HEAD_DIM] / jnp.maximum(jnp.abs(den), jnp.exp(-(f_col + g_col)))
    g_end = jnp.maximum(g_prev, jnp.max(a_row, axis=-1, keepdims=True))
    ktw = (kt * jnp.exp(a_row - g_end)).astype(BF16)
    c_new = jnp.exp(g_prev - g_end) * c_aug + _dot(ktw, v_aug)
    return out, (c_new, g_end)


def _mlstm_kernel(h_ref, gn_ref, w_ref, conv_ref, gb_ref, ng_ref, out_ref,
                  un_ref, z_ref, zc_ref, acc_ref, acol_ref, fcol_ref, atm_ref, atr_ref, kt_ref, vaug_ref):
    n_rows = h_ref.shape[1]
    n_real = n_rows - N_META
    t = SCAN_T
    n_chunks = n_real // t

    @pl.when(pl.program_id(1) == 0)
    def _():
        _norm_to_scratch(h_ref, gn_ref, un_ref)

    _project(un_ref, w_ref, z_ref, 0, 2 * HEAD_DIM)
    _project(un_ref, w_ref, z_ref, 2 * HEAD_DIM)

    zc_ref[pl.ds(0, CONV_PAD), :] = jnp.zeros((CONV_PAD, 2 * HEAD_DIM), F32)
    zc_ref[pl.ds(CONV_PAD + n_rows, CONV_PAD), :] = jnp.zeros((CONV_PAD, 2 * HEAD_DIM), F32)
    for r0, nr in _row_blocks(n_rows):
        zc_ref[pl.ds(CONV_PAD + r0, nr), :] = z_ref[pl.ds(r0, nr), 0:2 * HEAD_DIM]
    lane2 = lax.broadcasted_iota(jnp.int32, (1, 2 * HEAD_DIM), 1)
    qk_scale = jnp.where(lane2 >= HEAD_DIM, HEAD_DIM ** -0.5, 1.0).astype(F32)
    for r0, nr in _row_blocks(n_rows):
        acc = jnp.zeros((nr, 2 * HEAD_DIM), F32)
        for j in range(CONV_K):
            off = CONV_PAD - (CONV_K - 1) // 2 + j + r0
            acc = acc + zc_ref[pl.ds(off, nr), :] * conv_ref[0, pl.ds(j, 1), :]
        z_ref[pl.ds(r0, nr), 0:2 * HEAD_DIM] = acc * _sigmoid(acc) * qk_scale

    gi_off, gf_off = 4 * HEAD_DIM, 5 * HEAD_DIM
    bias_i = gb_ref[0, :, 0:LANES]
    bias_f = gb_ref[0, :, LANES:2 * LANES]
    lane = lax.broadcasted_iota(jnp.int32, (1, LANES), 1)
    tri_m = jnp.where(_tri_mask(N_META, True), 1.0, 0.0).astype(BF16)
    lf_m = _log_sigmoid(z_ref[pl.ds(0, N_META), gf_off:gf_off + LANES] + bias_f)
    f_meta = _tri_prefix(tri_m, lf_m)
    f_meta_end = f_meta[N_META - 1:N_META, :]
    tri_b = jnp.where(_tri_mask(LANES, True), 1.0, 0.0).astype(BF16)

    def prefix_body(c, carry):
        r0 = pl.multiple_of(N_META + c * LANES, 8)
        lf = _log_sigmoid(z_ref[pl.ds(r0, LANES), gf_off:gf_off + LANES] + bias_f)
        p = _tri_prefix(tri_b, lf) + carry
        fcol_ref[pl.ds(r0, LANES), :] = p
        return p[LANES - 1:LANES, :]

    total = lax.fori_loop(0, n_real // LANES, prefix_body, jnp.zeros((1, LANES), F32), unroll=4)

    fcol_ref[pl.ds(0, N_META), :] = f_meta
    acol_ref[pl.ds(0, N_META), :] = z_ref[pl.ds(0, N_META), gi_off:gi_off + LANES] + bias_i - f_meta

    def finish_body(c, carry):
        r0 = pl.multiple_of(N_META + c * LANES, 8)
        lf = _log_sigmoid(z_ref[pl.ds(r0, LANES), gf_off:gf_off + LANES] + bias_f)
        p = fcol_ref[pl.ds(r0, LANES), :]
        f = f_meta_end + jnp.where(lane == 0, p, total - p + lf)
        fcol_ref[pl.ds(r0, LANES), :] = f
        a = z_ref[pl.ds(r0, LANES), gi_off:gi_off + LANES] + bias_i - f
        acol_ref[pl.ds(r0, LANES), :] = a
        l0 = pl.multiple_of(c * LANES, LANES)
        atr_ref[:, pl.ds(l0, LANES)] = a.T[0:8, :]
        kt_ref[:, pl.ds(l0, LANES)] = z_ref[pl.ds(r0, LANES), HEAD_DIM:2 * HEAD_DIM].T
        vaug_ref[pl.ds(r0, LANES), :] = jnp.concatenate(
            [z_ref[pl.ds(r0, LANES), 2 * HEAD_DIM:3 * HEAD_DIM], ones_col], axis=1).astype(BF16)
        return carry

    ones_col = jnp.where(lax.broadcasted_iota(jnp.int32, (LANES, LANES), 1) == 0, 1.0, 0.0)
    lax.fori_loop(0, n_real // LANES, finish_body, 0, unroll=4)
    atm_ref[...] = acol_ref[pl.ds(0, LANES), :].T[0:8, :]
    kt_meta = z_ref[pl.ds(0, LANES), HEAD_DIM:2 * HEAD_DIM].T[:, 0:N_META]
    vaug_meta = jnp.concatenate([z_ref[pl.ds(0, N_META), 2 * HEAD_DIM:3 * HEAD_DIM],
                                 jnp.where(lax.broadcasted_iota(jnp.int32, (N_META, LANES), 1) == 0, 1.0, 0.0)],
                                axis=1).astype(BF16)

    mask_m = _tri_mask(N_META, True)
    states = []
    for d in (0, 1):
        state = (jnp.zeros((HEAD_DIM, 2 * HEAD_DIM), F32), jnp.zeros((1, 1), F32))
        out_m, state = _mlstm_chunk(
            z_ref[pl.ds(0, N_META), 0:HEAD_DIM], kt_meta, vaug_meta,
            atm_ref[pl.ds(d, 1), 0:N_META], fcol_ref[pl.ds(0, N_META), d:d + 1], mask_m, state)
        acc_ref[d, pl.ds(0, N_META), :] = out_m
        states.append(state)

    def body(i, carry):
        new = []
        for d in (0, 1):
            c = i if d == 0 else n_chunks - 1 - i
            r0 = pl.multiple_of(N_META + c * t, 8)
            l0 = pl.multiple_of(c * t, LANES)
            out, st = _mlstm_chunk(
                z_ref[pl.ds(r0, t), 0:HEAD_DIM], kt_ref[:, pl.ds(l0, t)], vaug_ref[pl.ds(r0, t), :],
                atr_ref[pl.ds(d, 1), pl.ds(l0, t)], fcol_ref[pl.ds(r0, t), d:d + 1],
                _tri_mask(t, d == 0), carry[d])
            acc_ref[d, pl.ds(r0, t), :] = out
            new.append(st)
        return tuple(new)

    lax.fori_loop(0, n_chunks, body, tuple(states))

    for r0, nr in _row_blocks(n_rows):
        hs = acc_ref[0, pl.ds(r0, nr), :] + acc_ref[1, pl.ds(r0, nr), :]
        o = z_ref[pl.ds(r0, nr), 3 * HEAD_DIM:4 * HEAD_DIM]
        out_ref[0, pl.ds(r0, nr), :] = (_sigmoid(o) * _rms(hs, ng_ref[0])).astype(BF16)


def _mlstm_call(h, gn, w, conv, gb, ng):
    bsz, n_rows, d = h.shape
    nw = w.shape[-1]
    n_real = n_rows - N_META
    return pl.pallas_call(
        _mlstm_kernel,
        grid=(bsz, N_HEADS),
        in_specs=[
            pl.BlockSpec((1, n_rows, d), lambda b, hd: (b, 0, 0)),
            pl.BlockSpec((1, d), lambda b, hd: (0, 0)),
            pl.BlockSpec((1, d, nw), lambda b, hd: (hd, 0, 0)),
            pl.BlockSpec((1, CONV_K, 2 * HEAD_DIM), lambda b, hd: (hd, 0, 0)),
            pl.BlockSpec((1, 1, 2 * LANES), lambda b, hd: (hd, 0, 0)),
            pl.BlockSpec((1, 1, HEAD_DIM), lambda b, hd: (hd, 0, 0)),
        ],
        out_specs=pl.BlockSpec((1, n_rows, HEAD_DIM), lambda b, hd: (b, 0, hd)),
        out_shape=jax.ShapeDtypeStruct((bsz, n_rows, N_HEADS * HEAD_DIM), BF16),
        scratch_shapes=[
            pltpu.VMEM((n_rows, d), BF16),
            pltpu.VMEM((n_rows, nw), F32),
            pltpu.VMEM((n_rows + 2 * CONV_PAD, 2 * HEAD_DIM), F32),
            pltpu.VMEM((2, n_rows, HEAD_DIM), F32),
            pltpu.VMEM((n_rows, LANES), F32),
            pltpu.VMEM((n_rows, LANES), F32),
            pltpu.VMEM((8, LANES), F32),
            pltpu.VMEM((8, n_real), F32),
            pltpu.VMEM((HEAD_DIM, n_real), F32),
            pltpu.VMEM((n_rows, 2 * HEAD_DIM), BF16),
        ],
        compiler_params=_cparams(2),
        name="mlstm_heads",
    )(h, gn, w, conv, gb, ng)


def _ret_chunk(qc, kt, vc, dmat, dq, dk_row, dchunk, r_st):
    qb, vb = qc.astype(BF16), vc.astype(BF16)
    s = _dot(qb, kt.astype(BF16)) * dmat
    out = _dot(s.astype(BF16), vb) + dq * _dot(qb, r_st.astype(BF16))
    r_new = dchunk * r_st + _dot((kt * dk_row).astype(BF16), vb)
    return out, r_new


def _decay_tables(n, lg, forward):
    r = lax.broadcasted_iota(jnp.int32, (n, n), 0)
    c = lax.broadcasted_iota(jnp.int32, (n, n), 1)
    pos_q = lax.broadcasted_iota(jnp.int32, (n, 1), 0).astype(F32)
    pos_k = lax.broadcasted_iota(jnp.int32, (1, n), 1).astype(F32)
    if forward:
        dist, mask = (r - c).astype(F32), c <= r
        dq, dk = jnp.exp((pos_q + 1.0) * lg), jnp.exp((n - 1.0 - pos_k) * lg)
    else:
        dist, mask = (c - r).astype(F32), c >= r
        dq, dk = jnp.exp((n - pos_q) * lg), jnp.exp(pos_k * lg)
    dmat = jnp.exp(jnp.where(mask, dist * lg, NEG))
    return dmat, dq, dk, jnp.exp(n * lg)


def _ret_kernel(h_ref, gn_ref, w_ref, cos_ref, sin_ref, dl_ref, ng_ref, out_ref,
                un_ref, z_ref, acc_ref, dmat_ref, dq_ref, kt_ref):
    n_rows = h_ref.shape[1]
    n_real = n_rows - N_META
    t = SCAN_T
    n_chunks = n_real // t

    @pl.when(pl.program_id(1) == 0)
    def _():
        _norm_to_scratch(h_ref, gn_ref, un_ref)

    _project(un_ref, w_ref, z_ref, 0, 2 * HEAD_DIM)
    _project(un_ref, w_ref, z_ref, 2 * HEAD_DIM)

    for r0, nr in _row_blocks(n_rows):
        cs, sn = cos_ref[pl.ds(r0, nr), :], sin_ref[pl.ds(r0, nr), :]
        q = z_ref[pl.ds(r0, nr), 0:HEAD_DIM]
        z_ref[pl.ds(r0, nr), 0:HEAD_DIM] = q * cs + pltpu.roll(q, HEAD_DIM // 2, 1) * sn
        k = z_ref[pl.ds(r0, nr), HEAD_DIM:2 * HEAD_DIM]
        z_ref[pl.ds(r0, nr), HEAD_DIM:2 * HEAD_DIM] = (
            (k * cs + pltpu.roll(k, HEAD_DIM // 2, 1) * sn) * HEAD_DIM ** -0.5)

    for c in range(n_real // LANES):
        kt_ref[:, c * LANES:(c + 1) * LANES] = z_ref[pl.ds(N_META + c * LANES, LANES), HEAD_DIM:2 * HEAD_DIM].T
    kt_meta = z_ref[pl.ds(0, LANES), HEAD_DIM:2 * HEAD_DIM].T[:, 0:N_META]

    lg_all = _log_sigmoid(dl_ref[0])
    states, dk_rows, dchunks = [], [], []
    for d in (0, 1):
        lg = lg_all[:, d:d + 1]
        dmat_m, _, dk_m, dch_m = _decay_tables(N_META, lg, True)
        dmat, dq, dk, dch = _decay_tables(t, lg, d == 0)
        dmat_ref[d] = dmat
        dq_ref[d] = jnp.broadcast_to(dq, (t, LANES))
        dk_rows.append(dk)
        dchunks.append(dch)
        out_m, r_st = _ret_chunk(
            z_ref[pl.ds(0, N_META), 0:HEAD_DIM], kt_meta, z_ref[pl.ds(0, N_META), 2 * HEAD_DIM:3 * HEAD_DIM],
            dmat_m, jnp.zeros((N_META, 1), F32), dk_m, dch_m, jnp.zeros((HEAD_DIM, HEAD_DIM), F32))
        acc_ref[d, pl.ds(0, N_META), :] = out_m
        states.append(r_st)

    def body(i, carry):
        new = []
        for d in (0, 1):
            c = i if d == 0 else n_chunks - 1 - i
            r0 = pl.multiple_of(N_META + c * t, 8)
            l0 = pl.multiple_of(c * t, LANES)
            out, st = _ret_chunk(
                z_ref[pl.ds(r0, t), 0:HEAD_DIM], kt_ref[:, pl.ds(l0, t)],
                z_ref[pl.ds(r0, t), 2 * HEAD_DIM:3 * HEAD_DIM],
                dmat_ref[d], dq_ref[d], dk_rows[d], dchunks[d], carry[d])
            acc_ref[d, pl.ds(r0, t), :] = out
            new.append(st)
        return tuple(new)

    lax.fori_loop(0, n_chunks, body, tuple(states))

    for r0, nr in _row_blocks(n_rows):
        hs = acc_ref[0, pl.ds(r0, nr), :] + acc_ref[1, pl.ds(r0, nr), :]
        g = z_ref[pl.ds(r0, nr), 3 * HEAD_DIM:4 * HEAD_DIM]
        out_ref[0, pl.ds(r0, nr), :] = (g * _sigmoid(g) * _rms(hs, ng_ref[0])).astype(BF16)


def _ret_call(h, gn, w, cos2, sin2, dl, ng):
    bsz, n_rows, d = h.shape
    nw = w.shape[-1]
    return pl.pallas_call(
        _ret_kernel,
        grid=(bsz, N_HEADS),
        in_specs=[
            pl.BlockSpec((1, n_rows, d), lambda b, hd: (b, 0, 0)),
            pl.BlockSpec((1, d), lambda b, hd: (0, 0)),
            pl.BlockSpec((1, d, nw), lambda b, hd: (hd, 0, 0)),
            pl.BlockSpec((n_rows, HEAD_DIM), lambda b, hd: (0, 0)),
            pl.BlockSpec((n_rows, HEAD_DIM), lambda b, hd: (0, 0)),
            pl.BlockSpec((1, 1, LANES), lambda b, hd: (hd, 0, 0)),
            pl.BlockSpec((1, 1, HEAD_DIM), lambda b, hd: (hd, 0, 0)),
        ],
        out_specs=pl.BlockSpec((1, n_rows, HEAD_DIM), lambda b, hd: (b, 0, hd)),
        out_shape=jax.ShapeDtypeStruct((bsz, n_rows, N_HEADS * HEAD_DIM), BF16),
        scratch_shapes=[
            pltpu.VMEM((n_rows, d), BF16),
            pltpu.VMEM((n_rows, nw), F32),
            pltpu.VMEM((2, n_rows, HEAD_DIM), F32),
            pltpu.VMEM((2, SCAN_T, SCAN_T), F32),
            pltpu.VMEM((2, SCAN_T, LANES), F32),
            pltpu.VMEM((HEAD_DIM, n_rows - N_META), F32),
        ],
        compiler_params=_cparams(2),
        name="retention_heads",
    )(h, gn, w, cos2, sin2, dl, ng)


def _head_pair_block_diag(qp):
    lane = lax.broadcasted_iota(jnp.int32, qp.shape, 1)
    zero = jnp.zeros_like(qp)
    return jnp.concatenate([jnp.where(lane < NA_HEAD_DIM, qp, zero),
                            jnp.where(lane >= NA_HEAD_DIM, qp, zero)], axis=0)


def _na_attend(qp, keys, values, bias_t):
    n = qp.shape[0]
    bd = _head_pair_block_diag(qp)
    scores = []
    for kb, bt in zip(keys, bias_t):
        s = _dot_nt(kb, bd)
        scores.append(s if bt is None else s + bt)
    m = scores[0].max(axis=0, keepdims=True)
    for s in scores[1:]:
        m = jnp.maximum(m, s.max(axis=0, keepdims=True))
    probs = [jnp.exp(s - m) for s in scores]
    den = probs[0].sum(axis=0, keepdims=True)
    for p in probs[1:]:
        den = den + p.sum(axis=0, keepdims=True)
    inv = 1.0 / den
    o2 = None
    for p, vb in zip(probs, values):
        part = _dot_tn((p * inv).astype(BF16), vb)
        o2 = part if o2 is None else o2 + part
    lane = lax.broadcasted_iota(jnp.int32, (n, 2 * NA_HEAD_DIM), 1)
    return jnp.where(lane < NA_HEAD_DIM, o2[0:n, :], o2[n:2 * n, :])


def _na_kernel(h_ref, gn_ref, w_ref, qkg_ref, bias_ref, out_ref,
               un_ref, z_ref, q_ref, k_ref, vt_ref, ot_ref, s_ref, p_ref, den_ref):
    n_rows = h_ref.shape[1]
    n_real = n_rows - N_META
    n_grid_rows = n_real // GRID_W
    dh = NA_HEAD_DIM
    pw = 2 * dh
    band = NA_WIN_ROWS * GRID_W
    pairs = range(NA_PAIRS)
    v_off = NA_PAIRS * 2 * pw

    @pl.when(pl.program_id(1) == 0)
    def _():
        _norm_to_scratch(h_ref, gn_ref, un_ref)

    _project(un_ref, w_ref, z_ref, 0, v_off)
    _project(un_ref, w_ref, z_ref, v_off)

    r = lax.broadcasted_iota(jnp.int32, (2 * pw, 2 * pw), 0)
    c = lax.broadcasted_iota(jnp.int32, (2 * pw, 2 * pw), 1)
    head_ones = jnp.where(r // dh == c // dh, 1.0, 0.0).astype(BF16)
    for p in pairs:
        for r0, nr in _row_blocks(n_rows):
            x = z_ref[pl.ds(r0, nr), p * 2 * pw:(p + 1) * 2 * pw]
            sq = x * x
            hi = sq.astype(BF16)
            lo = (sq - hi.astype(F32)).astype(BF16)
            ssq = _dot(hi, head_ones) + _dot(lo, head_ones)
            y = (x * lax.rsqrt(ssq * (1.0 / dh) + EPS) * qkg_ref[...]).astype(BF16)
            q_ref[p, pl.ds(r0, nr), :] = y[:, 0:pw]
            k_ref[p, pl.ds(r0, nr), :] = y[:, pw:2 * pw]

    n_tblocks = n_real // LANES
    k_meta, vt_meta = [], []
    for p in pairs:
        v_cols = slice(v_off + p * pw, v_off + (p + 1) * pw)
        for cpy in (0, 1):
            for c in range(n_tblocks - cpy):
                vb = z_ref[pl.ds(N_META + cpy * GRID_W + c * LANES, LANES), v_cols]
                vt_ref[p, cpy, :, c * LANES:(c + 1) * LANES] = vb.T.astype(BF16)
        vt_ref[p, 1, :, (n_tblocks - 1) * LANES:n_tblocks * LANES] = jnp.zeros((pw, LANES), BF16)
        vt_meta.append(z_ref[pl.ds(0, LANES), v_cols].T[:, 0:N_META].astype(BF16))
        k_meta.append(k_ref[p, pl.ds(0, N_META), :])
        out_ref[0, pl.ds(0, N_META), p * pw:(p + 1) * pw] = _na_attend(
            q_ref[p, pl.ds(0, N_META), :], [k_meta[p]], [z_ref[pl.ds(0, N_META), v_cols].astype(BF16)],
            [None]).astype(BF16)

    sub = lax.broadcasted_iota(jnp.int32, (pw, pw), 0)
    lane = lax.broadcasted_iota(jnp.int32, (pw, pw), 1)

    def row_start(r):
        return jnp.clip(r - NA_WIN_ROWS // 2, 0, n_grid_rows - NA_WIN_ROWS)

    def stage_scores(r, slot):
        rs = row_start(r)
        q0 = pl.multiple_of(N_META + r * GRID_W, 8)
        k0 = pl.multiple_of(N_META + rs * GRID_W, 8)
        for p in pairs:
            bd = _head_pair_block_diag(q_ref[p, pl.ds(q0, GRID_W), :])
            bias = bias_ref[p, pl.ds(NA_WIN_ROWS - 1 - (r - rs), NA_WIN_ROWS)].reshape(band, pw)
            s_ref[p, slot, pl.ds(0, band), :] = _dot_nt(k_ref[p, pl.ds(k0, band), :], bd) + bias
            s_ref[p, slot, pl.ds(band, N_META), :] = _dot_nt(k_meta[p], bd)

    def stage_softmax(slot):
        for p in pairs:
            s = s_ref[p, slot]
            e = jnp.exp(s - s.max(axis=0, keepdims=True))
            den_ref[p, slot] = e.sum(axis=0, keepdims=True)
            p_ref[p, slot] = e.astype(BF16)

    def stage_values(r, slot):
        rs = row_start(r)
        par = rs % 2
        l0 = pl.multiple_of((rs - par) * GRID_W, LANES)
        for p in pairs:
            o_t = (_dot(vt_ref[p, par, :, pl.ds(l0, band)], p_ref[p, slot, pl.ds(0, band), :])
                   + _dot(vt_meta[p], p_ref[p, slot, pl.ds(band, N_META), :]))
            ot_ref[p, r] = o_t / den_ref[p, slot]

    stage_scores(0, 0)
    stage_softmax(0)
    stage_scores(1, 1)

    def pipe_body(j, carry):
        i0 = 2 + 2 * j
        stage_values(i0 - 2, 0)
        stage_softmax(1)
        stage_scores(i0, 0)
        stage_values(i0 - 1, 1)
        stage_softmax(0)
        stage_scores(i0 + 1, 1)
        return carry

    lax.fori_loop(0, (n_grid_rows - 2) // 2, pipe_body, 0)
    stage_values(n_grid_rows - 2, 0)
    stage_softmax(1)
    stage_values(n_grid_rows - 1, 1)

    for p in pairs:
        for c in range(n_tblocks):
            oa, ob = ot_ref[p, 2 * c], ot_ref[p, 2 * c + 1]
            sel_a = jnp.where(sub < dh, oa, pltpu.roll(oa, dh, 1))
            sel_b = jnp.where(sub < dh, pltpu.roll(ob, dh, 1), ob)
            out_ref[0, pl.ds(N_META + c * LANES, LANES), p * pw:(p + 1) * pw] = (
                jnp.where(lane < dh, sel_a, sel_b).T.astype(BF16))


def _na_call(h, gn, w, qg, kg, bias):
    bsz, n_rows, d = h.shape
    n_steps, _, nw = w.shape
    pw = 2 * NA_HEAD_DIM
    band = NA_WIN_ROWS * GRID_W
    n_real = n_rows - N_META
    qkg = jnp.concatenate([jnp.tile(qg, 2) * NA_HEAD_DIM ** -0.5, jnp.tile(kg, 2)])[None, :].astype(F32)
    return pl.pallas_call(
        _na_kernel,
        grid=(bsz, n_steps),
        in_specs=[
            pl.BlockSpec((1, n_rows, d), lambda b, s: (b, 0, 0)),
            pl.BlockSpec((1, d), lambda b, s: (0, 0)),
            pl.BlockSpec((1, d, nw), lambda b, s: (s, 0, 0)),
            pl.BlockSpec((1, 2 * pw), lambda b, s: (0, 0)),
            pl.BlockSpec((NA_PAIRS,) + bias.shape[1:], lambda b, s: (s, 0, 0, 0)),
        ],
        out_specs=pl.BlockSpec((1, n_rows, NA_PAIRS * pw), lambda b, s: (b, 0, s)),
        out_shape=jax.ShapeDtypeStruct((bsz, n_rows, n_steps * NA_PAIRS * pw), BF16),
        scratch_shapes=[
            pltpu.VMEM((n_rows, d), BF16),
            pltpu.VMEM((n_rows, nw), F32),
            pltpu.VMEM((NA_PAIRS, n_rows, pw), BF16),
            pltpu.VMEM((NA_PAIRS, n_rows, pw), BF16),
            pltpu.VMEM((NA_PAIRS, 2, pw, n_real), BF16),
            pltpu.VMEM((NA_PAIRS, n_real // GRID_W, pw, pw), F32),
            pltpu.VMEM((NA_PAIRS, 2, band + N_META, pw), F32),
            pltpu.VMEM((NA_PAIRS, 2, band + N_META, pw), BF16),
            pltpu.VMEM((NA_PAIRS, 2, 1, pw), F32),
        ],
        compiler_params=_cparams(2),
        name="neighbourhood_attention",
    )(h, gn, w, qkg, bias)


def _lane_prefix_exclusive(x, tri_strict):
    n_blocks = x.shape[1] // LANES
    carry = jnp.zeros((x.shape[0], 1), F32)
    pieces = []
    for j in range(n_blocks):
        blk = x[:, j * LANES:(j + 1) * LANES]
        pieces.append(_dot(blk.astype(BF16), tri_strict) + carry)
        carry = carry + jnp.sum(blk, axis=-1, keepdims=True)
    return jnp.concatenate(pieces, axis=1)


def _aligned_row_blocks(n_rows, step=512):
    blocks, r0 = [], 0
    while r0 < n_rows:
        nr = min(step, n_rows - r0)
        blocks.append((r0, nr))
        r0 += nr
    return blocks


def _post_kernel(n_mix, cap, *refs):
    h_ref = refs[0]
    mix_refs = refs[1:1 + n_mix]
    wo_ref, fg_ref, wr_ref = refs[1 + n_mix:4 + n_mix]
    hn_ref, u2_ref, slot_ref, gate_ref = refs[4 + n_mix:8 + n_mix]
    lgc_ref = refs[8 + n_mix]
    n_rows = h_ref.shape[1]
    n_pad = u2_ref.shape[1]

    for r0, nr in _row_blocks(n_rows):
        acc = h_ref[0, pl.ds(r0, nr), :]
        k0 = 0
        for m_ref in mix_refs:
            kw = m_ref.shape[2]
            acc = acc + _dot(m_ref[0, pl.ds(r0, nr), :], wo_ref[pl.ds(k0, kw), :])
            k0 += kw
        hn_ref[0, pl.ds(r0, nr), :] = acc
        u = _rms(acc, fg_ref[...])
        u_hi = u.astype(BF16)
        u2_ref[0, pl.ds(r0, nr), :] = u_hi
        u_lo = (u - u_hi.astype(F32)).astype(BF16)
        hh_hl = _dot(u_hi, wr_ref[...])
        lgc_ref[pl.ds(r0, nr), :] = (hh_hl[:, 0:LANES] + hh_hl[:, LANES:2 * LANES]
                                     + _dot(u_lo, wr_ref[:, 0:LANES]))
    u2_ref[0, pl.ds(n_rows, n_pad - n_rows), :] = jnp.zeros((n_pad - n_rows, u2_ref.shape[2]), BF16)
    lgc_ref[pl.ds(n_rows, n_pad - n_rows), :] = jnp.zeros((n_pad - n_rows, LANES), F32)

    logits = jnp.concatenate(
        [lgc_ref[pl.ds(c * LANES, LANES), :].T[0:N_EXPERTS, :] for c in range(n_pad // LANES)], axis=1)
    ex = jnp.exp(logits - jnp.max(logits, axis=0, keepdims=True))
    aff = ex / jnp.sum(ex, axis=0, keepdims=True)
    tok = lax.broadcasted_iota(jnp.int32, aff.shape, 1)
    aff = jnp.where(tok < n_rows, aff, -1.0)

    def count_ge(x):
        return jnp.sum(jnp.where(aff >= x, 1.0, 0.0), axis=-1, keepdims=True)

    capf = float(cap)

    def refine(base, cands):
        best = base
        for cand in cands:
            best = jnp.where(count_ge(cand) >= capf, cand, best)
        return best

    tiny = jnp.full((aff.shape[0], 1), 2.0 ** -126, F32)
    ok0 = count_ge(tiny) >= capf
    p = tiny
    p = refine(p, [p * (2.0 ** (16 * k)) for k in range(1, 8)])
    p = refine(p, [p * (2.0 ** (2 * k)) for k in range(1, 8)])
    p = refine(p, [p * 2.0])
    m = p
    for i in range(1, 8):
        step = p * (2.0 ** (-3 * i))
        m = refine(m, [m + k * step for k in range(1, 8)])
    step = p * (2.0 ** -23)
    m = refine(m, [m + k * step for k in range(1, 4)])
    thr = jnp.where(ok0, m, 0.0)

    gt = aff > thr
    eq = aff == thr
    need = capf - jnp.sum(jnp.where(gt, 1.0, 0.0), axis=-1, keepdims=True)
    r = lax.broadcasted_iota(jnp.int32, (LANES, LANES), 0)
    c = lax.broadcasted_iota(jnp.int32, (LANES, LANES), 1)
    tri_strict = jnp.where(r < c, 1.0, 0.0).astype(BF16)
    eq_rank = _lane_prefix_exclusive(jnp.where(eq, 1.0, 0.0), tri_strict)
    sel = gt | (eq & (eq_rank < need))
    pos = _lane_prefix_exclusive(jnp.where(sel, 1.0, 0.0), tri_strict)
    slot_ref[0] = jnp.where(sel, pos, -1.0).astype(jnp.int32)
    gate_ref[0] = jnp.where(sel, aff, 0.0)


def _post_call(h, mixes, wo, fg, wr, cap):
    bsz, n_rows, d = h.shape
    n_pad = ((n_rows + LANES - 1) // LANES) * LANES
    n_mix = len(mixes)
    row_spec = lambda w: pl.BlockSpec((1, n_rows, w), lambda b: (b, 0, 0), pipeline_mode=pl.Buffered(1))
    return pl.pallas_call(
        functools.partial(_post_kernel, n_mix, cap),
        grid=(bsz,),
        in_specs=[row_spec(d)] + [row_spec(m.shape[2]) for m in mixes] + [
            pl.BlockSpec((d, d), lambda b: (0, 0)),
            pl.BlockSpec((1, d), lambda b: (0, 0)),
            pl.BlockSpec((d, 2 * LANES), lambda b: (0, 0)),
        ],
        out_specs=[
            row_spec(d),
            pl.BlockSpec((1, n_pad, d), lambda b: (b, 0, 0), pipeline_mode=pl.Buffered(1)),
            pl.BlockSpec((1, N_EXPERTS, n_pad), lambda b: (b, 0, 0)),
            pl.BlockSpec((1, N_EXPERTS, n_pad), lambda b: (b, 0, 0)),
        ],
        out_shape=[
            jax.ShapeDtypeStruct((bsz, n_rows, d), F32),
            jax.ShapeDtypeStruct((bsz, n_pad, d), BF16),
            jax.ShapeDtypeStruct((bsz, N_EXPERTS, n_pad), jnp.int32),
            jax.ShapeDtypeStruct((bsz, N_EXPERTS, n_pad), F32),
        ],
        scratch_shapes=[pltpu.VMEM((n_pad, LANES), F32)],
        compiler_params=_cparams(1),
        name="outproj_router",
    )(h, *mixes, wo, fg, wr)


def _gather_kernel(u2_ref, slot_ref, xs_ref):
    cap_pad = xs_ref.shape[2]
    n_pad = u2_ref.shape[1]
    srow_id = lax.broadcasted_iota(jnp.int32, (cap_pad, n_pad), 0)

    def body(i, carry):
        pieces = [jnp.where(srow_id == slot_ref[0, pl.ds(i * GATHER_GROUP + j, 1), :], 1.0, 0.0).astype(BF16)
                  for j in range(GATHER_GROUP)]
        rows = _dot(jnp.concatenate(pieces, axis=0), u2_ref[0]).astype(BF16)
        for j in range(GATHER_GROUP):
            xs_ref[0, i * GATHER_GROUP + j] = rows[j * cap_pad:(j + 1) * cap_pad, :]
        return carry

    lax.fori_loop(0, N_EXPERTS // GATHER_GROUP, body, 0)


def _gather_call(u2, slot, cap_pad):
    bsz, n_pad, d = u2.shape
    return pl.pallas_call(
        _gather_kernel,
        grid=(bsz,),
        in_specs=[
            pl.BlockSpec((1, n_pad, d), lambda b: (b, 0, 0)),
            pl.BlockSpec((1, N_EXPERTS, n_pad), lambda b: (b, 0, 0)),
        ],
        out_specs=pl.BlockSpec((1, N_EXPERTS, cap_pad, d), lambda b: (b, 0, 0, 0)),
        out_shape=jax.ShapeDtypeStruct((bsz, N_EXPERTS, cap_pad, d), BF16),
        compiler_params=_cparams(1),
        name="expert_gather",
    )(u2, slot)


def _expert_kernel(xs_ref, wg_ref, wu_ref, wd_ref, ys_ref, wgb_ref, wub_ref, wdb_ref):
    bb, _, cap_pad, d = xs_ref.shape

    @pl.when(pl.program_id(1) == 0)
    def _():
        wgb_ref[...] = wg_ref[0].astype(BF16)
        wub_ref[...] = wu_ref[0].astype(BF16)
        wdb_ref[...] = wd_ref[0].astype(BF16)

    x = xs_ref[...].reshape(bb * cap_pad, d)
    g = _dot(x, wgb_ref[...])
    u = _dot(x, wub_ref[...])
    hdn = (g * _sigmoid(g) * u).astype(BF16)
    ys_ref[...] = _dot(hdn, wdb_ref[...]).astype(BF16).reshape(ys_ref.shape)


def _expert_call(xs, wg, wu, wd, layer):
    bsz, n_e, cap_pad, d = xs.shape
    ff = wg.shape[-1]
    seq_blocks = 2 if bsz % 2 == 0 else 1
    bb = bsz // seq_blocks
    return pl.pallas_call(
        _expert_kernel,
        grid=(n_e, seq_blocks),
        in_specs=[
            pl.BlockSpec((bb, 1, cap_pad, d), lambda e, m: (m, e, 0, 0)),
            pl.BlockSpec((1, None, d, ff), lambda e, m: (layer, e, 0, 0)),
            pl.BlockSpec((1, None, d, ff), lambda e, m: (layer, e, 0, 0)),
            pl.BlockSpec((1, None, ff, d), lambda e, m: (layer, e, 0, 0)),
        ],
        out_specs=pl.BlockSpec((bb, 1, cap_pad, d), lambda e, m: (m, e, 0, 0)),
        out_shape=jax.ShapeDtypeStruct((bsz, n_e, cap_pad, d), BF16),
        scratch_shapes=[pltpu.VMEM((d, ff), BF16), pltpu.VMEM((d, ff), BF16), pltpu.VMEM((ff, d), BF16)],
        compiler_params=_cparams(2),
        name="expert_swiglu",
    )(xs, wg, wu, wd)


def _combine_kernel(h_ref, ym_ref, yt_ref, slot_ref, gate_ref, out_ref):
    n_rows = h_ref.shape[1]
    n_out = out_ref.shape[1]
    skip = n_rows - n_out
    _, e_grp, n_main, d = ym_ref.shape
    _, n_e, n_tail, _ = yt_ref.shape
    n_pad = slot_ref.shape[2]
    g = pl.program_id(1)

    @pl.when(g == 0)
    def _():
        out_ref[0] = h_ref[0, pl.ds(skip, n_out), :]

    def weights_for(e, first_slot, n_slots):
        srow_id = lax.broadcasted_iota(jnp.int32, (n_slots, n_pad), 0) + first_slot
        hit = srow_id == slot_ref[0, pl.ds(e, 1), :]
        return jnp.where(hit, gate_ref[0, pl.ds(e, 1), :], 0.0).astype(BF16)

    def accumulate(weights, ys):
        for t0, tn in _aligned_row_blocks(n_pad, 1024):
            lo, hi = max(t0, skip), min(t0 + tn, n_rows)
            if hi <= lo:
                continue
            part = _dot_tn(weights[:, t0:t0 + tn], ys)
            out_ref[0, pl.ds(lo - skip, hi - lo), :] += part[lo - t0:hi - t0, :]

    @pl.when(g == 0)
    def _():
        weights = jnp.concatenate([weights_for(e, n_main, n_tail) for e in range(n_e)], axis=0)
        accumulate(weights, yt_ref[0].reshape(n_e * n_tail, d))

    @pl.when(g > 0)
    def _():
        weights = jnp.concatenate([weights_for((g - 1) * e_grp + j, 0, n_main) for j in range(e_grp)], axis=0)
        accumulate(weights, ym_ref[0].reshape(e_grp * n_main, d))


def _combine_call(h, ys, slot, gate, n_out, e_grp=4):
    bsz, n_rows, d = h.shape
    _, n_e, cap_pad, _ = ys.shape
    n_pad = slot.shape[2]
    n_main_steps = n_e // e_grp
    n_tail = cap_pad - MXU_DEPTH
    return pl.pallas_call(
        _combine_kernel,
        grid=(bsz, n_main_steps + 1),
        in_specs=[
            pl.BlockSpec((1, n_rows, d), lambda b, g: (b, 0, 0)),
            pl.BlockSpec((1, e_grp, MXU_DEPTH, d), lambda b, g: (b, jnp.maximum(g - 1, 0), 0, 0)),
            pl.BlockSpec((1, n_e, n_tail, d), lambda b, g: (b, 0, MXU_DEPTH // n_tail, 0)),
            pl.BlockSpec((1, n_e, n_pad), lambda b, g: (b, 0, 0)),
            pl.BlockSpec((1, n_e, n_pad), lambda b, g: (b, 0, 0)),
        ],
        out_specs=pl.BlockSpec((1, n_out, d), lambda b, g: (b, 0, 0)),
        out_shape=jax.ShapeDtypeStruct((bsz, n_out, d), F32),
        compiler_params=_cparams(2),
        name="expert_combine",
    )(h, ys, ys, slot, gate)


def _even_weights(w_in, conv_w, gate_b, decay_logit):
    d = w_in.shape[0]
    mw = N_HEADS * HEAD_DIM
    mq, mk, mv, mo = (w_in[:, i * mw:(i + 1) * mw] for i in range(4))
    gates = w_in[:, 4 * mw:4 * mw + 4 * N_HEADS]
    r0 = 4 * mw + 4 * N_HEADS
    rq, rk, rv, rg = (w_in[:, r0 + i * mw:r0 + (i + 1) * mw] for i in range(4))

    def per_head(t):
        return t.reshape(d, N_HEADS, HEAD_DIM).transpose(1, 0, 2)

    def gate_cols(fw_off, bw_off):
        cols = jnp.stack([gates[:, fw_off:fw_off + N_HEADS], gates[:, bw_off:bw_off + N_HEADS]], axis=-1)
        cols = cols.transpose(1, 0, 2)
        return jnp.pad(cols, ((0, 0), (0, 0), (0, LANES - 2)))

    def gate_bias(fw_off, bw_off):
        b = jnp.stack([gate_b[fw_off:fw_off + N_HEADS], gate_b[bw_off:bw_off + N_HEADS]], axis=-1)
        return jnp.pad(b, ((0, 0), (0, LANES - 2)))

    w_m = jnp.concatenate([per_head(mq), per_head(mk), per_head(mv), per_head(mo),
                           gate_cols(0, 2 * N_HEADS), gate_cols(N_HEADS, 3 * N_HEADS)], axis=-1).astype(BF16)
    gb = jnp.concatenate([gate_bias(0, 2 * N_HEADS), gate_bias(N_HEADS, 3 * N_HEADS)], axis=-1)[:, None, :]
    conv = jnp.concatenate([conv_w[:, :mw].reshape(CONV_K, N_HEADS, HEAD_DIM),
                            conv_w[:, mw:].reshape(CONV_K, N_HEADS, HEAD_DIM)], axis=-1).transpose(1, 0, 2)
    w_r = jnp.concatenate([per_head(rq), per_head(rk), per_head(rv), per_head(rg)], axis=-1).astype(BF16)
    dl = jnp.pad(decay_logit.T, ((0, 0), (0, LANES - 2)))[:, None, :]
    return w_m, gb.astype(F32), conv.astype(F32), w_r, dl.astype(F32)


def _rotary_tables(n_rows):
    half = HEAD_DIM // 2
    inv = ROPE_BASE ** (-jnp.arange(half, dtype=F32) / half)
    ang = jnp.arange(n_rows, dtype=F32)[:, None] * inv[None, :]
    cos, sin = jnp.cos(ang), jnp.sin(ang)
    return jnp.concatenate([cos, cos], axis=-1), jnp.concatenate([-sin, sin], axis=-1)


def _na_weights(w_in):
    d = w_in.shape[0]
    n_heads = w_in.shape[1] // (3 * NA_HEAD_DIM)
    pw = 2 * NA_HEAD_DIM
    n_steps = n_heads // (2 * NA_PAIRS)
    z = w_in.reshape(d, 3, n_steps, NA_PAIRS, pw)
    qk = z[:, 0:2].transpose(2, 0, 3, 1, 4).reshape(n_steps, d, NA_PAIRS * 2 * pw)
    v = z[:, 2].transpose(1, 0, 2, 3).reshape(n_steps, d, NA_PAIRS * pw)
    return jnp.concatenate([qk, v], axis=-1).astype(BF16)


def _na_bias_table(rpb):
    col = np.arange(GRID_W)
    col_start = np.clip(col - NA_WIN_COLS // 2, 0, GRID_W - NA_WIN_COLS)
    col_in = (col[None, :] >= col_start[:, None]) & (col[None, :] < col_start[:, None] + NA_WIN_COLS)
    dc_idx = np.clip(col[None, :] - col[:, None], -(NA_WIN_COLS - 1), NA_WIN_COLS - 1) + NA_WIN_COLS - 1
    selector = (np.arange(rpb.shape[2])[:, None, None] == dc_idx[None]).astype(np.float32)
    rpb_cols = jnp.einsum('hrd,dqk->hrqk', rpb.astype(F32), selector,
                          precision=lax.Precision.HIGHEST)
    tbl = jnp.where(col_in[None, None], rpb_cols, NEG)
    n_pairs, n_dr = rpb.shape[0] // 2, rpb.shape[1]
    tbl = tbl.reshape(n_pairs, 2, n_dr, GRID_W, GRID_W).transpose(0, 2, 4, 1, 3)
    return tbl.reshape(n_pairs, n_dr, GRID_W, 2 * GRID_W)


def _ffn(h, mixes, wo, fg, wr, wg, wu, wd, layer, n_out):
    n_rows = h.shape[1]
    cap = CAP_FACTOR * n_rows // N_EXPERTS
    cap_pad = ((cap + BF16_ROWS - 1) // BF16_ROWS) * BF16_ROWS
    wr_hi = wr.astype(BF16)
    wr_lo = (wr - wr_hi.astype(F32)).astype(BF16)
    pad = ((0, 0), (0, LANES - N_EXPERTS))
    wr_pieces = jnp.concatenate([jnp.pad(wr_hi, pad), jnp.pad(wr_lo, pad)], axis=1)
    hn, u2, slot, gate = _post_call(h, mixes, wo.astype(BF16), fg[None, :], wr_pieces, cap)
    xs = _gather_call(u2, slot, cap_pad)
    ys = _expert_call(xs, wg, wu, wd, layer)
    return _combine_call(hn, ys, slot, gate, n_out)


def kernel(x, meta_tokens, attn_norm_g, ffn_norm_g, even_w_in, even_conv_w, even_gate_b, even_m_norm_g, even_ret_decay_logit, even_r_norm_g, even_w_out, odd_w_in, odd_q_norm_g, odd_k_norm_g, odd_rpb, odd_w_out, router_w, expert_w_gate, expert_w_up, expert_w_down):
    bsz = x.shape[0]
    depth = attn_norm_g.shape[0]
    meta = jnp.broadcast_to(meta_tokens.astype(x.dtype)[None], (bsz,) + meta_tokens.shape)
    h = jnp.concatenate([meta, x], axis=1)
    n_rows = h.shape[1]
    cos2, sin2 = _rotary_tables(n_rows)
    for layer in range(depth):
        j = layer // 2
        gn = attn_norm_g[layer][None, :]
        if layer % 2 == 0:
            w_m, gb, conv, w_r, dl = _even_weights(even_w_in[j], even_conv_w[j], even_gate_b[j],
                                                   even_ret_decay_logit[j])
            m_out = _mlstm_call(h, gn, w_m, conv, gb, even_m_norm_g[j].reshape(N_HEADS, 1, HEAD_DIM))
            r_out = _ret_call(h, gn, w_r, cos2, sin2, dl, even_r_norm_g[j].reshape(N_HEADS, 1, HEAD_DIM))
            mixes, wo = [m_out, r_out], even_w_out[j]
        else:
            a_out = _na_call(h, gn, _na_weights(odd_w_in[j]), odd_q_norm_g[j], odd_k_norm_g[j],
                             _na_bias_table(odd_rpb[j]))
            mixes, wo = [a_out], odd_w_out[j]
        n_out = n_rows - N_META if layer == depth - 1 else n_rows
        h = _ffn(h, mixes, wo, ffn_norm_g[layer], router_w[layer],
                 expert_w_gate, expert_w_up, expert_w_down, layer, n_out)
    return h
```

```python
import functools

import jax
import jax.numpy as jnp
import numpy as np
from jax import lax
from jax.experimental import pallas as pl
from jax.experimental.pallas import tpu as pltpu

F32 = jnp.float32
BF16 = jnp.bfloat16

LANES = 128
BF16_ROWS = 16
MXU_DEPTH = 256
N_META = 16
GRID_W = 64
EPS = 1e-6
HEAD_DIM = 128
N_HEADS = 4
CONV_K = 5
CONV_PAD = 8
ROPE_BASE = 10000.0
NA_HEAD_DIM = 64
NA_WIN_ROWS = 8
NA_WIN_COLS = 16
NA_PAIRS = 2
N_EXPERTS = 16
CAP_FACTOR = 2
GATHER_GROUP = 2
NEG = -1e30
SCAN_T = 256
VMEM_LIMIT = 56 * 1024 * 1024


def _cparams(n_axes):
    return pltpu.CompilerParams(
        dimension_semantics=("arbitrary",) * n_axes, vmem_limit_bytes=VMEM_LIMIT)


def _row_blocks(n_rows):
    for nb in (3, 2, 4, 6, 1):
        if n_rows % (8 * nb) == 0:
            step = n_rows // nb
            return [(i * step, step) for i in range(nb)]
    return [(0, n_rows)]


def _rms(x, g):
    return x * lax.rsqrt(jnp.mean(x * x, axis=-1, keepdims=True) + EPS) * g


def _sigmoid(x):
    return 1.0 / (1.0 + jnp.exp(-x))


def _log_sigmoid(x):
    return jnp.minimum(x, 0.0) - jnp.log(1.0 + jnp.exp(-jnp.abs(x)))


def _dot(a, b):
    return jnp.dot(a, b, preferred_element_type=F32)


def _dot_nt(a, b):
    return lax.dot_general(a, b, (((1,), (1,)), ((), ())), preferred_element_type=F32)


def _dot_tn(a, b):
    return lax.dot_general(a, b, (((0,), (0,)), ((), ())), preferred_element_type=F32)


def _split3(x):
    hi = x.astype(BF16)
    r1 = x - hi.astype(F32)
    mid = r1.astype(BF16)
    lo = (r1 - mid.astype(F32)).astype(BF16)
    return hi, mid, lo


def _tri_prefix(tri_bf16, x):
    hi, mid, lo = _split3(x)
    return _dot(tri_bf16, hi) + _dot(tri_bf16, mid) + _dot(tri_bf16, lo)


def _norm_to_scratch(h_ref, gn_ref, un_ref):
    n_rows = h_ref.shape[1]
    for r0, nr in _row_blocks(n_rows):
        x = h_ref[0, pl.ds(r0, nr), :]
        un_ref[pl.ds(r0, nr), :] = _rms(x, gn_ref[...]).astype(BF16)


def _project(un_ref, w_ref, z_ref, c0=0, c1=None):
    n_rows = un_ref.shape[0]
    c1 = z_ref.shape[1] if c1 is None else c1
    for r0, nr in _row_blocks(n_rows):
        z_ref[pl.ds(r0, nr), c0:c1] = _dot(un_ref[pl.ds(r0, nr), :], w_ref[0, :, c0:c1])


def _tri_mask(n, lower):
    r = lax.broadcasted_iota(jnp.int32, (n, n), 0)
    c = lax.broadcasted_iota(jnp.int32, (n, n), 1)
    return (c <= r) if lower else (c >= r)


def _mlstm_chunk(qc, kt, v_aug, a_row, f_col, mask, state):
    c_aug, g_prev = state
    qb = qc.astype(BF16)
    cm = jnp.max(jnp.where(mask, a_row, NEG), axis=-1, keepdims=True)
    g_col = jnp.maximum(g_prev, cm)
    dm = jnp.exp(jnp.where(mask, a_row - g_col, NEG))
    s = _dot(qb, kt.astype(BF16)) * dm
    w_inter = jnp.exp(g_prev - g_col)
    tot = w_inter * _dot(qb, c_aug.astype(BF16)) + _dot(s.astype(BF16), v_aug)
    den = tot[:, HEAD_DIM:2 * HEAD_DIM][:, 0:1]
    out = tot[:, 0:HEAD_DIM] / jnp.maximum(jnp.abs(den), jnp.exp(-(f_col + g_col)))
    g_end = jnp.maximum(g_prev, jnp.max(a_row, axis=-1, keepdims=True))
    ktw = (kt * jnp.exp(a_row - g_end)).astype(BF16)
    c_new = jnp.exp(g_prev - g_end) * c_aug + _dot(ktw, v_aug)
    return out, (c_new, g_end)


def _mlstm_kernel(h_ref, gn_ref, w_ref, conv_ref, gb_ref, ng_ref, out_ref,
                  un_ref, z_ref, zc_ref, acc_ref, acol_ref, fcol_ref, atm_ref, atr_ref, kt_ref, vaug_ref):
    n_rows = h_ref.shape[1]
    n_real = n_rows - N_META
    t = SCAN_T
    n_chunks = n_real // t

    @pl.when(pl.program_id(1) == 0)
    def _():
        _norm_to_scratch(h_ref, gn_ref, un_ref)

    _project(un_ref, w_ref, z_ref, 0, 2 * HEAD_DIM)
    _project(un_ref, w_ref, z_ref, 2 * HEAD_DIM)

    zc_ref[pl.ds(0, CONV_PAD), :] = jnp.zeros((CONV_PAD, 2 * HEAD_DIM), F32)
    zc_ref[pl.ds(CONV_PAD + n_rows, CONV_PAD), :] = jnp.zeros((CONV_PAD, 2 * HEAD_DIM), F32)
    for r0, nr in _row_blocks(n_rows):
        zc_ref[pl.ds(CONV_PAD + r0, nr), :] = z_ref[pl.ds(r0, nr), 0:2 * HEAD_DIM]
    lane2 = lax.broadcasted_iota(jnp.int32, (1, 2 * HEAD_DIM), 1)
    qk_scale = jnp.where(lane2 >= HEAD_DIM, HEAD_DIM ** -0.5, 1.0).astype(F32)
    for r0, nr in _row_blocks(n_rows):
        acc = jnp.zeros((nr, 2 * HEAD_DIM), F32)
        for j in range(CONV_K):
            off = CONV_PAD - (CONV_K - 1) // 2 + j + r0
            acc = acc + zc_ref[pl.ds(off, nr), :] * conv_ref[0, pl.ds(j, 1), :]
        z_ref[pl.ds(r0, nr), 0:2 * HEAD_DIM] = acc * _sigmoid(acc) * qk_scale

    gi_off, gf_off = 4 * HEAD_DIM, 5 * HEAD_DIM
    bias_i = gb_ref[0, :, 0:LANES]
    bias_f = gb_ref[0, :, LANES:2 * LANES]
    lane = lax.broadcasted_iota(jnp.int32, (1, LANES), 1)
    tri_m = jnp.where(_tri_mask(N_META, True), 1.0, 0.0).astype(BF16)
    lf_m = _log_sigmoid(z_ref[pl.ds(0, N_META), gf_off:gf_off + LANES] + bias_f)
    f_meta = _tri_prefix(tri_m, lf_m)
    f_meta_end = f_meta[N_META - 1:N_META, :]
    tri_b = jnp.where(_tri_mask(LANES, True), 1.0, 0.0).astype(BF16)

    def prefix_body(c, carry):
        r0 = pl.multiple_of(N_META + c * LANES, 8)
        lf = _log_sigmoid(z_ref[pl.ds(r0, LANES), gf_off:gf_off + LANES] + bias_f)
        p = _tri_prefix(tri_b, lf) + carry
        fcol_ref[pl.ds(r0, LANES), :] = p
        return p[LANES - 1:LANES, :]

    total = lax.fori_loop(0, n_real // LANES, prefix_body, jnp.zeros((1, LANES), F32), unroll=4)

    fcol_ref[pl.ds(0, N_META), :] = f_meta
    acol_ref[pl.ds(0, N_META), :] = z_ref[pl.ds(0, N_META), gi_off:gi_off + LANES] + bias_i - f_meta

    def finish_body(c, carry):
        r0 = pl.multiple_of(N_META + c * LANES, 8)
        lf = _log_sigmoid(z_ref[pl.ds(r0, LANES), gf_off:gf_off + LANES] + bias_f)
        p = fcol_ref[pl.ds(r0, LANES), :]
        f = f_meta_end + jnp.where(lane == 0, p, total - p + lf)
        fcol_ref[pl.ds(r0, LANES), :] = f
        a = z_ref[pl.ds(r0, LANES), gi_off:gi_off + LANES] + bias_i - f
        acol_ref[pl.ds(r0, LANES), :] = a
        l0 = pl.multiple_of(c * LANES, LANES)
        atr_ref[:, pl.ds(l0, LANES)] = a.T[0:8, :]
        kt_ref[:, pl.ds(l0, LANES)] = z_ref[pl.ds(r0, LANES), HEAD_DIM:2 * HEAD_DIM].T
        vaug_ref[pl.ds(r0, LANES), :] = jnp.concatenate(
            [z_ref[pl.ds(r0, LANES), 2 * HEAD_DIM:3 * HEAD_DIM], ones_col], axis=1).astype(BF16)
        return carry

    ones_col = jnp.where(lax.broadcasted_iota(jnp.int32, (LANES, LANES), 1) == 0, 1.0, 0.0)
    lax.fori_loop(0, n_real // LANES, finish_body, 0, unroll=4)
    atm_ref[...] = acol_ref[pl.ds(0, LANES), :].T[0:8, :]
    kt_meta = z_ref[pl.ds(0, LANES), HEAD_DIM:2 * HEAD_DIM].T[:, 0:N_META]
    vaug_meta = jnp.concatenate([z_ref[pl.ds(0, N_META), 2 * HEAD_DIM:3 * HEAD_DIM],
                                 jnp.where(lax.broadcasted_iota(jnp.int32, (N_META, LANES), 1) == 0, 1.0, 0.0)],
                                axis=1).astype(BF16)

    mask_m = _tri_mask(N_META, True)
    states = []
    for d in (0, 1):
        state = (jnp.zeros((HEAD_DIM, 2 * HEAD_DIM), F32), jnp.zeros((1, 1), F32))
        out_m, state = _mlstm_chunk(
            z_ref[pl.ds(0, N_META), 0:HEAD_DIM], kt_meta, vaug_meta,
            atm_ref[pl.ds(d, 1), 0:N_META], fcol_ref[pl.ds(0, N_META), d:d + 1], mask_m, state)
        acc_ref[d, pl.ds(0, N_META), :] = out_m
        states.append(state)

    def body(i, carry):
        new = []
        for d in (0, 1):
            c = i if d == 0 else n_chunks - 1 - i
            r0 = pl.multiple_of(N_META + c * t, 8)
            l0 = pl.multiple_of(c * t, LANES)
            out, st = _mlstm_chunk(
                z_ref[pl.ds(r0, t), 0:HEAD_DIM], kt_ref[:, pl.ds(l0, t)], vaug_ref[pl.ds(r0, t), :],
                atr_ref[pl.ds(d, 1), pl.ds(l0, t)], fcol_ref[pl.ds(r0, t), d:d + 1],
                _tri_mask(t, d == 0), carry[d])
            acc_ref[d, pl.ds(r0, t), :] = out
            new.append(st)
        return tuple(new)

    lax.fori_loop(0, n_chunks, body, tuple(states))

    for r0, nr in _row_blocks(n_rows):
        hs = acc_ref[0, pl.ds(r0, nr), :] + acc_ref[1, pl.ds(r0, nr), :]
        o = z_ref[pl.ds(r0, nr), 3 * HEAD_DIM:4 * HEAD_DIM]
        out_ref[0, pl.ds(r0, nr), :] = (_sigmoid(o) * _rms(hs, ng_ref[0])).astype(BF16)


def _mlstm_call(h, gn, w, conv, gb, ng):
    bsz, n_rows, d = h.shape
    nw = w.shape[-1]
    n_real = n_rows - N_META
    return pl.pallas_call(
        _mlstm_kernel,
        grid=(bsz, N_HEADS),
        in_specs=[
            pl.BlockSpec((1, n_rows, d), lambda b, hd: (b, 0, 0)),
            pl.BlockSpec((1, d), lambda b, hd: (0, 0)),
            pl.BlockSpec((1, d, nw), lambda b, hd: (hd, 0, 0)),
            pl.BlockSpec((1, CONV_K, 2 * HEAD_DIM), lambda b, hd: (hd, 0, 0)),
            pl.BlockSpec((1, 1, 2 * LANES), lambda b, hd: (hd, 0, 0)),
            pl.BlockSpec((1, 1, HEAD_DIM), lambda b, hd: (hd, 0, 0)),
        ],
        out_specs=pl.BlockSpec((1, n_rows, HEAD_DIM), lambda b, hd: (b, 0, hd)),
        out_shape=jax.ShapeDtypeStruct((bsz, n_rows, N_HEADS * HEAD_DIM), BF16),
        scratch_shapes=[
            pltpu.VMEM((n_rows, d), BF16),
            pltpu.VMEM((n_rows, nw), F32),
            pltpu.VMEM((n_rows + 2 * CONV_PAD, 2 * HEAD_DIM), F32),
            pltpu.VMEM((2, n_rows, HEAD_DIM), F32),
            pltpu.VMEM((n_rows, LANES), F32),
            pltpu.VMEM((n_rows, LANES), F32),
            pltpu.VMEM((8, LANES), F32),
            pltpu.VMEM((8, n_real), F32),
            pltpu.VMEM((HEAD_DIM, n_real), F32),
            pltpu.VMEM((n_rows, 2 * HEAD_DIM), BF16),
        ],
        compiler_params=_cparams(2),
        name="mlstm_heads",
    )(h, gn, w, conv, gb, ng)


def _ret_chunk(qc, kt, vc, dmat, dq, dk_row, dchunk, r_st):
    qb, vb = qc.astype(BF16), vc.astype(BF16)
    s = _dot(qb, kt.astype(BF16)) * dmat
    out = _dot(s.astype(BF16), vb) + dq * _dot(qb, r_st.astype(BF16))
    r_new = dchunk * r_st + _dot((kt * dk_row).astype(BF16), vb)
    return out, r_new


def _decay_tables(n, lg, forward):
    r = lax.broadcasted_iota(jnp.int32, (n, n), 0)
    c = lax.broadcasted_iota(jnp.int32, (n, n), 1)
    pos_q = lax.broadcasted_iota(jnp.int32, (n, 1), 0).astype(F32)
    pos_k = lax.broadcasted_iota(jnp.int32, (1, n), 1).astype(F32)
    if forward:
        dist, mask = (r - c).astype(F32), c <= r
        dq, dk = jnp.exp((pos_q + 1.0) * lg), jnp.exp((n - 1.0 - pos_k) * lg)
    else:
        dist, mask = (c - r).astype(F32), c >= r
        dq, dk = jnp.exp((n - pos_q) * lg), jnp.exp(pos_k * lg)
    dmat = jnp.exp(jnp.where(mask, dist * lg, NEG))
    return dmat, dq, dk, jnp.exp(n * lg)


def _ret_kernel(h_ref, gn_ref, w_ref, cos_ref, sin_ref, dl_ref, ng_ref, out_ref,
                un_ref, z_ref, acc_ref, dmat_ref, dq_ref, kt_ref):
    n_rows = h_ref.shape[1]
    n_real = n_rows - N_META
    t = SCAN_T
    n_chunks = n_real // t

    @pl.when(pl.program_id(1) == 0)
    def _():
        _norm_to_scratch(h_ref, gn_ref, un_ref)

    _project(un_ref, w_ref, z_ref, 0, 2 * HEAD_DIM)
    _project(un_ref, w_ref, z_ref, 2 * HEAD_DIM)

    for r0, nr in _row_blocks(n_rows):
        cs, sn = cos_ref[pl.ds(r0, nr), :], sin_ref[pl.ds(r0, nr), :]
        q = z_ref[pl.ds(r0, nr), 0:HEAD_DIM]
        z_ref[pl.ds(r0, nr), 0:HEAD_DIM] = q * cs + pltpu.roll(q, HEAD_DIM // 2, 1) * sn
        k = z_ref[pl.ds(r0, nr), HEAD_DIM:2 * HEAD_DIM]
        z_ref[pl.ds(r0, nr), HEAD_DIM:2 * HEAD_DIM] = (
            (k * cs + pltpu.roll(k, HEAD_DIM // 2, 1) * sn) * HEAD_DIM ** -0.5)

    for c in range(n_real // LANES):
        kt_ref[:, c * LANES:(c + 1) * LANES] = z_ref[pl.ds(N_META + c * LANES, LANES), HEAD_DIM:2 * HEAD_DIM].T
    kt_meta = z_ref[pl.ds(0, LANES), HEAD_DIM:2 * HEAD_DIM].T[:, 0:N_META]

    lg_all = _log_sigmoid(dl_ref[0])
    states, dk_rows, dchunks = [], [], []
    for d in (0, 1):
        lg = lg_all[:, d:d + 1]
        dmat_m, _, dk_m, dch_m = _decay_tables(N_META, lg, True)
        dmat, dq, dk, dch = _decay_tables(t, lg, d == 0)
        dmat_ref[d] = dmat
        dq_ref[d] = jnp.broadcast_to(dq, (t, LANES))
        dk_rows.append(dk)
        dchunks.append(dch)
        out_m, r_st = _ret_chunk(
            z_ref[pl.ds(0, N_META), 0:HEAD_DIM], kt_meta, z_ref[pl.ds(0, N_META), 2 * HEAD_DIM:3 * HEAD_DIM],
            dmat_m, jnp.zeros((N_META, 1), F32), dk_m, dch_m, jnp.zeros((HEAD_DIM, HEAD_DIM), F32))
        acc_ref[d, pl.ds(0, N_META), :] = out_m
        states.append(r_st)

    def body(i, carry):
        new = []
        for d in (0, 1):
            c = i if d == 0 else n_chunks - 1 - i
            r0 = pl.multiple_of(N_META + c * t, 8)
            l0 = pl.multiple_of(c * t, LANES)
            out, st = _ret_chunk(
                z_ref[pl.ds(r0, t), 0:HEAD_DIM], kt_ref[:, pl.ds(l0, t)],
                z_ref[pl.ds(r0, t), 2 * HEAD_DIM:3 * HEAD_DIM],
                dmat_ref[d], dq_ref[d], dk_rows[d], dchunks[d], carry[d])
            acc_ref[d, pl.ds(r0, t), :] = out
            new.append(st)
        return tuple(new)

    lax.fori_loop(0, n_chunks, body, tuple(states))

    for r0, nr in _row_blocks(n_rows):
        hs = acc_ref[0, pl.ds(r0, nr), :] + acc_ref[1, pl.ds(r0, nr), :]
        g = z_ref[pl.ds(r0, nr), 3 * HEAD_DIM:4 * HEAD_DIM]
        out_ref[0, pl.ds(r0, nr), :] = (g * _sigmoid(g) * _rms(hs, ng_ref[0])).astype(BF16)


def _ret_call(h, gn, w, cos2, sin2, dl, ng):
    bsz, n_rows, d = h.shape
    nw = w.shape[-1]
    return pl.pallas_call(
        _ret_kernel,
        grid=(bsz, N_HEADS),
        in_specs=[
            pl.BlockSpec((1, n_rows, d), lambda b, hd: (b, 0, 0)),
            pl.BlockSpec((1, d), lambda b, hd: (0, 0)),
            pl.BlockSpec((1, d, nw), lambda b, hd: (hd, 0, 0)),
            pl.BlockSpec((n_rows, HEAD_DIM), lambda b, hd: (0, 0)),
            pl.BlockSpec((n_rows, HEAD_DIM), lambda b, hd: (0, 0)),
            pl.BlockSpec((1, 1, LANES), lambda b, hd: (hd, 0, 0)),
            pl.BlockSpec((1, 1, HEAD_DIM), lambda b, hd: (hd, 0, 0)),
        ],
        out_specs=pl.BlockSpec((1, n_rows, HEAD_DIM), lambda b, hd: (b, 0, hd)),
        out_shape=jax.ShapeDtypeStruct((bsz, n_rows, N_HEADS * HEAD_DIM), BF16),
        scratch_shapes=[
            pltpu.VMEM((n_rows, d), BF16),
            pltpu.VMEM((n_rows, nw), F32),
            pltpu.VMEM((2, n_rows, HEAD_DIM), F32),
            pltpu.VMEM((2, SCAN_T, SCAN_T), F32),
            pltpu.VMEM((2, SCAN_T, LANES), F32),
            pltpu.VMEM((HEAD_DIM, n_rows - N_META), F32),
        ],
        compiler_params=_cparams(2),
        name="retention_heads",
    )(h, gn, w, cos2, sin2, dl, ng)


def _head_pair_block_diag(qp):
    lane = lax.broadcasted_iota(jnp.int32, qp.shape, 1)
    zero = jnp.zeros_like(qp)
    return jnp.concatenate([jnp.where(lane < NA_HEAD_DIM, qp, zero),
                            jnp.where(lane >= NA_HEAD_DIM, qp, zero)], axis=0)


def _na_attend(qp, keys, values, bias_t):
    n = qp.shape[0]
    bd = _head_pair_block_diag(qp)
    scores = []
    for kb, bt in zip(keys, bias_t):
        s = _dot_nt(kb, bd)
        scores.append(s if bt is None else s + bt)
    m = scores[0].max(axis=0, keepdims=True)
    for s in scores[1:]:
        m = jnp.maximum(m, s.max(axis=0, keepdims=True))
    probs = [jnp.exp(s - m) for s in scores]
    den = probs[0].sum(axis=0, keepdims=True)
    for p in probs[1:]:
        den = den + p.sum(axis=0, keepdims=True)
    inv = 1.0 / den
    o2 = None
    for p, vb in zip(probs, values):
        part = _dot_tn((p * inv).astype(BF16), vb)
        o2 = part if o2 is None else o2 + part
    lane = lax.broadcasted_iota(jnp.int32, (n, 2 * NA_HEAD_DIM), 1)
    return jnp.where(lane < NA_HEAD_DIM, o2[0:n, :], o2[n:2 * n, :])


def _na_kernel(h_ref, gn_ref, w_ref, qkg_ref, bias_ref, out_ref,
               un_ref, z_ref, q_ref, k_ref, vt_ref, ot_ref, s_ref, p_ref, den_ref):
    n_rows = h_ref.shape[1]
    n_real = n_rows - N_META
    n_grid_rows = n_real // GRID_W
    dh = NA_HEAD_DIM
    pw = 2 * dh
    band = NA_WIN_ROWS * GRID_W
    pairs = range(NA_PAIRS)
    v_off = NA_PAIRS * 2 * pw

    @pl.when(pl.program_id(1) == 0)
    def _():
        _norm_to_scratch(h_ref, gn_ref, un_ref)

    _project(un_ref, w_ref, z_ref, 0, v_off)
    _project(un_ref, w_ref, z_ref, v_off)

    r = lax.broadcasted_iota(jnp.int32, (2 * pw, 2 * pw), 0)
    c = lax.broadcasted_iota(jnp.int32, (2 * pw, 2 * pw), 1)
    head_ones = jnp.where(r // dh == c // dh, 1.0, 0.0).astype(BF16)
    for p in pairs:
        for r0, nr in _row_blocks(n_rows):
            x = z_ref[pl.ds(r0, nr), p * 2 * pw:(p + 1) * 2 * pw]
            ssq = _dot((x * x).astype(BF16), head_ones)
            y = (x * lax.rsqrt(ssq * (1.0 / dh) + EPS) * qkg_ref[...]).astype(BF16)
            q_ref[p, pl.ds(r0, nr), :] = y[:, 0:pw]
            k_ref[p, pl.ds(r0, nr), :] = y[:, pw:2 * pw]

    n_tblocks = n_real // LANES
    k_meta, vt_meta = [], []
    for p in pairs:
        v_cols = slice(v_off + p * pw, v_off + (p + 1) * pw)
        for cpy in (0, 1):
            for c in range(n_tblocks - cpy):
                vb = z_ref[pl.ds(N_META + cpy * GRID_W + c * LANES, LANES), v_cols]
                vt_ref[p, cpy, :, c * LANES:(c + 1) * LANES] = vb.T.astype(BF16)
        vt_ref[p, 1, :, (n_tblocks - 1) * LANES:n_tblocks * LANES] = jnp.zeros((pw, LANES), BF16)
        vt_meta.append(z_ref[pl.ds(0, LANES), v_cols].T[:, 0:N_META].astype(BF16))
        k_meta.append(k_ref[p, pl.ds(0, N_META), :])
        out_ref[0, pl.ds(0, N_META), p * pw:(p + 1) * pw] = _na_attend(
            q_ref[p, pl.ds(0, N_META), :], [k_meta[p]], [z_ref[pl.ds(0, N_META), v_cols].astype(BF16)],
            [None]).astype(BF16)

    sub = lax.broadcasted_iota(jnp.int32, (pw, pw), 0)
    lane = lax.broadcasted_iota(jnp.int32, (pw, pw), 1)

    def row_start(r):
        return jnp.clip(r - NA_WIN_ROWS // 2, 0, n_grid_rows - NA_WIN_ROWS)

    def stage_scores(r, slot):
        rs = row_start(r)
        q0 = pl.multiple_of(N_META + r * GRID_W, 8)
        k0 = pl.multiple_of(N_META + rs * GRID_W, 8)
        for p in pairs:
            bd = _head_pair_block_diag(q_ref[p, pl.ds(q0, GRID_W), :])
            bias = bias_ref[p, pl.ds(NA_WIN_ROWS - 1 - (r - rs), NA_WIN_ROWS)].reshape(band, pw)
            s_ref[p, slot, pl.ds(0, band), :] = _dot_nt(k_ref[p, pl.ds(k0, band), :], bd) + bias
            s_ref[p, slot, pl.ds(band, N_META), :] = _dot_nt(k_meta[p], bd)

    def stage_softmax(slot):
        for p in pairs:
            s = s_ref[p, slot]
            e = jnp.exp(s - s.max(axis=0, keepdims=True))
            den_ref[p, slot] = e.sum(axis=0, keepdims=True)
            p_ref[p, slot] = e.astype(BF16)

    def stage_values(r, slot):
        rs = row_start(r)
        par = rs % 2
        l0 = pl.multiple_of((rs - par) * GRID_W, LANES)
        for p in pairs:
            o_t = (_dot(vt_ref[p, par, :, pl.ds(l0, band)], p_ref[p, slot, pl.ds(0, band), :])
                   + _dot(vt_meta[p], p_ref[p, slot, pl.ds(band, N_META), :]))
            ot_ref[p, r] = o_t / den_ref[p, slot]

    stage_scores(0, 0)
    stage_softmax(0)
    stage_scores(1, 1)

    def pipe_body(j, carry):
        i0 = 2 + 2 * j
        stage_values(i0 - 2, 0)
        stage_softmax(1)
        stage_scores(i0, 0)
        stage_values(i0 - 1, 1)
        stage_softmax(0)
        stage_scores(i0 + 1, 1)
        return carry

    lax.fori_loop(0, (n_grid_rows - 2) // 2, pipe_body, 0)
    stage_values(n_grid_rows - 2, 0)
    stage_softmax(1)
    stage_values(n_grid_rows - 1, 1)

    for p in pairs:
        for c in range(n_tblocks):
            oa, ob = ot_ref[p, 2 * c], ot_ref[p, 2 * c + 1]
            sel_a = jnp.where(sub < dh, oa, pltpu.roll(oa, dh, 1))
            sel_b = jnp.where(sub < dh, pltpu.roll(ob, dh, 1), ob)
            out_ref[0, pl.ds(N_META + c * LANES, LANES), p * pw:(p + 1) * pw] = (
                jnp.where(lane < dh, sel_a, sel_b).T.astype(BF16))


def _na_call(h, gn, w, qg, kg, bias):
    bsz, n_rows, d = h.shape
    n_steps, _, nw = w.shape
    pw = 2 * NA_HEAD_DIM
    band = NA_WIN_ROWS * GRID_W
    n_real = n_rows - N_META
    qkg = jnp.concatenate([jnp.tile(qg, 2) * NA_HEAD_DIM ** -0.5, jnp.tile(kg, 2)])[None, :].astype(F32)
    return pl.pallas_call(
        _na_kernel,
        grid=(bsz, n_steps),
        in_specs=[
            pl.BlockSpec((1, n_rows, d), lambda b, s: (b, 0, 0)),
            pl.BlockSpec((1, d), lambda b, s: (0, 0)),
            pl.BlockSpec((1, d, nw), lambda b, s: (s, 0, 0)),
            pl.BlockSpec((1, 2 * pw), lambda b, s: (0, 0)),
            pl.BlockSpec((NA_PAIRS,) + bias.shape[1:], lambda b, s: (s, 0, 0, 0)),
        ],
        out_specs=pl.BlockSpec((1, n_rows, NA_PAIRS * pw), lambda b, s: (b, 0, s)),
        out_shape=jax.ShapeDtypeStruct((bsz, n_rows, n_steps * NA_PAIRS * pw), BF16),
        scratch_shapes=[
            pltpu.VMEM((n_rows, d), BF16),
            pltpu.VMEM((n_rows, nw), F32),
            pltpu.VMEM((NA_PAIRS, n_rows, pw), BF16),
            pltpu.VMEM((NA_PAIRS, n_rows, pw), BF16),
            pltpu.VMEM((NA_PAIRS, 2, pw, n_real), BF16),
            pltpu.VMEM((NA_PAIRS, n_real // GRID_W, pw, pw), F32),
            pltpu.VMEM((NA_PAIRS, 2, band + N_META, pw), F32),
            pltpu.VMEM((NA_PAIRS, 2, band + N_META, pw), BF16),
            pltpu.VMEM((NA_PAIRS, 2, 1, pw), F32),
        ],
        compiler_params=_cparams(2),
        name="neighbourhood_attention",
    )(h, gn, w, qkg, bias)


def _lane_prefix_exclusive(x, tri_strict):
    n_blocks = x.shape[1] // LANES
    carry = jnp.zeros((x.shape[0], 1), F32)
    pieces = []
    for j in range(n_blocks):
        blk = x[:, j * LANES:(j + 1) * LANES]
        pieces.append(_dot(blk.astype(BF16), tri_strict) + carry)
        carry = carry + jnp.sum(blk, axis=-1, keepdims=True)
    return jnp.concatenate(pieces, axis=1)


def _aligned_row_blocks(n_rows, step=512):
    blocks, r0 = [], 0
    while r0 < n_rows:
        nr = min(step, n_rows - r0)
        blocks.append((r0, nr))
        r0 += nr
    return blocks


def _outproj_kernel(n_mix, *refs):
    h_ref = refs[0]
    mix_refs = refs[1:1 + n_mix]
    wo_ref, fg_ref, wr_ref, hn_ref, u2_ref, lg_ref = refs[1 + n_mix:7 + n_mix]
    acc = h_ref[0]
    k0 = 0
    for m_ref in mix_refs:
        kw = m_ref.shape[2]
        acc = acc + _dot(m_ref[0], wo_ref[pl.ds(k0, kw), :])
        k0 += kw
    hn_ref[0] = acc
    u = _rms(acc, fg_ref[...])
    u_hi = u.astype(BF16)
    u2_ref[0] = u_hi
    u_lo = (u - u_hi.astype(F32)).astype(BF16)
    hh_hl = _dot(u_hi, wr_ref[...])
    lg_ref[0] = hh_hl[:, 0:LANES] + hh_hl[:, LANES:2 * LANES] + _dot(u_lo, wr_ref[:, 0:LANES])


def _outproj_call(h, mixes, wo, fg, wr):
    bsz, n_rows, d = h.shape
    n_mix = len(mixes)
    nr = _row_blocks(n_rows)[0][1]
    row_spec = lambda w: pl.BlockSpec((1, nr, w), lambda b, r: (b, r, 0))
    return pl.pallas_call(
        functools.partial(_outproj_kernel, n_mix),
        grid=(bsz, n_rows // nr),
        in_specs=[row_spec(d)] + [row_spec(m.shape[2]) for m in mixes] + [
            pl.BlockSpec((d, d), lambda b, r: (0, 0)),
            pl.BlockSpec((1, d), lambda b, r: (0, 0)),
            pl.BlockSpec((d, 2 * LANES), lambda b, r: (0, 0)),
        ],
        out_specs=[row_spec(d), row_spec(d), row_spec(LANES)],
        out_shape=[
            jax.ShapeDtypeStruct((bsz, n_rows, d), F32),
            jax.ShapeDtypeStruct((bsz, n_rows, d), BF16),
            jax.ShapeDtypeStruct((bsz, n_rows, LANES), F32),
        ],
        compiler_params=_cparams(2),
        name="outproj_norm_logits",
    )(h, *mixes, wo, fg, wr)


def _route_kernel(cap, lg_ref, slot_ref, gate_ref, lgc_ref):
    n_rows = lg_ref.shape[1]
    n_pad = slot_ref.shape[2]
    lgc_ref[pl.ds(0, n_rows), :] = lg_ref[0]
    lgc_ref[pl.ds(n_rows, n_pad - n_rows), :] = jnp.zeros((n_pad - n_rows, LANES), F32)

    logits = jnp.concatenate(
        [lgc_ref[pl.ds(c * LANES, LANES), :].T[0:N_EXPERTS, :] for c in range(n_pad // LANES)], axis=1)
    ex = jnp.exp(logits - jnp.max(logits, axis=0, keepdims=True))
    aff = ex / jnp.sum(ex, axis=0, keepdims=True)
    tok = lax.broadcasted_iota(jnp.int32, aff.shape, 1)
    aff = jnp.where(tok < n_rows, aff, -1.0)

    def count_ge(x):
        return jnp.sum(jnp.where(aff >= x, 1.0, 0.0), axis=-1, keepdims=True)

    capf = float(cap)

    def refine(base, cands):
        best = base
        for cand in cands:
            best = jnp.where(count_ge(cand) >= capf, cand, best)
        return best

    tiny = jnp.full((aff.shape[0], 1), 2.0 ** -126, F32)
    ok0 = count_ge(tiny) >= capf
    p = tiny
    p = refine(p, [p * (2.0 ** (16 * k)) for k in range(1, 8)])
    p = refine(p, [p * (2.0 ** (2 * k)) for k in range(1, 8)])
    p = refine(p, [p * 2.0])
    m = p
    for i in range(1, 8):
        step = p * (2.0 ** (-3 * i))
        m = refine(m, [m + k * step for k in range(1, 8)])
    step = p * (2.0 ** -23)
    m = refine(m, [m + k * step for k in range(1, 4)])
    thr = jnp.where(ok0, m, 0.0)

    gt = aff > thr
    eq = aff == thr
    need = capf - jnp.sum(jnp.where(gt, 1.0, 0.0), axis=-1, keepdims=True)
    r = lax.broadcasted_iota(jnp.int32, (LANES, LANES), 0)
    c = lax.broadcasted_iota(jnp.int32, (LANES, LANES), 1)
    tri_strict = jnp.where(r < c, 1.0, 0.0).astype(BF16)
    eq_rank = _lane_prefix_exclusive(jnp.where(eq, 1.0, 0.0), tri_strict)
    sel = gt | (eq & (eq_rank < need))
    pos = _lane_prefix_exclusive(jnp.where(sel, 1.0, 0.0), tri_strict)
    slot_ref[0] = jnp.where(sel, pos, -1.0).astype(jnp.int32)
    gate_ref[0] = jnp.where(sel, aff, 0.0)


def _route_call(lg, cap):
    bsz, n_rows, _ = lg.shape
    n_pad = ((n_rows + LANES - 1) // LANES) * LANES
    return pl.pallas_call(
        functools.partial(_route_kernel, cap),
        grid=(bsz,),
        in_specs=[pl.BlockSpec((1, n_rows, LANES), lambda b: (b, 0, 0))],
        out_specs=[
            pl.BlockSpec((1, N_EXPERTS, n_pad), lambda b: (b, 0, 0)),
            pl.BlockSpec((1, N_EXPERTS, n_pad), lambda b: (b, 0, 0)),
        ],
        out_shape=[
            jax.ShapeDtypeStruct((bsz, N_EXPERTS, n_pad), jnp.int32),
            jax.ShapeDtypeStruct((bsz, N_EXPERTS, n_pad), F32),
        ],
        scratch_shapes=[pltpu.VMEM((n_pad, LANES), F32)],
        compiler_params=_cparams(1),
        name="expert_choice_router",
    )(lg)


def _gather_kernel(u2_ref, slot_ref, xs_ref):
    cap_pad = xs_ref.shape[2]
    n_rows = u2_ref.shape[1]
    srow_id = lax.broadcasted_iota(jnp.int32, (cap_pad, n_rows), 0)

    def body(i, carry):
        pieces = [jnp.where(srow_id == slot_ref[0, pl.ds(i * GATHER_GROUP + j, 1), 0:n_rows],
                            1.0, 0.0).astype(BF16)
                  for j in range(GATHER_GROUP)]
        rows = _dot(jnp.concatenate(pieces, axis=0), u2_ref[0]).astype(BF16)
        for j in range(GATHER_GROUP):
            xs_ref[0, i * GATHER_GROUP + j] = rows[j * cap_pad:(j + 1) * cap_pad, :]
        return carry

    lax.fori_loop(0, N_EXPERTS // GATHER_GROUP, body, 0)


def _gather_call(u2, slot, cap_pad):
    bsz, n_rows, d = u2.shape
    n_pad = slot.shape[2]
    return pl.pallas_call(
        _gather_kernel,
        grid=(bsz,),
        in_specs=[
            pl.BlockSpec((1, n_rows, d), lambda b: (b, 0, 0)),
            pl.BlockSpec((1, N_EXPERTS, n_pad), lambda b: (b, 0, 0)),
        ],
        out_specs=pl.BlockSpec((1, N_EXPERTS, cap_pad, d), lambda b: (b, 0, 0, 0)),
        out_shape=jax.ShapeDtypeStruct((bsz, N_EXPERTS, cap_pad, d), BF16),
        compiler_params=_cparams(1),
        name="expert_gather",
    )(u2, slot)


def _expert_kernel(xs_ref, wg_ref, wu_ref, wd_ref, ys_ref, wgb_ref, wub_ref, wdb_ref):
    bb, _, cap_pad, d = xs_ref.shape

    @pl.when(pl.program_id(1) == 0)
    def _():
        wgb_ref[...] = wg_ref[0].astype(BF16)
        wub_ref[...] = wu_ref[0].astype(BF16)
        wdb_ref[...] = wd_ref[0].astype(BF16)

    x = xs_ref[...].reshape(bb * cap_pad, d)
    g = _dot(x, wgb_ref[...])
    u = _dot(x, wub_ref[...])
    hdn = (g * _sigmoid(g) * u).astype(BF16)
    ys_ref[...] = _dot(hdn, wdb_ref[...]).astype(BF16).reshape(ys_ref.shape)


def _expert_call(xs, wg, wu, wd, layer):
    bsz, n_e, cap_pad, d = xs.shape
    ff = wg.shape[-1]
    seq_blocks = 2 if bsz % 2 == 0 else 1
    bb = bsz // seq_blocks
    return pl.pallas_call(
        _expert_kernel,
        grid=(n_e, seq_blocks),
        in_specs=[
            pl.BlockSpec((bb, 1, cap_pad, d), lambda e, m: (m, e, 0, 0)),
            pl.BlockSpec((1, None, d, ff), lambda e, m: (layer, e, 0, 0)),
            pl.BlockSpec((1, None, d, ff), lambda e, m: (layer, e, 0, 0)),
            pl.BlockSpec((1, None, ff, d), lambda e, m: (layer, e, 0, 0)),
        ],
        out_specs=pl.BlockSpec((bb, 1, cap_pad, d), lambda e, m: (m, e, 0, 0)),
        out_shape=jax.ShapeDtypeStruct((bsz, n_e, cap_pad, d), BF16),
        scratch_shapes=[pltpu.VMEM((d, ff), BF16), pltpu.VMEM((d, ff), BF16), pltpu.VMEM((ff, d), BF16)],
        compiler_params=_cparams(2),
        name="expert_swiglu",
    )(xs, wg, wu, wd)


def _combine_kernel(h_ref, ym_ref, yt_ref, slot_ref, gate_ref, out_ref):
    n_rows = h_ref.shape[1]
    n_out = out_ref.shape[1]
    skip = n_rows - n_out
    _, e_grp, n_main, d = ym_ref.shape
    _, n_e, n_tail, _ = yt_ref.shape
    n_pad = slot_ref.shape[2]
    g = pl.program_id(1)

    @pl.when(g == 0)
    def _():
        out_ref[0] = h_ref[0, pl.ds(skip, n_out), :]

    def weights_for(e, first_slot, n_slots):
        srow_id = lax.broadcasted_iota(jnp.int32, (n_slots, n_pad), 0) + first_slot
        hit = srow_id == slot_ref[0, pl.ds(e, 1), :]
        return jnp.where(hit, gate_ref[0, pl.ds(e, 1), :], 0.0).astype(BF16)

    def accumulate(weights, ys):
        for t0, tn in _aligned_row_blocks(n_pad, 1024):
            lo, hi = max(t0, skip), min(t0 + tn, n_rows)
            if hi <= lo:
                continue
            part = _dot_tn(weights[:, t0:t0 + tn], ys)
            out_ref[0, pl.ds(lo - skip, hi - lo), :] += part[lo - t0:hi - t0, :]

    @pl.when(g == 0)
    def _():
        weights = jnp.concatenate([weights_for(e, n_main, n_tail) for e in range(n_e)], axis=0)
        accumulate(weights, yt_ref[0].reshape(n_e * n_tail, d))

    @pl.when(g > 0)
    def _():
        weights = jnp.concatenate([weights_for((g - 1) * e_grp + j, 0, n_main) for j in range(e_grp)], axis=0)
        accumulate(weights, ym_ref[0].reshape(e_grp * n_main, d))


def _combine_call(h, ys, slot, gate, n_out, e_grp=4):
    bsz, n_rows, d = h.shape
    _, n_e, cap_pad, _ = ys.shape
    n_pad = slot.shape[2]
    n_main_steps = n_e // e_grp
    n_tail = cap_pad - MXU_DEPTH
    return pl.pallas_call(
        _combine_kernel,
        grid=(bsz, n_main_steps + 1),
        in_specs=[
            pl.BlockSpec((1, n_rows, d), lambda b, g: (b, 0, 0)),
            pl.BlockSpec((1, e_grp, MXU_DEPTH, d), lambda b, g: (b, jnp.maximum(g - 1, 0), 0, 0)),
            pl.BlockSpec((1, n_e, n_tail, d), lambda b, g: (b, 0, MXU_DEPTH // n_tail, 0)),
            pl.BlockSpec((1, n_e, n_pad), lambda b, g: (b, 0, 0)),
            pl.BlockSpec((1, n_e, n_pad), lambda b, g: (b, 0, 0)),
        ],
        out_specs=pl.BlockSpec((1, n_out, d), lambda b, g: (b, 0, 0)),
        out_shape=jax.ShapeDtypeStruct((bsz, n_out, d), F32),
        compiler_params=_cparams(2),
        name="expert_combine",
    )(h, ys, ys, slot, gate)


def _even_weights(w_in, conv_w, gate_b, decay_logit):
    d = w_in.shape[0]
    mw = N_HEADS * HEAD_DIM
    mq, mk, mv, mo = (w_in[:, i * mw:(i + 1) * mw] for i in range(4))
    gates = w_in[:, 4 * mw:4 * mw + 4 * N_HEADS]
    r0 = 4 * mw + 4 * N_HEADS
    rq, rk, rv, rg = (w_in[:, r0 + i * mw:r0 + (i + 1) * mw] for i in range(4))

    def per_head(t):
        return t.reshape(d, N_HEADS, HEAD_DIM).transpose(1, 0, 2)

    def gate_cols(fw_off, bw_off):
        cols = jnp.stack([gates[:, fw_off:fw_off + N_HEADS], gates[:, bw_off:bw_off + N_HEADS]], axis=-1)
        cols = cols.transpose(1, 0, 2)
        return jnp.pad(cols, ((0, 0), (0, 0), (0, LANES - 2)))

    def gate_bias(fw_off, bw_off):
        b = jnp.stack([gate_b[fw_off:fw_off + N_HEADS], gate_b[bw_off:bw_off + N_HEADS]], axis=-1)
        return jnp.pad(b, ((0, 0), (0, LANES - 2)))

    w_m = jnp.concatenate([per_head(mq), per_head(mk), per_head(mv), per_head(mo),
                           gate_cols(0, 2 * N_HEADS), gate_cols(N_HEADS, 3 * N_HEADS)], axis=-1).astype(BF16)
    gb = jnp.concatenate([gate_bias(0, 2 * N_HEADS), gate_bias(N_HEADS, 3 * N_HEADS)], axis=-1)[:, None, :]
    conv = jnp.concatenate([conv_w[:, :mw].reshape(CONV_K, N_HEADS, HEAD_DIM),
                            conv_w[:, mw:].reshape(CONV_K, N_HEADS, HEAD_DIM)], axis=-1).transpose(1, 0, 2)
    w_r = jnp.concatenate([per_head(rq), per_head(rk), per_head(rv), per_head(rg)], axis=-1).astype(BF16)
    dl = jnp.pad(decay_logit.T, ((0, 0), (0, LANES - 2)))[:, None, :]
    return w_m, gb.astype(F32), conv.astype(F32), w_r, dl.astype(F32)


def _rotary_tables(n_rows):
    half = HEAD_DIM // 2
    inv = ROPE_BASE ** (-jnp.arange(half, dtype=F32) / half)
    ang = jnp.arange(n_rows, dtype=F32)[:, None] * inv[None, :]
    cos, sin = jnp.cos(ang), jnp.sin(ang)
    return jnp.concatenate([cos, cos], axis=-1), jnp.concatenate([-sin, sin], axis=-1)


def _na_weights(w_in):
    d = w_in.shape[0]
    n_heads = w_in.shape[1] // (3 * NA_HEAD_DIM)
    pw = 2 * NA_HEAD_DIM
    n_steps = n_heads // (2 * NA_PAIRS)
    z = w_in.reshape(d, 3, n_steps, NA_PAIRS, pw)
    qk = z[:, 0:2].transpose(2, 0, 3, 1, 4).reshape(n_steps, d, NA_PAIRS * 2 * pw)
    v = z[:, 2].transpose(1, 0, 2, 3).reshape(n_steps, d, NA_PAIRS * pw)
    return jnp.concatenate([qk, v], axis=-1).astype(BF16)


def _na_bias_table(rpb):
    col = np.arange(GRID_W)
    col_start = np.clip(col - NA_WIN_COLS // 2, 0, GRID_W - NA_WIN_COLS)
    col_in = (col[None, :] >= col_start[:, None]) & (col[None, :] < col_start[:, None] + NA_WIN_COLS)
    dc_idx = np.clip(col[None, :] - col[:, None], -(NA_WIN_COLS - 1), NA_WIN_COLS - 1) + NA_WIN_COLS - 1
    selector = (np.arange(rpb.shape[2])[:, None, None] == dc_idx[None]).astype(np.float32)
    rpb_cols = jnp.einsum('hrd,dqk->hrqk', rpb.astype(F32), selector,
                          precision=lax.Precision.HIGHEST)
    tbl = jnp.where(col_in[None, None], rpb_cols, NEG)
    n_pairs, n_dr = rpb.shape[0] // 2, rpb.shape[1]
    tbl = tbl.reshape(n_pairs, 2, n_dr, GRID_W, GRID_W).transpose(0, 2, 4, 1, 3)
    return tbl.reshape(n_pairs, n_dr, GRID_W, 2 * GRID_W)


def _ffn(h, mixes, wo, fg, wr, wg, wu, wd, layer, n_out):
    n_rows = h.shape[1]
    cap = CAP_FACTOR * n_rows // N_EXPERTS
    cap_pad = ((cap + BF16_ROWS - 1) // BF16_ROWS) * BF16_ROWS
    wr_hi = wr.astype(BF16)
    wr_lo = (wr - wr_hi.astype(F32)).astype(BF16)
    pad = ((0, 0), (0, LANES - N_EXPERTS))
    wr_pieces = jnp.concatenate([jnp.pad(wr_hi, pad), jnp.pad(wr_lo, pad)], axis=1)
    hn, u2, lg = _outproj_call(h, mixes, wo.astype(BF16), fg[None, :], wr_pieces)
    slot, gate = _route_call(lg, cap)
    xs = _gather_call(u2, slot, cap_pad)
    ys = _expert_call(xs, wg, wu, wd, layer)
    return _combine_call(hn, ys, slot, gate, n_out)


def kernel(x, meta_tokens, attn_norm_g, ffn_norm_g, even_w_in, even_conv_w, even_gate_b, even_m_norm_g, even_ret_decay_logit, even_r_norm_g, even_w_out, odd_w_in, odd_q_norm_g, odd_k_norm_g, odd_rpb, odd_w_out, router_w, expert_w_gate, expert_w_up, expert_w_down):
    bsz = x.shape[0]
    depth = attn_norm_g.shape[0]
    meta = jnp.broadcast_to(meta_tokens.astype(x.dtype)[None], (bsz,) + meta_tokens.shape)
    h = jnp.concatenate([meta, x], axis=1)
    n_rows = h.shape[1]
    cos2, sin2 = _rotary_tables(n_rows)
    for layer in range(depth):
        j = layer // 2
        gn = attn_norm_g[layer][None, :]
        if layer % 2 == 0:
            w_m, gb, conv, w_r, dl = _even_weights(even_w_in[j], even_conv_w[j], even_gate_b[j],
                                                   even_ret_decay_logit[j])
            m_out = _mlstm_call(h, gn, w_m, conv, gb, even_m_norm_g[j].reshape(N_HEADS, 1, HEAD_DIM))
            r_out = _ret_call(h, gn, w_r, cos2, sin2, dl, even_r_norm_g[j].reshape(N_HEADS, 1, HEAD_DIM))
            mixes, wo = [m_out, r_out], even_w_out[j]
        else:
            a_out = _na_call(h, gn, _na_weights(odd_w_in[j]), odd_q_norm_g[j], odd_k_norm_g[j],
                             _na_bias_table(odd_rpb[j]))
            mixes, wo = [a_out], odd_w_out[j]
        n_out = n_rows - N_META if layer == depth - 1 else n_rows
        h = _ffn(h, mixes, wo, ffn_norm_g[layer], router_w[layer],
                 expert_w_gate, expert_w_up, expert_w_down, layer, n_out)
    return h
```

```python
import functools

import jax
import jax.numpy as jnp
import numpy as np
from jax import lax
from jax.experimental import pallas as pl
from jax.experimental.pallas import tpu as pltpu

F32 = jnp.float32
BF16 = jnp.bfloat16

LANES = 128
BF16_ROWS = 16
MXU_DEPTH = 256
N_META = 16
GRID_W = 64
EPS = 1e-6
HEAD_DIM = 128
N_HEADS = 4
CONV_K = 5
CONV_PAD = 8
ROPE_BASE = 10000.0
NA_HEAD_DIM = 64
NA_WIN_ROWS = 8
NA_WIN_COLS = 16
NA_PAIRS = 2
N_EXPERTS = 16
CAP_FACTOR = 2
GATHER_GROUP = 2
NEG = -1e30
SCAN_T = 256
VMEM_LIMIT = 56 * 1024 * 1024


def _cparams(n_axes):
    return pltpu.CompilerParams(
        dimension_semantics=("arbitrary",) * n_axes, vmem_limit_bytes=VMEM_LIMIT)


def _row_blocks(n_rows):
    for nb in (3, 2, 4, 6, 1):
        if n_rows % (8 * nb) == 0:
            step = n_rows // nb
            return [(i * step, step) for i in range(nb)]
    return [(0, n_rows)]


def _rms(x, g):
    return x * lax.rsqrt(jnp.mean(x * x, axis=-1, keepdims=True) + EPS) * g


def _sigmoid(x):
    return 1.0 / (1.0 + jnp.exp(-x))


def _log_sigmoid(x):
    return jnp.minimum(x, 0.0) - jnp.log(1.0 + jnp.exp(-jnp.abs(x)))


def _dot(a, b):
    return jnp.dot(a, b, preferred_element_type=F32)


def _dot_nt(a, b):
    return lax.dot_general(a, b, (((1,), (1,)), ((), ())), preferred_element_type=F32)


def _dot_tn(a, b):
    return lax.dot_general(a, b, (((0,), (0,)), ((), ())), preferred_element_type=F32)


def _split3(x):
    hi = x.astype(BF16)
    r1 = x - hi.astype(F32)
    mid = r1.astype(BF16)
    lo = (r1 - mid.astype(F32)).astype(BF16)
    return hi, mid, lo


def _tri_prefix(tri_bf16, x):
    hi, mid, lo = _split3(x)
    return _dot(tri_bf16, hi) + _dot(tri_bf16, mid) + _dot(tri_bf16, lo)


def _norm_to_scratch(h_ref, gn_ref, un_ref):
    n_rows = h_ref.shape[1]
    for r0, nr in _row_blocks(n_rows):
        x = h_ref[0, pl.ds(r0, nr), :]
        un_ref[pl.ds(r0, nr), :] = _rms(x, gn_ref[...]).astype(BF16)


def _assemble_weights(w_refs, wcat_ref):
    c0 = 0
    for w_ref in w_refs:
        w = w_ref[...] if len(w_ref.shape) == 2 else w_ref[0]
        wcat_ref[:, c0:c0 + w.shape[1]] = w
        c0 += w.shape[1]


def _project(un_ref, wcat_ref, z_ref):
    n_rows = un_ref.shape[0]
    for r0, nr in _row_blocks(n_rows):
        z_ref[pl.ds(r0, nr), :] = _dot(un_ref[pl.ds(r0, nr), :], wcat_ref[...])


def _tri_mask(n, lower):
    r = lax.broadcasted_iota(jnp.int32, (n, n), 0)
    c = lax.broadcasted_iota(jnp.int32, (n, n), 1)
    return (c <= r) if lower else (c >= r)


def _mlstm_chunk(qc, kt, v_aug, a_row, f_col, mask, state):
    c_aug, g_prev = state
    qb = qc.astype(BF16)
    cm = jnp.max(jnp.where(mask, a_row, NEG), axis=-1, keepdims=True)
    g_col = jnp.maximum(g_prev, cm)
    dm = jnp.exp(jnp.where(mask, a_row - g_col, NEG))
    s = _dot(qb, kt.astype(BF16)) * dm
    w_inter = jnp.exp(g_prev - g_col)
    tot = w_inter * _dot(qb, c_aug.astype(BF16)) + _dot(s.astype(BF16), v_aug)
    den = tot[:, HEAD_DIM:2 * HEAD_DIM][:, 0:1]
    out = tot[:, 0:HEAD_DIM] / jnp.maximum(jnp.abs(den), jnp.exp(-(f_col + g_col)))
    g_end = jnp.maximum(g_prev, jnp.max(a_row, axis=-1, keepdims=True))
    ktw = (kt * jnp.exp(a_row - g_end)).astype(BF16)
    c_new = jnp.exp(g_prev - g_end) * c_aug + _dot(ktw, v_aug)
    return out, (c_new, g_end)


def _mlstm_kernel(h_ref, gn_ref, wq_ref, wk_ref, wv_ref, wo_ref, wg_ref, conv_ref, gb_ref, ng_ref, out_ref,
                  un_ref, z_ref, zc_ref, acc_ref, acol_ref, fcol_ref, atm_ref, atr_ref, kt_ref, vaug_ref,
                  wcat_ref):
    n_rows = h_ref.shape[1]
    n_real = n_rows - N_META
    t = SCAN_T
    n_chunks = n_real // t

    @pl.when(pl.program_id(1) == 0)
    def _():
        _norm_to_scratch(h_ref, gn_ref, un_ref)

    _assemble_weights([wq_ref, wk_ref, wv_ref, wo_ref, wg_ref], wcat_ref)
    _project(un_ref, wcat_ref, z_ref)

    zc_ref[pl.ds(0, CONV_PAD), :] = jnp.zeros((CONV_PAD, 2 * HEAD_DIM), F32)
    zc_ref[pl.ds(CONV_PAD + n_rows, CONV_PAD), :] = jnp.zeros((CONV_PAD, 2 * HEAD_DIM), F32)
    for r0, nr in _row_blocks(n_rows):
        zc_ref[pl.ds(CONV_PAD + r0, nr), :] = z_ref[pl.ds(r0, nr), 0:2 * HEAD_DIM]
    lane2 = lax.broadcasted_iota(jnp.int32, (1, 2 * HEAD_DIM), 1)
    qk_scale = jnp.where(lane2 >= HEAD_DIM, HEAD_DIM ** -0.5, 1.0).astype(F32)
    for r0, nr in _row_blocks(n_rows):
        acc = jnp.zeros((nr, 2 * HEAD_DIM), F32)
        for j in range(CONV_K):
            off = CONV_PAD - (CONV_K - 1) // 2 + j + r0
            acc = acc + zc_ref[pl.ds(off, nr), :] * conv_ref[0, pl.ds(j, 1), :]
        z_ref[pl.ds(r0, nr), 0:2 * HEAD_DIM] = acc * _sigmoid(acc) * qk_scale

    gi_off, gf_off = 4 * HEAD_DIM, 5 * HEAD_DIM
    bias_i = gb_ref[0, :, 0:LANES]
    bias_f = gb_ref[0, :, LANES:2 * LANES]
    lane = lax.broadcasted_iota(jnp.int32, (1, LANES), 1)
    tri_m = jnp.where(_tri_mask(N_META, True), 1.0, 0.0).astype(BF16)
    lf_m = _log_sigmoid(z_ref[pl.ds(0, N_META), gf_off:gf_off + LANES] + bias_f)
    f_meta = _tri_prefix(tri_m, lf_m)
    f_meta_end = f_meta[N_META - 1:N_META, :]
    tri_b = jnp.where(_tri_mask(LANES, True), 1.0, 0.0).astype(BF16)

    def prefix_body(c, carry):
        r0 = pl.multiple_of(N_META + c * LANES, 8)
        lf = _log_sigmoid(z_ref[pl.ds(r0, LANES), gf_off:gf_off + LANES] + bias_f)
        p = _tri_prefix(tri_b, lf) + carry
        fcol_ref[pl.ds(r0, LANES), :] = p
        return p[LANES - 1:LANES, :]

    total = lax.fori_loop(0, n_real // LANES, prefix_body, jnp.zeros((1, LANES), F32), unroll=4)

    fcol_ref[pl.ds(0, N_META), :] = f_meta
    acol_ref[pl.ds(0, N_META), :] = z_ref[pl.ds(0, N_META), gi_off:gi_off + LANES] + bias_i - f_meta

    def finish_body(c, carry):
        r0 = pl.multiple_of(N_META + c * LANES, 8)
        lf = _log_sigmoid(z_ref[pl.ds(r0, LANES), gf_off:gf_off + LANES] + bias_f)
        p = fcol_ref[pl.ds(r0, LANES), :]
        f = f_meta_end + jnp.where(lane == 0, p, total - p + lf)
        fcol_ref[pl.ds(r0, LANES), :] = f
        a = z_ref[pl.ds(r0, LANES), gi_off:gi_off + LANES] + bias_i - f
        acol_ref[pl.ds(r0, LANES), :] = a
        l0 = pl.multiple_of(c * LANES, LANES)
        atr_ref[:, pl.ds(l0, LANES)] = a.T[0:8, :]
        kt_ref[:, pl.ds(l0, LANES)] = z_ref[pl.ds(r0, LANES), HEAD_DIM:2 * HEAD_DIM].T
        vaug_ref[pl.ds(r0, LANES), :] = jnp.concatenate(
            [z_ref[pl.ds(r0, LANES), 2 * HEAD_DIM:3 * HEAD_DIM], ones_col], axis=1).astype(BF16)
        return carry

    ones_col = jnp.where(lax.broadcasted_iota(jnp.int32, (LANES, LANES), 1) == 0, 1.0, 0.0)
    lax.fori_loop(0, n_real // LANES, finish_body, 0, unroll=4)
    atm_ref[...] = acol_ref[pl.ds(0, LANES), :].T[0:8, :]
    kt_meta = z_ref[pl.ds(0, LANES), HEAD_DIM:2 * HEAD_DIM].T[:, 0:N_META]
    vaug_meta = jnp.concatenate([z_ref[pl.ds(0, N_META), 2 * HEAD_DIM:3 * HEAD_DIM],
                                 jnp.where(lax.broadcasted_iota(jnp.int32, (N_META, LANES), 1) == 0, 1.0, 0.0)],
                                axis=1).astype(BF16)

    mask_m = _tri_mask(N_META, True)
    states = []
    for d in (0, 1):
        state = (jnp.zeros((HEAD_DIM, 2 * HEAD_DIM), F32), jnp.zeros((1, 1), F32))
        out_m, state = _mlstm_chunk(
            z_ref[pl.ds(0, N_META), 0:HEAD_DIM], kt_meta, vaug_meta,
            atm_ref[pl.ds(d, 1), 0:N_META], fcol_ref[pl.ds(0, N_META), d:d + 1], mask_m, state)
        acc_ref[d, pl.ds(0, N_META), :] = out_m
        states.append(state)

    def body(i, carry):
        new = []
        for d in (0, 1):
            c = i if d == 0 else n_chunks - 1 - i
            r0 = pl.multiple_of(N_META + c * t, 8)
            l0 = pl.multiple_of(c * t, LANES)
            out, st = _mlstm_chunk(
                z_ref[pl.ds(r0, t), 0:HEAD_DIM], kt_ref[:, pl.ds(l0, t)], vaug_ref[pl.ds(r0, t), :],
                atr_ref[pl.ds(d, 1), pl.ds(l0, t)], fcol_ref[pl.ds(r0, t), d:d + 1],
                _tri_mask(t, d == 0), carry[d])
            acc_ref[d, pl.ds(r0, t), :] = out
            new.append(st)
        return tuple(new)

    lax.fori_loop(0, n_chunks, body, tuple(states))

    for r0, nr in _row_blocks(n_rows):
        hs = acc_ref[0, pl.ds(r0, nr), :] + acc_ref[1, pl.ds(r0, nr), :]
        o = z_ref[pl.ds(r0, nr), 3 * HEAD_DIM:4 * HEAD_DIM]
        out_ref[0, pl.ds(r0, nr), :] = (_sigmoid(o) * _rms(hs, ng_ref[0])).astype(BF16)


def _head_column_specs(d, n_groups):
    return [pl.BlockSpec((d, HEAD_DIM), functools.partial(lambda g, b, hd: (0, g * N_HEADS + hd), g))
            for g in range(n_groups)]


def _mlstm_call(h, gn, w, wg, conv, gb, ng):
    bsz, n_rows, d = h.shape
    nw = 4 * HEAD_DIM + wg.shape[-1]
    n_real = n_rows - N_META
    return pl.pallas_call(
        _mlstm_kernel,
        grid=(bsz, N_HEADS),
        in_specs=[
            pl.BlockSpec((1, n_rows, d), lambda b, hd: (b, 0, 0)),
            pl.BlockSpec((1, d), lambda b, hd: (0, 0)),
            *_head_column_specs(d, 4),
            pl.BlockSpec((1, d, wg.shape[-1]), lambda b, hd: (hd, 0, 0)),
            pl.BlockSpec((1, CONV_K, 2 * HEAD_DIM), lambda b, hd: (hd, 0, 0)),
            pl.BlockSpec((1, 1, 2 * LANES), lambda b, hd: (hd, 0, 0)),
            pl.BlockSpec((1, 1, HEAD_DIM), lambda b, hd: (hd, 0, 0)),
        ],
        out_specs=pl.BlockSpec((1, n_rows, HEAD_DIM), lambda b, hd: (b, 0, hd)),
        out_shape=jax.ShapeDtypeStruct((bsz, n_rows, N_HEADS * HEAD_DIM), BF16),
        scratch_shapes=[
            pltpu.VMEM((n_rows, d), BF16),
            pltpu.VMEM((n_rows, nw), F32),
            pltpu.VMEM((n_rows + 2 * CONV_PAD, 2 * HEAD_DIM), F32),
            pltpu.VMEM((2, n_rows, HEAD_DIM), F32),
            pltpu.VMEM((n_rows, LANES), F32),
            pltpu.VMEM((n_rows, LANES), F32),
            pltpu.VMEM((8, LANES), F32),
            pltpu.VMEM((8, n_real), F32),
            pltpu.VMEM((HEAD_DIM, n_real), F32),
            pltpu.VMEM((n_rows, 2 * HEAD_DIM), BF16),
            pltpu.VMEM((d, nw), BF16),
        ],
        compiler_params=_cparams(2),
        name="mlstm_heads",
    )(h, gn, w, w, w, w, wg, conv, gb, ng)


def _ret_chunk(qc, kt, vc, dmat, dq, dk_row, dchunk, r_st):
    qb, vb = qc.astype(BF16), vc.astype(BF16)
    s = _dot(qb, kt.astype(BF16)) * dmat
    out = _dot(s.astype(BF16), vb) + dq * _dot(qb, r_st.astype(BF16))
    r_new = dchunk * r_st + _dot((kt * dk_row).astype(BF16), vb)
    return out, r_new


def _decay_tables(n, lg, forward):
    r = lax.broadcasted_iota(jnp.int32, (n, n), 0)
    c = lax.broadcasted_iota(jnp.int32, (n, n), 1)
    pos_q = lax.broadcasted_iota(jnp.int32, (n, 1), 0).astype(F32)
    pos_k = lax.broadcasted_iota(jnp.int32, (1, n), 1).astype(F32)
    if forward:
        dist, mask = (r - c).astype(F32), c <= r
        dq, dk = jnp.exp((pos_q + 1.0) * lg), jnp.exp((n - 1.0 - pos_k) * lg)
    else:
        dist, mask = (c - r).astype(F32), c >= r
        dq, dk = jnp.exp((n - pos_q) * lg), jnp.exp(pos_k * lg)
    dmat = jnp.exp(jnp.where(mask, dist * lg, NEG))
    return dmat, dq, dk, jnp.exp(n * lg)


def _ret_kernel(h_ref, gn_ref, wq_ref, wk_ref, wv_ref, wg_ref, cos_ref, sin_ref, dl_ref, ng_ref, out_ref,
                un_ref, z_ref, acc_ref, dmat_ref, dq_ref, kt_ref, wcat_ref):
    n_rows = h_ref.shape[1]
    n_real = n_rows - N_META
    t = SCAN_T
    n_chunks = n_real // t

    @pl.when(pl.program_id(1) == 0)
    def _():
        _norm_to_scratch(h_ref, gn_ref, un_ref)

    _assemble_weights([wq_ref, wk_ref, wv_ref, wg_ref], wcat_ref)
    _project(un_ref, wcat_ref, z_ref)

    for r0, nr in _row_blocks(n_rows):
        cs, sn = cos_ref[pl.ds(r0, nr), :], sin_ref[pl.ds(r0, nr), :]
        q = z_ref[pl.ds(r0, nr), 0:HEAD_DIM]
        z_ref[pl.ds(r0, nr), 0:HEAD_DIM] = q * cs + pltpu.roll(q, HEAD_DIM // 2, 1) * sn
        k = z_ref[pl.ds(r0, nr), HEAD_DIM:2 * HEAD_DIM]
        z_ref[pl.ds(r0, nr), HEAD_DIM:2 * HEAD_DIM] = (
            (k * cs + pltpu.roll(k, HEAD_DIM // 2, 1) * sn) * HEAD_DIM ** -0.5)

    for c in range(n_real // LANES):
        kt_ref[:, c * LANES:(c + 1) * LANES] = z_ref[pl.ds(N_META + c * LANES, LANES), HEAD_DIM:2 * HEAD_DIM].T
    kt_meta = z_ref[pl.ds(0, LANES), HEAD_DIM:2 * HEAD_DIM].T[:, 0:N_META]

    lg_all = _log_sigmoid(dl_ref[0])
    states, dk_rows, dchunks = [], [], []
    for d in (0, 1):
        lg = lg_all[:, d:d + 1]
        dmat_m, _, dk_m, dch_m = _decay_tables(N_META, lg, True)
        dmat, dq, dk, dch = _decay_tables(t, lg, d == 0)
        dmat_ref[d] = dmat
        dq_ref[d] = jnp.broadcast_to(dq, (t, LANES))
        dk_rows.append(dk)
        dchunks.append(dch)
        out_m, r_st = _ret_chunk(
            z_ref[pl.ds(0, N_META), 0:HEAD_DIM], kt_meta, z_ref[pl.ds(0, N_META), 2 * HEAD_DIM:3 * HEAD_DIM],
            dmat_m, jnp.zeros((N_META, 1), F32), dk_m, dch_m, jnp.zeros((HEAD_DIM, HEAD_DIM), F32))
        acc_ref[d, pl.ds(0, N_META), :] = out_m
        states.append(r_st)

    def body(i, carry):
        new = []
        for d in (0, 1):
            c = i if d == 0 else n_chunks - 1 - i
            r0 = pl.multiple_of(N_META + c * t, 8)
            l0 = pl.multiple_of(c * t, LANES)
            out, st = _ret_chunk(
                z_ref[pl.ds(r0, t), 0:HEAD_DIM], kt_ref[:, pl.ds(l0, t)],
                z_ref[pl.ds(r0, t), 2 * HEAD_DIM:3 * HEAD_DIM],
                dmat_ref[d], dq_ref[d], dk_rows[d], dchunks[d], carry[d])
            acc_ref[d, pl.ds(r0, t), :] = out
            new.append(st)
        return tuple(new)

    lax.fori_loop(0, n_chunks, body, tuple(states))

    for r0, nr in _row_blocks(n_rows):
        hs = acc_ref[0, pl.ds(r0, nr), :] + acc_ref[1, pl.ds(r0, nr), :]
        g = z_ref[pl.ds(r0, nr), 3 * HEAD_DIM:4 * HEAD_DIM]
        out_ref[0, pl.ds(r0, nr), :] = (g * _sigmoid(g) * _rms(hs, ng_ref[0])).astype(BF16)


def _ret_call(h, gn, w, cos2, sin2, dl, ng):
    bsz, n_rows, d = h.shape
    nw = 4 * HEAD_DIM
    return pl.pallas_call(
        _ret_kernel,
        grid=(bsz, N_HEADS),
        in_specs=[
            pl.BlockSpec((1, n_rows, d), lambda b, hd: (b, 0, 0)),
            pl.BlockSpec((1, d), lambda b, hd: (0, 0)),
            *_head_column_specs(d, 4),
            pl.BlockSpec((n_rows, HEAD_DIM), lambda b, hd: (0, 0)),
            pl.BlockSpec((n_rows, HEAD_DIM), lambda b, hd: (0, 0)),
            pl.BlockSpec((1, 1, LANES), lambda b, hd: (hd, 0, 0)),
            pl.BlockSpec((1, 1, HEAD_DIM), lambda b, hd: (hd, 0, 0)),
        ],
        out_specs=pl.BlockSpec((1, n_rows, HEAD_DIM), lambda b, hd: (b, 0, hd)),
        out_shape=jax.ShapeDtypeStruct((bsz, n_rows, N_HEADS * HEAD_DIM), BF16),
        scratch_shapes=[
            pltpu.VMEM((n_rows, d), BF16),
            pltpu.VMEM((n_rows, nw), F32),
            pltpu.VMEM((2, n_rows, HEAD_DIM), F32),
            pltpu.VMEM((2, SCAN_T, SCAN_T), F32),
            pltpu.VMEM((2, SCAN_T, LANES), F32),
            pltpu.VMEM((HEAD_DIM, n_rows - N_META), F32),
            pltpu.VMEM((d, nw), BF16),
        ],
        compiler_params=_cparams(2),
        name="retention_heads",
    )(h, gn, w, w, w, w, cos2, sin2, dl, ng)


def _head_pair_block_diag(qp):
    lane = lax.broadcasted_iota(jnp.int32, qp.shape, 1)
    zero = jnp.zeros_like(qp)
    return jnp.concatenate([jnp.where(lane < NA_HEAD_DIM, qp, zero),
                            jnp.where(lane >= NA_HEAD_DIM, qp, zero)], axis=0)


def _na_attend(qp, keys, values, bias_t):
    n = qp.shape[0]
    bd = _head_pair_block_diag(qp)
    scores = []
    for kb, bt in zip(keys, bias_t):
        s = _dot_nt(kb, bd)
        scores.append(s if bt is None else s + bt)
    m = scores[0].max(axis=0, keepdims=True)
    for s in scores[1:]:
        m = jnp.maximum(m, s.max(axis=0, keepdims=True))
    probs = [jnp.exp(s - m) for s in scores]
    den = probs[0].sum(axis=0, keepdims=True)
    for p in probs[1:]:
        den = den + p.sum(axis=0, keepdims=True)
    inv = 1.0 / den
    o2 = None
    for p, vb in zip(probs, values):
        part = _dot_tn((p * inv).astype(BF16), vb)
        o2 = part if o2 is None else o2 + part
    lane = lax.broadcasted_iota(jnp.int32, (n, 2 * NA_HEAD_DIM), 1)
    return jnp.where(lane < NA_HEAD_DIM, o2[0:n, :], o2[n:2 * n, :])


def _na_kernel(h_ref, gn_ref, wq_ref, wk_ref, wv_ref, qkg_ref, bias_ref, out_ref,
               un_ref, z_ref, q_ref, k_ref, vt_ref, ot_ref, s_ref, p_ref, den_ref, wcat_ref, bias_s_ref):
    n_rows = h_ref.shape[1]
    n_real = n_rows - N_META
    n_grid_rows = n_real // GRID_W
    dh = NA_HEAD_DIM
    pw = 2 * dh
    gw = NA_PAIRS * pw
    band = NA_WIN_ROWS * GRID_W
    pairs = range(NA_PAIRS)
    v_off = 2 * gw

    @pl.when(pl.program_id(1) == 0)
    def _():
        _norm_to_scratch(h_ref, gn_ref, un_ref)

    _assemble_weights([wq_ref, wk_ref, wv_ref], wcat_ref)
    _project(un_ref, wcat_ref, z_ref)

    for p in pairs:
        for dr in range(bias_ref.shape[1]):
            bias_s_ref[p, dr] = jnp.concatenate([bias_ref[2 * p, dr], bias_ref[2 * p + 1, dr]], axis=1)

    r = lax.broadcasted_iota(jnp.int32, (gw, gw), 0)
    c = lax.broadcasted_iota(jnp.int32, (gw, gw), 1)
    head_ones = jnp.where(r // dh == c // dh, 1.0, 0.0).astype(BF16)
    for g, dst_ref in enumerate((q_ref, k_ref)):
        for r0, nr in _row_blocks(n_rows):
            x = z_ref[pl.ds(r0, nr), g * gw:(g + 1) * gw]
            ssq = _dot((x * x).astype(BF16), head_ones)
            y = (x * lax.rsqrt(ssq * (1.0 / dh) + EPS) * qkg_ref[:, g * gw:(g + 1) * gw]).astype(BF16)
            for p in pairs:
                dst_ref[p, pl.ds(r0, nr), :] = y[:, p * pw:(p + 1) * pw]

    n_tblocks = n_real // LANES
    k_meta, vt_meta = [], []
    for p in pairs:
        v_cols = slice(v_off + p * pw, v_off + (p + 1) * pw)
        for cpy in (0, 1):
            for c in range(n_tblocks - cpy):
                vb = z_ref[pl.ds(N_META + cpy * GRID_W + c * LANES, LANES), v_cols]
                vt_ref[p, cpy, :, c * LANES:(c + 1) * LANES] = vb.T.astype(BF16)
        vt_ref[p, 1, :, (n_tblocks - 1) * LANES:n_tblocks * LANES] = jnp.zeros((pw, LANES), BF16)
        vt_meta.append(z_ref[pl.ds(0, LANES), v_cols].T[:, 0:N_META].astype(BF16))
        k_meta.append(k_ref[p, pl.ds(0, N_META), :])
        out_ref[0, pl.ds(0, N_META), p * pw:(p + 1) * pw] = _na_attend(
            q_ref[p, pl.ds(0, N_META), :], [k_meta[p]], [z_ref[pl.ds(0, N_META), v_cols].astype(BF16)],
            [None]).astype(BF16)

    sub = lax.broadcasted_iota(jnp.int32, (pw, pw), 0)
    lane = lax.broadcasted_iota(jnp.int32, (pw, pw), 1)

    def row_start(r):
        return jnp.clip(r - NA_WIN_ROWS // 2, 0, n_grid_rows - NA_WIN_ROWS)

    def stage_scores(r, slot):
        rs = row_start(r)
        q0 = pl.multiple_of(N_META + r * GRID_W, 8)
        k0 = pl.multiple_of(N_META + rs * GRID_W, 8)
        for p in pairs:
            bd = _head_pair_block_diag(q_ref[p, pl.ds(q0, GRID_W), :])
            bias = bias_s_ref[p, pl.ds(NA_WIN_ROWS - 1 - (r - rs), NA_WIN_ROWS)].reshape(band, pw)
            s_ref[p, slot, pl.ds(0, band), :] = _dot_nt(k_ref[p, pl.ds(k0, band), :], bd) + bias
            s_ref[p, slot, pl.ds(band, N_META), :] = _dot_nt(k_meta[p], bd)

    def stage_softmax(slot):
        for p in pairs:
            s = s_ref[p, slot]
            e = jnp.exp(s - s.max(axis=0, keepdims=True))
            den_ref[p, slot] = e.sum(axis=0, keepdims=True)
            p_ref[p, slot] = e.astype(BF16)

    def stage_values(r, slot):
        rs = row_start(r)
        par = rs % 2
        l0 = pl.multiple_of((rs - par) * GRID_W, LANES)
        for p in pairs:
            o_t = (_dot(vt_ref[p, par, :, pl.ds(l0, band)], p_ref[p, slot, pl.ds(0, band), :])
                   + _dot(vt_meta[p], p_ref[p, slot, pl.ds(band, N_META), :]))
            ot_ref[p, r] = o_t / den_ref[p, slot]

    stage_scores(0, 0)
    stage_softmax(0)
    stage_scores(1, 1)

    def pipe_body(j, carry):
        i0 = 2 + 2 * j
        stage_values(i0 - 2, 0)
        stage_softmax(1)
        stage_scores(i0, 0)
        stage_values(i0 - 1, 1)
        stage_softmax(0)
        stage_scores(i0 + 1, 1)
        return carry

    lax.fori_loop(0, (n_grid_rows - 2) // 2, pipe_body, 0)
    stage_values(n_grid_rows - 2, 0)
    stage_softmax(1)
    stage_values(n_grid_rows - 1, 1)

    for p in pairs:
        for c in range(n_tblocks):
            oa, ob = ot_ref[p, 2 * c], ot_ref[p, 2 * c + 1]
            sel_a = jnp.where(sub < dh, oa, pltpu.roll(oa, dh, 1))
            sel_b = jnp.where(sub < dh, pltpu.roll(ob, dh, 1), ob)
            out_ref[0, pl.ds(N_META + c * LANES, LANES), p * pw:(p + 1) * pw] = (
                jnp.where(lane < dh, sel_a, sel_b).T.astype(BF16))


def _na_call(h, gn, w, qg, kg, bias):
    bsz, n_rows, d = h.shape
    pw = 2 * NA_HEAD_DIM
    gw = NA_PAIRS * pw
    n_steps = w.shape[1] // (3 * gw)
    nw = 3 * gw
    band = NA_WIN_ROWS * GRID_W
    n_real = n_rows - N_META
    heads_per_step = 2 * NA_PAIRS
    qkg = jnp.concatenate([jnp.tile(qg, heads_per_step) * NA_HEAD_DIM ** -0.5,
                           jnp.tile(kg, heads_per_step)])[None, :].astype(F32)
    group_spec = lambda g: pl.BlockSpec((d, gw), functools.partial(lambda g, b, s: (0, g * n_steps + s), g))
    return pl.pallas_call(
        _na_kernel,
        grid=(bsz, n_steps),
        in_specs=[
            pl.BlockSpec((1, n_rows, d), lambda b, s: (b, 0, 0)),
            pl.BlockSpec((1, d), lambda b, s: (0, 0)),
            group_spec(0), group_spec(1), group_spec(2),
            pl.BlockSpec((1, 2 * gw), lambda b, s: (0, 0)),
            pl.BlockSpec((heads_per_step,) + bias.shape[1:], lambda b, s: (s, 0, 0, 0)),
        ],
        out_specs=pl.BlockSpec((1, n_rows, NA_PAIRS * pw), lambda b, s: (b, 0, s)),
        out_shape=jax.ShapeDtypeStruct((bsz, n_rows, n_steps * NA_PAIRS * pw), BF16),
        scratch_shapes=[
            pltpu.VMEM((n_rows, d), BF16),
            pltpu.VMEM((n_rows, nw), F32),
            pltpu.VMEM((NA_PAIRS, n_rows, pw), BF16),
            pltpu.VMEM((NA_PAIRS, n_rows, pw), BF16),
            pltpu.VMEM((NA_PAIRS, 2, pw, n_real), BF16),
            pltpu.VMEM((NA_PAIRS, n_real // GRID_W, pw, pw), F32),
            pltpu.VMEM((NA_PAIRS, 2, band + N_META, pw), F32),
            pltpu.VMEM((NA_PAIRS, 2, band + N_META, pw), BF16),
            pltpu.VMEM((NA_PAIRS, 2, 1, pw), F32),
            pltpu.VMEM((d, nw), BF16),
            pltpu.VMEM((NA_PAIRS, bias.shape[1], GRID_W, pw), F32),
        ],
        compiler_params=_cparams(2),
        name="neighbourhood_attention",
    )(h, gn, w, w, w, qkg, bias)


def _lane_prefix_exclusive(x, tri_strict):
    n_blocks = x.shape[1] // LANES
    carry = jnp.zeros((x.shape[0], 1), F32)
    pieces = []
    for j in range(n_blocks):
        blk = x[:, j * LANES:(j + 1) * LANES]
        pieces.append(_dot(blk.astype(BF16), tri_strict) + carry)
        carry = carry + jnp.sum(blk, axis=-1, keepdims=True)
    return jnp.concatenate(pieces, axis=1)


def _aligned_row_blocks(n_rows, step=512):
    blocks, r0 = [], 0
    while r0 < n_rows:
        nr = min(step, n_rows - r0)
        blocks.append((r0, nr))
        r0 += nr
    return blocks


def _outproj_kernel(n_mix, *refs):
    h_ref = refs[0]
    mix_refs = refs[1:1 + n_mix]
    wo_ref, fg_ref, wr_ref, hn_ref, u2_ref, lg_ref = refs[1 + n_mix:7 + n_mix]
    acc = h_ref[0]
    k0 = 0
    for m_ref in mix_refs:
        kw = m_ref.shape[2]
        acc = acc + _dot(m_ref[0], wo_ref[pl.ds(k0, kw), :])
        k0 += kw
    hn_ref[0] = acc
    u = _rms(acc, fg_ref[...])
    u_hi = u.astype(BF16)
    u2_ref[0] = u_hi
    u_lo = (u - u_hi.astype(F32)).astype(BF16)
    hh_hl = _dot(u_hi, wr_ref[...])
    lg_ref[0] = hh_hl[:, 0:LANES] + hh_hl[:, LANES:2 * LANES] + _dot(u_lo, wr_ref[:, 0:LANES])


def _outproj_call(h, mixes, wo, fg, wr):
    bsz, n_rows, d = h.shape
    n_mix = len(mixes)
    nr = _row_blocks(n_rows)[0][1]
    row_spec = lambda w: pl.BlockSpec((1, nr, w), lambda b, r: (b, r, 0))
    return pl.pallas_call(
        functools.partial(_outproj_kernel, n_mix),
        grid=(bsz, n_rows // nr),
        in_specs=[row_spec(d)] + [row_spec(m.shape[2]) for m in mixes] + [
            pl.BlockSpec((d, d), lambda b, r: (0, 0)),
            pl.BlockSpec((1, d), lambda b, r: (0, 0)),
            pl.BlockSpec((d, 2 * LANES), lambda b, r: (0, 0)),
        ],
        out_specs=[row_spec(d), row_spec(d), row_spec(LANES)],
        out_shape=[
            jax.ShapeDtypeStruct((bsz, n_rows, d), F32),
            jax.ShapeDtypeStruct((bsz, n_rows, d), BF16),
            jax.ShapeDtypeStruct((bsz, n_rows, LANES), F32),
        ],
        compiler_params=_cparams(2),
        name="outproj_norm_logits",
    )(h, *mixes, wo, fg, wr)


def _route_kernel(cap, lg_ref, slot_ref, gate_ref, lgc_ref):
    n_rows = lg_ref.shape[1]
    n_pad = slot_ref.shape[2]
    lgc_ref[pl.ds(0, n_rows), :] = lg_ref[0]
    lgc_ref[pl.ds(n_rows, n_pad - n_rows), :] = jnp.zeros((n_pad - n_rows, LANES), F32)

    logits = jnp.concatenate(
        [lgc_ref[pl.ds(c * LANES, LANES), :].T[0:N_EXPERTS, :] for c in range(n_pad // LANES)], axis=1)
    ex = jnp.exp(logits - jnp.max(logits, axis=0, keepdims=True))
    aff = ex / jnp.sum(ex, axis=0, keepdims=True)
    tok = lax.broadcasted_iota(jnp.int32, aff.shape, 1)
    aff = jnp.where(tok < n_rows, aff, -1.0)

    def count_ge(x):
        return jnp.sum(jnp.where(aff >= x, 1.0, 0.0), axis=-1, keepdims=True)

    capf = float(cap)

    def refine(base, cands):
        best = base
        for cand in cands:
            best = jnp.where(count_ge(cand) >= capf, cand, best)
        return best

    tiny = jnp.full((aff.shape[0], 1), 2.0 ** -126, F32)
    ok0 = count_ge(tiny) >= capf
    p = tiny
    p = refine(p, [p * (2.0 ** (16 * k)) for k in range(1, 8)])
    p = refine(p, [p * (2.0 ** (2 * k)) for k in range(1, 8)])
    p = refine(p, [p * 2.0])
    m = p
    for i in range(1, 8):
        step = p * (2.0 ** (-3 * i))
        m = refine(m, [m + k * step for k in range(1, 8)])
    step = p * (2.0 ** -23)
    m = refine(m, [m + k * step for k in range(1, 4)])
    thr = jnp.where(ok0, m, 0.0)

    gt = aff > thr
    eq = aff == thr
    need = capf - jnp.sum(jnp.where(gt, 1.0, 0.0), axis=-1, keepdims=True)
    r = lax.broadcasted_iota(jnp.int32, (LANES, LANES), 0)
    c = lax.broadcasted_iota(jnp.int32, (LANES, LANES), 1)
    tri_strict = jnp.where(r < c, 1.0, 0.0).astype(BF16)
    eq_rank = _lane_prefix_exclusive(jnp.where(eq, 1.0, 0.0), tri_strict)
    sel = gt | (eq & (eq_rank < need))
    pos = _lane_prefix_exclusive(jnp.where(sel, 1.0, 0.0), tri_strict)
    slot_ref[0] = jnp.where(sel, pos, -1.0).astype(jnp.int32)
    gate_ref[0] = jnp.where(sel, aff, 0.0)


def _route_call(lg, cap):
    bsz, n_rows, _ = lg.shape
    n_pad = ((n_rows + LANES - 1) // LANES) * LANES
    return pl.pallas_call(
        functools.partial(_route_kernel, cap),
        grid=(bsz,),
        in_specs=[pl.BlockSpec((1, n_rows, LANES), lambda b: (b, 0, 0))],
        out_specs=[
            pl.BlockSpec((1, N_EXPERTS, n_pad), lambda b: (b, 0, 0)),
            pl.BlockSpec((1, N_EXPERTS, n_pad), lambda b: (b, 0, 0)),
        ],
        out_shape=[
            jax.ShapeDtypeStruct((bsz, N_EXPERTS, n_pad), jnp.int32),
            jax.ShapeDtypeStruct((bsz, N_EXPERTS, n_pad), F32),
        ],
        scratch_shapes=[pltpu.VMEM((n_pad, LANES), F32)],
        compiler_params=_cparams(1),
        name="expert_choice_router",
    )(lg)


def _gather_kernel(u2_ref, slot_ref, xs_ref):
    cap_pad = xs_ref.shape[2]
    n_rows = u2_ref.shape[1]
    srow_id = lax.broadcasted_iota(jnp.int32, (cap_pad, n_rows), 0)

    def body(i, carry):
        pieces = [jnp.where(srow_id == slot_ref[0, pl.ds(i * GATHER_GROUP + j, 1), 0:n_rows],
                            1.0, 0.0).astype(BF16)
                  for j in range(GATHER_GROUP)]
        rows = _dot(jnp.concatenate(pieces, axis=0), u2_ref[0]).astype(BF16)
        for j in range(GATHER_GROUP):
            xs_ref[0, i * GATHER_GROUP + j] = rows[j * cap_pad:(j + 1) * cap_pad, :]
        return carry

    lax.fori_loop(0, N_EXPERTS // GATHER_GROUP, body, 0)


def _gather_call(u2, slot, cap_pad):
    bsz, n_rows, d = u2.shape
    n_pad = slot.shape[2]
    return pl.pallas_call(
        _gather_kernel,
        grid=(bsz,),
        in_specs=[
            pl.BlockSpec((1, n_rows, d), lambda b: (b, 0, 0)),
            pl.BlockSpec((1, N_EXPERTS, n_pad), lambda b: (b, 0, 0)),
        ],
        out_specs=pl.BlockSpec((1, N_EXPERTS, cap_pad, d), lambda b: (b, 0, 0, 0)),
        out_shape=jax.ShapeDtypeStruct((bsz, N_EXPERTS, cap_pad, d), BF16),
        compiler_params=_cparams(1),
        name="expert_gather",
    )(u2, slot)


def _expert_kernel(xs_ref, wg_ref, wu_ref, wd_ref, ys_ref, wgb_ref, wub_ref, wdb_ref):
    bb, _, cap_pad, d = xs_ref.shape

    @pl.when(pl.program_id(1) == 0)
    def _():
        wgb_ref[...] = wg_ref[0].astype(BF16)
        wub_ref[...] = wu_ref[0].astype(BF16)
        wdb_ref[...] = wd_ref[0].astype(BF16)

    x = xs_ref[...].reshape(bb * cap_pad, d)
    g = _dot(x, wgb_ref[...])
    u = _dot(x, wub_ref[...])
    hdn = (g * _sigmoid(g) * u).astype(BF16)
    ys_ref[...] = _dot(hdn, wdb_ref[...]).astype(BF16).reshape(ys_ref.shape)


def _expert_call(xs, wg, wu, wd, layer):
    bsz, n_e, cap_pad, d = xs.shape
    ff = wg.shape[-1]
    seq_blocks = 2 if bsz % 2 == 0 else 1
    bb = bsz // seq_blocks
    return pl.pallas_call(
        _expert_kernel,
        grid=(n_e, seq_blocks),
        in_specs=[
            pl.BlockSpec((bb, 1, cap_pad, d), lambda e, m: (m, e, 0, 0)),
            pl.BlockSpec((1, None, d, ff), lambda e, m: (layer, e, 0, 0)),
            pl.BlockSpec((1, None, d, ff), lambda e, m: (layer, e, 0, 0)),
            pl.BlockSpec((1, None, ff, d), lambda e, m: (layer, e, 0, 0)),
        ],
        out_specs=pl.BlockSpec((bb, 1, cap_pad, d), lambda e, m: (m, e, 0, 0)),
        out_shape=jax.ShapeDtypeStruct((bsz, n_e, cap_pad, d), BF16),
        scratch_shapes=[pltpu.VMEM((d, ff), BF16), pltpu.VMEM((d, ff), BF16), pltpu.VMEM((ff, d), BF16)],
        compiler_params=_cparams(2),
        name="expert_swiglu",
    )(xs, wg, wu, wd)


def _combine_kernel(h_ref, ym_ref, yt_ref, slot_ref, gate_ref, out_ref):
    n_rows = h_ref.shape[1]
    n_out = out_ref.shape[1]
    skip = n_rows - n_out
    _, e_grp, n_main, d = ym_ref.shape
    _, n_e, n_tail, _ = yt_ref.shape
    n_pad = slot_ref.shape[2]
    g = pl.program_id(1)

    @pl.when(g == 0)
    def _():
        out_ref[0] = h_ref[0, pl.ds(skip, n_out), :]

    def weights_for(e, first_slot, n_slots):
        srow_id = lax.broadcasted_iota(jnp.int32, (n_slots, n_pad), 0) + first_slot
        hit = srow_id == slot_ref[0, pl.ds(e, 1), :]
        return jnp.where(hit, gate_ref[0, pl.ds(e, 1), :], 0.0).astype(BF16)

    def accumulate(weights, ys):
        for t0, tn in _aligned_row_blocks(n_pad, 1024):
            lo, hi = max(t0, skip), min(t0 + tn, n_rows)
            if hi <= lo:
                continue
            part = _dot_tn(weights[:, t0:t0 + tn], ys)
            out_ref[0, pl.ds(lo - skip, hi - lo), :] += part[lo - t0:hi - t0, :]

    @pl.when(g == 0)
    def _():
        weights = jnp.concatenate([weights_for(e, n_main, n_tail) for e in range(n_e)], axis=0)
        accumulate(weights, yt_ref[0].reshape(n_e * n_tail, d))

    @pl.when(g > 0)
    def _():
        weights = jnp.concatenate([weights_for((g - 1) * e_grp + j, 0, n_main) for j in range(e_grp)], axis=0)
        accumulate(weights, ym_ref[0].reshape(e_grp * n_main, d))


def _combine_call(h, ys, slot, gate, n_out, e_grp=4):
    bsz, n_rows, d = h.shape
    _, n_e, cap_pad, _ = ys.shape
    n_pad = slot.shape[2]
    n_main_steps = n_e // e_grp
    n_tail = cap_pad - MXU_DEPTH
    return pl.pallas_call(
        _combine_kernel,
        grid=(bsz, n_main_steps + 1),
        in_specs=[
            pl.BlockSpec((1, n_rows, d), lambda b, g: (b, 0, 0)),
            pl.BlockSpec((1, e_grp, MXU_DEPTH, d), lambda b, g: (b, jnp.maximum(g - 1, 0), 0, 0)),
            pl.BlockSpec((1, n_e, n_tail, d), lambda b, g: (b, 0, MXU_DEPTH // n_tail, 0)),
            pl.BlockSpec((1, n_e, n_pad), lambda b, g: (b, 0, 0)),
            pl.BlockSpec((1, n_e, n_pad), lambda b, g: (b, 0, 0)),
        ],
        out_specs=pl.BlockSpec((1, n_out, d), lambda b, g: (b, 0, 0)),
        out_shape=jax.ShapeDtypeStruct((bsz, n_out, d), F32),
        compiler_params=_cparams(2),
        name="expert_combine",
    )(h, ys, ys, slot, gate)


def _even_weights(w_in, conv_w, gate_b, decay_logit):
    mw = N_HEADS * HEAD_DIM
    gates = w_in[:, 4 * mw:4 * mw + 4 * N_HEADS]
    r0 = 4 * mw + 4 * N_HEADS

    def gate_cols(fw_off, bw_off):
        cols = jnp.stack([gates[:, fw_off:fw_off + N_HEADS], gates[:, bw_off:bw_off + N_HEADS]], axis=-1)
        cols = cols.transpose(1, 0, 2)
        return jnp.pad(cols, ((0, 0), (0, 0), (0, LANES - 2)))

    def gate_bias(fw_off, bw_off):
        b = jnp.stack([gate_b[fw_off:fw_off + N_HEADS], gate_b[bw_off:bw_off + N_HEADS]], axis=-1)
        return jnp.pad(b, ((0, 0), (0, LANES - 2)))

    w_m = w_in[:, 0:4 * mw].astype(BF16)
    w_g = jnp.concatenate([gate_cols(0, 2 * N_HEADS), gate_cols(N_HEADS, 3 * N_HEADS)], axis=-1).astype(BF16)
    gb = jnp.concatenate([gate_bias(0, 2 * N_HEADS), gate_bias(N_HEADS, 3 * N_HEADS)], axis=-1)[:, None, :]
    conv = jnp.concatenate([conv_w[:, :mw].reshape(CONV_K, N_HEADS, HEAD_DIM),
                            conv_w[:, mw:].reshape(CONV_K, N_HEADS, HEAD_DIM)], axis=-1).transpose(1, 0, 2)
    w_r = w_in[:, r0:r0 + 4 * mw].astype(BF16)
    dl = jnp.pad(decay_logit.T, ((0, 0), (0, LANES - 2)))[:, None, :]
    return w_m, w_g, gb.astype(F32), conv.astype(F32), w_r, dl.astype(F32)


def _rotary_tables(n_rows):
    half = HEAD_DIM // 2
    inv = ROPE_BASE ** (-jnp.arange(half, dtype=F32) / half)
    ang = jnp.arange(n_rows, dtype=F32)[:, None] * inv[None, :]
    cos, sin = jnp.cos(ang), jnp.sin(ang)
    return jnp.concatenate([cos, cos], axis=-1), jnp.concatenate([-sin, sin], axis=-1)


def _na_bias_table(rpb):
    col = np.arange(GRID_W)
    col_start = np.clip(col - NA_WIN_COLS // 2, 0, GRID_W - NA_WIN_COLS)
    col_in = (col[None, :] >= col_start[:, None]) & (col[None, :] < col_start[:, None] + NA_WIN_COLS)
    dc_idx = np.clip(col[None, :] - col[:, None], -(NA_WIN_COLS - 1), NA_WIN_COLS - 1) + NA_WIN_COLS - 1
    selector = (np.arange(rpb.shape[2])[:, None, None] == dc_idx.T[None]).astype(np.float32)
    rpb_cols = jnp.einsum('hrd,dkq->hrkq', rpb.astype(F32), selector, precision=lax.Precision.HIGHEST)
    return jnp.where(col_in.T[None, None], rpb_cols, NEG)


def _ffn(h, mixes, wo, fg, wr, wg, wu, wd, layer, n_out):
    n_rows = h.shape[1]
    cap = CAP_FACTOR * n_rows // N_EXPERTS
    cap_pad = ((cap + BF16_ROWS - 1) // BF16_ROWS) * BF16_ROWS
    wr_hi = wr.astype(BF16)
    wr_lo = (wr - wr_hi.astype(F32)).astype(BF16)
    pad = ((0, 0), (0, LANES - N_EXPERTS))
    wr_pieces = jnp.concatenate([jnp.pad(wr_hi, pad), jnp.pad(wr_lo, pad)], axis=1)
    hn, u2, lg = _outproj_call(h, mixes, wo.astype(BF16), fg[None, :], wr_pieces)
    slot, gate = _route_call(lg, cap)
    xs = _gather_call(u2, slot, cap_pad)
    ys = _expert_call(xs, wg, wu, wd, layer)
    return _combine_call(hn, ys, slot, gate, n_out)


def kernel(x, meta_tokens, attn_norm_g, ffn_norm_g, even_w_in, even_conv_w, even_gate_b, even_m_norm_g, even_ret_decay_logit, even_r_norm_g, even_w_out, odd_w_in, odd_q_norm_g, odd_k_norm_g, odd_rpb, odd_w_out, router_w, expert_w_gate, expert_w_up, expert_w_down):
    bsz = x.shape[0]
    depth = attn_norm_g.shape[0]
    meta = jnp.broadcast_to(meta_tokens.astype(x.dtype)[None], (bsz,) + meta_tokens.shape)
    h = jnp.concatenate([meta, x], axis=1)
    n_rows = h.shape[1]
    cos2, sin2 = _rotary_tables(n_rows)
    for layer in range(depth):
        j = layer // 2
        gn = attn_norm_g[layer][None, :]
        if layer % 2 == 0:
            w_m, w_g, gb, conv, w_r, dl = _even_weights(even_w_in[j], even_conv_w[j], even_gate_b[j],
                                                        even_ret_decay_logit[j])
            m_out = _mlstm_call(h, gn, w_m, w_g, conv, gb, even_m_norm_g[j].reshape(N_HEADS, 1, HEAD_DIM))
            r_out = _ret_call(h, gn, w_r, cos2, sin2, dl, even_r_norm_g[j].reshape(N_HEADS, 1, HEAD_DIM))
            mixes, wo = [m_out, r_out], even_w_out[j]
        else:
            a_out = _na_call(h, gn, odd_w_in[j].astype(BF16), odd_q_norm_g[j], odd_k_norm_g[j],
                             _na_bias_table(odd_rpb[j]))
            mixes, wo = [a_out], odd_w_out[j]
        n_out = n_rows - N_META if layer == depth - 1 else n_rows
        h = _ffn(h, mixes, wo, ffn_norm_g[layer], router_w[layer],
                 expert_w_gate, expert_w_up, expert_w_down, layer, n_out)
    return h
```

```python
import functools

import jax
import jax.numpy as jnp
import numpy as np
from jax import lax
from jax.experimental import pallas as pl
from jax.experimental.pallas import tpu as pltpu

F32 = jnp.float32
BF16 = jnp.bfloat16

LANES = 128
BF16_ROWS = 16
MXU_DEPTH = 256
N_META = 16
GRID_W = 64
EPS = 1e-6
HEAD_DIM = 128
N_HEADS = 4
M_HPS = 2
CONV_K = 5
CONV_PAD = 8
ROPE_BASE = 10000.0
NA_HEAD_DIM = 64
NA_WIN_ROWS = 8
NA_WIN_COLS = 16
NA_PAIRS = 2
N_EXPERTS = 16
CAP_FACTOR = 2
GATHER_GROUP = 2
NEG = -1e30
SCAN_T = 256
VMEM_LIMIT = 56 * 1024 * 1024


def _cparams(n_axes):
    return pltpu.CompilerParams(
        dimension_semantics=("arbitrary",) * n_axes, vmem_limit_bytes=VMEM_LIMIT)


def _row_blocks(n_rows):
    for nb in (3, 2, 4, 6, 1):
        if n_rows % (8 * nb) == 0:
            step = n_rows // nb
            return [(i * step, step) for i in range(nb)]
    return [(0, n_rows)]


def _rms(x, g):
    return x * lax.rsqrt(jnp.mean(x * x, axis=-1, keepdims=True) + EPS) * g


def _sigmoid(x):
    return 1.0 / (1.0 + jnp.exp(-x))


def _log_sigmoid(x):
    return jnp.minimum(x, 0.0) - jnp.log(1.0 + jnp.exp(-jnp.abs(x)))


def _dot(a, b):
    return jnp.dot(a, b, preferred_element_type=F32)


def _dot_nt(a, b):
    return lax.dot_general(a, b, (((1,), (1,)), ((), ())), preferred_element_type=F32)


def _dot_tn(a, b):
    return lax.dot_general(a, b, (((0,), (0,)), ((), ())), preferred_element_type=F32)


def _split3(x):
    hi = x.astype(BF16)
    r1 = x - hi.astype(F32)
    mid = r1.astype(BF16)
    lo = (r1 - mid.astype(F32)).astype(BF16)
    return hi, mid, lo


def _tri_prefix(tri_bf16, x):
    hi, mid, lo = _split3(x)
    return _dot(tri_bf16, hi) + _dot(tri_bf16, mid) + _dot(tri_bf16, lo)


def _norm_to_scratch(h_ref, gn_ref, un_ref):
    n_rows = h_ref.shape[1]
    for r0, nr in _row_blocks(n_rows):
        x = h_ref[0, pl.ds(r0, nr), :]
        un_ref[pl.ds(r0, nr), :] = _rms(x, gn_ref[...]).astype(BF16)


def _assemble_weights(w_refs, wcat_ref):
    c0 = 0
    for w_ref in w_refs:
        w = w_ref[...] if len(w_ref.shape) == 2 else w_ref[0]
        wcat_ref[:, c0:c0 + w.shape[1]] = w
        c0 += w.shape[1]


def _project(un_ref, wcat_ref, z_ref):
    n_rows = un_ref.shape[0]
    for r0, nr in _row_blocks(n_rows):
        z_ref[pl.ds(r0, nr), :] = _dot(un_ref[pl.ds(r0, nr), :], wcat_ref[...])


def _tri_mask(n, lower):
    r = lax.broadcasted_iota(jnp.int32, (n, n), 0)
    c = lax.broadcasted_iota(jnp.int32, (n, n), 1)
    return (c <= r) if lower else (c >= r)


def _mlstm_chunk(qc, kt, v_aug, a_row, f_col, mask, state):
    c_aug, g_prev = state
    qb = qc.astype(BF16)
    cm = jnp.max(jnp.where(mask, a_row, NEG), axis=-1, keepdims=True)
    g_col = jnp.maximum(g_prev, cm)
    dm = jnp.exp(jnp.where(mask, a_row - g_col, NEG))
    s = _dot(qb, kt.astype(BF16)) * dm
    w_inter = jnp.exp(g_prev - g_col)
    tot = w_inter * _dot(qb, c_aug.astype(BF16)) + _dot(s.astype(BF16), v_aug)
    den = tot[:, HEAD_DIM:2 * HEAD_DIM][:, 0:1]
    out = tot[:, 0:HEAD_DIM] / jnp.maximum(jnp.abs(den), jnp.exp(-(f_col + g_col)))
    g_end = jnp.maximum(g_prev, jnp.max(a_row, axis=-1, keepdims=True))
    ktw = (kt * jnp.exp(a_row - g_end)).astype(BF16)
    c_new = jnp.exp(g_prev - g_end) * c_aug + _dot(ktw, v_aug)
    return out, (c_new, g_end)


def _mlstm_kernel(h_ref, gn_ref, wq_ref, wk_ref, wv_ref, wo_ref, wg_ref, conv_ref, gb_ref, ng_ref, out_ref,
                  un_ref, z_ref, zc_ref, acc_ref, acol_ref, fcol_ref, atm_ref, atr_ref, kt_ref, vaug_ref,
                  wcat_ref):
    n_rows = h_ref.shape[1]
    n_real = n_rows - N_META
    t = SCAN_T
    n_chunks = n_real // t
    hw = M_HPS * HEAD_DIM
    heads = range(M_HPS)

    @pl.when(pl.program_id(1) == 0)
    def _():
        _norm_to_scratch(h_ref, gn_ref, un_ref)

    _assemble_weights([wq_ref, wk_ref, wv_ref, wo_ref, wg_ref], wcat_ref)
    _project(un_ref, wcat_ref, z_ref)

    zc_ref[pl.ds(0, CONV_PAD), :] = jnp.zeros((CONV_PAD, 2 * hw), F32)
    zc_ref[pl.ds(CONV_PAD + n_rows, CONV_PAD), :] = jnp.zeros((CONV_PAD, 2 * hw), F32)
    for r0, nr in _row_blocks(n_rows):
        zc_ref[pl.ds(CONV_PAD + r0, nr), :] = z_ref[pl.ds(r0, nr), 0:2 * hw]
    lane2 = lax.broadcasted_iota(jnp.int32, (1, 2 * hw), 1)
    qk_scale = jnp.where(lane2 >= hw, HEAD_DIM ** -0.5, 1.0).astype(F32)
    for r0, nr in _row_blocks(n_rows):
        acc = jnp.zeros((nr, 2 * hw), F32)
        for j in range(CONV_K):
            off = CONV_PAD - (CONV_K - 1) // 2 + j + r0
            acc = acc + zc_ref[pl.ds(off, nr), :] * conv_ref[0, pl.ds(j, 1), :]
        z_ref[pl.ds(r0, nr), 0:2 * hw] = acc * _sigmoid(acc) * qk_scale

    gi_off, gf_off = 4 * hw, 4 * hw + LANES
    bias_i = gb_ref[0, :, 0:LANES]
    bias_f = gb_ref[0, :, LANES:2 * LANES]
    lane = lax.broadcasted_iota(jnp.int32, (1, LANES), 1)
    tri_m = jnp.where(_tri_mask(N_META, True), 1.0, 0.0).astype(BF16)
    lf_m = _log_sigmoid(z_ref[pl.ds(0, N_META), gf_off:gf_off + LANES] + bias_f)
    f_meta = _tri_prefix(tri_m, lf_m)
    f_meta_end = f_meta[N_META - 1:N_META, :]
    tri_b = jnp.where(_tri_mask(LANES, True), 1.0, 0.0).astype(BF16)

    def prefix_body(c, carry):
        r0 = pl.multiple_of(N_META + c * LANES, 8)
        lf = _log_sigmoid(z_ref[pl.ds(r0, LANES), gf_off:gf_off + LANES] + bias_f)
        p = _tri_prefix(tri_b, lf) + carry
        fcol_ref[pl.ds(r0, LANES), :] = p
        return p[LANES - 1:LANES, :]

    total = lax.fori_loop(0, n_real // LANES, prefix_body, jnp.zeros((1, LANES), F32), unroll=4)

    fcol_ref[pl.ds(0, N_META), :] = f_meta
    acol_ref[pl.ds(0, N_META), :] = z_ref[pl.ds(0, N_META), gi_off:gi_off + LANES] + bias_i - f_meta

    def finish_body(c, carry):
        r0 = pl.multiple_of(N_META + c * LANES, 8)
        lf = _log_sigmoid(z_ref[pl.ds(r0, LANES), gf_off:gf_off + LANES] + bias_f)
        p = fcol_ref[pl.ds(r0, LANES), :]
        f = f_meta_end + jnp.where(lane % 2 == 0, p, total - p + lf)
        fcol_ref[pl.ds(r0, LANES), :] = f
        a = z_ref[pl.ds(r0, LANES), gi_off:gi_off + LANES] + bias_i - f
        acol_ref[pl.ds(r0, LANES), :] = a
        l0 = pl.multiple_of(c * LANES, LANES)
        atr_ref[:, pl.ds(l0, LANES)] = a.T[0:8, :]
        for hh in heads:
            kt_ref[hh, :, pl.ds(l0, LANES)] = z_ref[pl.ds(r0, LANES), hw + hh * HEAD_DIM:hw + (hh + 1) * HEAD_DIM].T
            vaug_ref[hh, pl.ds(r0, LANES), :] = jnp.concatenate(
                [z_ref[pl.ds(r0, LANES), 2 * hw + hh * HEAD_DIM:2 * hw + (hh + 1) * HEAD_DIM], ones_col],
                axis=1).astype(BF16)
        return carry

    ones_col = jnp.where(lax.broadcasted_iota(jnp.int32, (LANES, LANES), 1) == 0, 1.0, 0.0)
    ones_col_m = jnp.where(lax.broadcasted_iota(jnp.int32, (N_META, LANES), 1) == 0, 1.0, 0.0)
    lax.fori_loop(0, n_real // LANES, finish_body, 0, unroll=4)
    atm_ref[...] = acol_ref[pl.ds(0, LANES), :].T[0:8, :]

    chains = [(hh, d) for hh in heads for d in (0, 1)]
    mask_m = _tri_mask(N_META, True)
    states = []
    for hh, d in chains:
        g_lane = 2 * hh + d
        kt_meta = z_ref[pl.ds(0, LANES), hw + hh * HEAD_DIM:hw + (hh + 1) * HEAD_DIM].T[:, 0:N_META]
        vaug_meta = jnp.concatenate(
            [z_ref[pl.ds(0, N_META), 2 * hw + hh * HEAD_DIM:2 * hw + (hh + 1) * HEAD_DIM], ones_col_m],
            axis=1).astype(BF16)
        state = (jnp.zeros((HEAD_DIM, 2 * HEAD_DIM), F32), jnp.zeros((1, 1), F32))
        out_m, state = _mlstm_chunk(
            z_ref[pl.ds(0, N_META), hh * HEAD_DIM:(hh + 1) * HEAD_DIM], kt_meta, vaug_meta,
            atm_ref[pl.ds(g_lane, 1), 0:N_META], fcol_ref[pl.ds(0, N_META), g_lane:g_lane + 1], mask_m, state)
        acc_ref[hh, d, pl.ds(0, N_META), :] = out_m
        states.append(state)

    def body(i, carry):
        new = []
        for idx, (hh, d) in enumerate(chains):
            g_lane = 2 * hh + d
            c = i if d == 0 else n_chunks - 1 - i
            r0 = pl.multiple_of(N_META + c * t, 8)
            l0 = pl.multiple_of(c * t, LANES)
            out, st = _mlstm_chunk(
                z_ref[pl.ds(r0, t), hh * HEAD_DIM:(hh + 1) * HEAD_DIM], kt_ref[hh, :, pl.ds(l0, t)],
                vaug_ref[hh, pl.ds(r0, t), :], atr_ref[pl.ds(g_lane, 1), pl.ds(l0, t)],
                fcol_ref[pl.ds(r0, t), g_lane:g_lane + 1], _tri_mask(t, d == 0), carry[idx])
            acc_ref[hh, d, pl.ds(r0, t), :] = out
            new.append(st)
        return tuple(new)

    lax.fori_loop(0, n_chunks, body, tuple(states))

    for hh in heads:
        cols = slice(hh * HEAD_DIM, (hh + 1) * HEAD_DIM)
        for r0, nr in _row_blocks(n_rows):
            hs = acc_ref[hh, 0, pl.ds(r0, nr), :] + acc_ref[hh, 1, pl.ds(r0, nr), :]
            o = z_ref[pl.ds(r0, nr), 3 * hw + hh * HEAD_DIM:3 * hw + (hh + 1) * HEAD_DIM]
            out_ref[0, pl.ds(r0, nr), cols] = (_sigmoid(o) * _rms(hs, ng_ref[0][:, cols])).astype(BF16)


def _head_column_specs(d, n_groups, heads_per_step=1):
    steps = N_HEADS // heads_per_step
    return [pl.BlockSpec((d, heads_per_step * HEAD_DIM), functools.partial(lambda g, b, s: (0, g * steps + s), g))
            for g in range(n_groups)]


def _mlstm_call(h, gn, w, wg, conv, gb, ng):
    bsz, n_rows, d = h.shape
    hw = M_HPS * HEAD_DIM
    nw = 4 * hw + wg.shape[-1]
    n_real = n_rows - N_META
    return pl.pallas_call(
        _mlstm_kernel,
        grid=(bsz, N_HEADS // M_HPS),
        in_specs=[
            pl.BlockSpec((1, n_rows, d), lambda b, s: (b, 0, 0)),
            pl.BlockSpec((1, d), lambda b, s: (0, 0)),
            *_head_column_specs(d, 4, M_HPS),
            pl.BlockSpec((1, d, wg.shape[-1]), lambda b, s: (s, 0, 0)),
            pl.BlockSpec((1, CONV_K, 2 * hw), lambda b, s: (s, 0, 0)),
            pl.BlockSpec((1, 1, 2 * LANES), lambda b, s: (s, 0, 0)),
            pl.BlockSpec((1, 1, hw), lambda b, s: (s, 0, 0)),
        ],
        out_specs=pl.BlockSpec((1, n_rows, hw), lambda b, s: (b, 0, s)),
        out_shape=jax.ShapeDtypeStruct((bsz, n_rows, N_HEADS * HEAD_DIM), BF16),
        scratch_shapes=[
            pltpu.VMEM((n_rows, d), BF16),
            pltpu.VMEM((n_rows, nw), F32),
            pltpu.VMEM((n_rows + 2 * CONV_PAD, 2 * hw), F32),
            pltpu.VMEM((M_HPS, 2, n_rows, HEAD_DIM), F32),
            pltpu.VMEM((n_rows, LANES), F32),
            pltpu.VMEM((n_rows, LANES), F32),
            pltpu.VMEM((8, LANES), F32),
            pltpu.VMEM((8, n_real), F32),
            pltpu.VMEM((M_HPS, HEAD_DIM, n_real), F32),
            pltpu.VMEM((M_HPS, n_rows, 2 * HEAD_DIM), BF16),
            pltpu.VMEM((d, nw), BF16),
        ],
        compiler_params=_cparams(2),
        name="mlstm_heads",
    )(h, gn, w, w, w, w, wg, conv, gb, ng)


def _ret_chunk(qc, kt, vc, dmat, dq, dk_row, dchunk, r_st):
    qb, vb = qc.astype(BF16), vc.astype(BF16)
    s = _dot(qb, kt.astype(BF16)) * dmat
    out = _dot(s.astype(BF16), vb) + dq * _dot(qb, r_st.astype(BF16))
    r_new = dchunk * r_st + _dot((kt * dk_row).astype(BF16), vb)
    return out, r_new


def _decay_tables(n, lg, forward):
    r = lax.broadcasted_iota(jnp.int32, (n, n), 0)
    c = lax.broadcasted_iota(jnp.int32, (n, n), 1)
    pos_q = lax.broadcasted_iota(jnp.int32, (n, 1), 0).astype(F32)
    pos_k = lax.broadcasted_iota(jnp.int32, (1, n), 1).astype(F32)
    if forward:
        dist, mask = (r - c).astype(F32), c <= r
        dq, dk = jnp.exp((pos_q + 1.0) * lg), jnp.exp((n - 1.0 - pos_k) * lg)
    else:
        dist, mask = (c - r).astype(F32), c >= r
        dq, dk = jnp.exp((n - pos_q) * lg), jnp.exp(pos_k * lg)
    dmat = jnp.exp(jnp.where(mask, dist * lg, NEG))
    return dmat, dq, dk, jnp.exp(n * lg)


def _ret_kernel(h_ref, gn_ref, wq_ref, wk_ref, wv_ref, wg_ref, cos_ref, sin_ref, dl_ref, ng_ref, out_ref,
                un_ref, z_ref, acc_ref, dmat_ref, dq_ref, kt_ref, wcat_ref):
    n_rows = h_ref.shape[1]
    n_real = n_rows - N_META
    t = SCAN_T
    n_chunks = n_real // t

    @pl.when(pl.program_id(1) == 0)
    def _():
        _norm_to_scratch(h_ref, gn_ref, un_ref)

    _assemble_weights([wq_ref, wk_ref, wv_ref, wg_ref], wcat_ref)
    _project(un_ref, wcat_ref, z_ref)

    for r0, nr in _row_blocks(n_rows):
        cs, sn = cos_ref[pl.ds(r0, nr), :], sin_ref[pl.ds(r0, nr), :]
        q = z_ref[pl.ds(r0, nr), 0:HEAD_DIM]
        z_ref[pl.ds(r0, nr), 0:HEAD_DIM] = q * cs + pltpu.roll(q, HEAD_DIM // 2, 1) * sn
        k = z_ref[pl.ds(r0, nr), HEAD_DIM:2 * HEAD_DIM]
        z_ref[pl.ds(r0, nr), HEAD_DIM:2 * HEAD_DIM] = (
            (k * cs + pltpu.roll(k, HEAD_DIM // 2, 1) * sn) * HEAD_DIM ** -0.5)

    for c in range(n_real // LANES):
        kt_ref[:, c * LANES:(c + 1) * LANES] = z_ref[pl.ds(N_META + c * LANES, LANES), HEAD_DIM:2 * HEAD_DIM].T
    kt_meta = z_ref[pl.ds(0, LANES), HEAD_DIM:2 * HEAD_DIM].T[:, 0:N_META]

    lg_all = _log_sigmoid(dl_ref[0])
    states, dk_rows, dchunks = [], [], []
    for d in (0, 1):
        lg = lg_all[:, d:d + 1]
        dmat_m, _, dk_m, dch_m = _decay_tables(N_META, lg, True)
        dmat, dq, dk, dch = _decay_tables(t, lg, d == 0)
        dmat_ref[d] = dmat
        dq_ref[d] = jnp.broadcast_to(dq, (t, LANES))
        dk_rows.append(dk)
        dchunks.append(dch)
        out_m, r_st = _ret_chunk(
            z_ref[pl.ds(0, N_META), 0:HEAD_DIM], kt_meta, z_ref[pl.ds(0, N_META), 2 * HEAD_DIM:3 * HEAD_DIM],
            dmat_m, jnp.zeros((N_META, 1), F32), dk_m, dch_m, jnp.zeros((HEAD_DIM, HEAD_DIM), F32))
        acc_ref[d, pl.ds(0, N_META), :] = out_m
        states.append(r_st)

    def body(i, carry):
        new = []
        for d in (0, 1):
            c = i if d == 0 else n_chunks - 1 - i
            r0 = pl.multiple_of(N_META + c * t, 8)
            l0 = pl.multiple_of(c * t, LANES)
            out, st = _ret_chunk(
                z_ref[pl.ds(r0, t), 0:HEAD_DIM], kt_ref[:, pl.ds(l0, t)],
                z_ref[pl.ds(r0, t), 2 * HEAD_DIM:3 * HEAD_DIM],
                dmat_ref[d], dq_ref[d], dk_rows[d], dchunks[d], carry[d])
            acc_ref[d, pl.ds(r0, t), :] = out
            new.append(st)
        return tuple(new)

    lax.fori_loop(0, n_chunks, body, tuple(states))

    for r0, nr in _row_blocks(n_rows):
        hs = acc_ref[0, pl.ds(r0, nr), :] + acc_ref[1, pl.ds(r0, nr), :]
        g = z_ref[pl.ds(r0, nr), 3 * HEAD_DIM:4 * HEAD_DIM]
        out_ref[0, pl.ds(r0, nr), :] = (g * _sigmoid(g) * _rms(hs, ng_ref[0])).astype(BF16)


def _ret_call(h, gn, w, cos2, sin2, dl, ng):
    bsz, n_rows, d = h.shape
    nw = 4 * HEAD_DIM
    return pl.pallas_call(
        _ret_kernel,
        grid=(bsz, N_HEADS),
        in_specs=[
            pl.BlockSpec((1, n_rows, d), lambda b, hd: (b, 0, 0)),
            pl.BlockSpec((1, d), lambda b, hd: (0, 0)),
            *_head_column_specs(d, 4),
            pl.BlockSpec((n_rows, HEAD_DIM), lambda b, hd: (0, 0)),
            pl.BlockSpec((n_rows, HEAD_DIM), lambda b, hd: (0, 0)),
            pl.BlockSpec((1, 1, LANES), lambda b, hd: (hd, 0, 0)),
            pl.BlockSpec((1, 1, HEAD_DIM), lambda b, hd: (hd, 0, 0)),
        ],
        out_specs=pl.BlockSpec((1, n_rows, HEAD_DIM), lambda b, hd: (b, 0, hd)),
        out_shape=jax.ShapeDtypeStruct((bsz, n_rows, N_HEADS * HEAD_DIM), BF16),
        scratch_shapes=[
            pltpu.VMEM((n_rows, d), BF16),
            pltpu.VMEM((n_rows, nw), F32),
            pltpu.VMEM((2, n_rows, HEAD_DIM), F32),
            pltpu.VMEM((2, SCAN_T, SCAN_T), F32),
            pltpu.VMEM((2, SCAN_T, LANES), F32),
            pltpu.VMEM((HEAD_DIM, n_rows - N_META), F32),
            pltpu.VMEM((d, nw), BF16),
        ],
        compiler_params=_cparams(2),
        name="retention_heads",
    )(h, gn, w, w, w, w, cos2, sin2, dl, ng)


def _head_pair_block_diag(qp):
    lane = lax.broadcasted_iota(jnp.int32, qp.shape, 1)
    zero = jnp.zeros_like(qp)
    return jnp.concatenate([jnp.where(lane < NA_HEAD_DIM, qp, zero),
                            jnp.where(lane >= NA_HEAD_DIM, qp, zero)], axis=0)


def _na_attend(qp, keys, values, bias_t):
    n = qp.shape[0]
    bd = _head_pair_block_diag(qp)
    scores = []
    for kb, bt in zip(keys, bias_t):
        s = _dot_nt(kb, bd)
        scores.append(s if bt is None else s + bt)
    m = scores[0].max(axis=0, keepdims=True)
    for s in scores[1:]:
        m = jnp.maximum(m, s.max(axis=0, keepdims=True))
    probs = [jnp.exp(s - m) for s in scores]
    den = probs[0].sum(axis=0, keepdims=True)
    for p in probs[1:]:
        den = den + p.sum(axis=0, keepdims=True)
    inv = 1.0 / den
    o2 = None
    for p, vb in zip(probs, values):
        part = _dot_tn((p * inv).astype(BF16), vb)
        o2 = part if o2 is None else o2 + part
    lane = lax.broadcasted_iota(jnp.int32, (n, 2 * NA_HEAD_DIM), 1)
    return jnp.where(lane < NA_HEAD_DIM, o2[0:n, :], o2[n:2 * n, :])


def _na_kernel(h_ref, gn_ref, wq_ref, wk_ref, wv_ref, qkg_ref, bias_ref, out_ref,
               un_ref, z_ref, q_ref, k_ref, vt_ref, ot_ref, s_ref, p_ref, den_ref, wcat_ref, bias_s_ref):
    n_rows = h_ref.shape[1]
    n_real = n_rows - N_META
    n_grid_rows = n_real // GRID_W
    dh = NA_HEAD_DIM
    pw = 2 * dh
    gw = NA_PAIRS * pw
    band = NA_WIN_ROWS * GRID_W
    pairs = range(NA_PAIRS)
    v_off = 2 * gw

    @pl.when(pl.program_id(1) == 0)
    def _():
        _norm_to_scratch(h_ref, gn_ref, un_ref)

    _assemble_weights([wq_ref, wk_ref, wv_ref], wcat_ref)
    _project(un_ref, wcat_ref, z_ref)

    for p in pairs:
        for dr in range(bias_ref.shape[1]):
            bias_s_ref[p, dr] = jnp.concatenate([bias_ref[2 * p, dr], bias_ref[2 * p + 1, dr]], axis=1)

    r = lax.broadcasted_iota(jnp.int32, (gw, gw), 0)
    c = lax.broadcasted_iota(jnp.int32, (gw, gw), 1)
    head_ones = jnp.where(r // dh == c // dh, 1.0, 0.0).astype(BF16)
    for g, dst_ref in enumerate((q_ref, k_ref)):
        for r0, nr in _row_blocks(n_rows):
            x = z_ref[pl.ds(r0, nr), g * gw:(g + 1) * gw]
            ssq = _dot((x * x).astype(BF16), head_ones)
            y = (x * lax.rsqrt(ssq * (1.0 / dh) + EPS) * qkg_ref[:, g * gw:(g + 1) * gw]).astype(BF16)
            for p in pairs:
                dst_ref[p, pl.ds(r0, nr), :] = y[:, p * pw:(p + 1) * pw]

    n_tblocks = n_real // LANES
    k_meta, vt_meta = [], []
    for p in pairs:
        v_cols = slice(v_off + p * pw, v_off + (p + 1) * pw)
        for cpy in (0, 1):
            for c in range(n_tblocks - cpy):
                vb = z_ref[pl.ds(N_META + cpy * GRID_W + c * LANES, LANES), v_cols]
                vt_ref[p, cpy, :, c * LANES:(c + 1) * LANES] = vb.T.astype(BF16)
        vt_ref[p, 1, :, (n_tblocks - 1) * LANES:n_tblocks * LANES] = jnp.zeros((pw, LANES), BF16)
        vt_meta.append(z_ref[pl.ds(0, LANES), v_cols].T[:, 0:N_META].astype(BF16))
        k_meta.append(k_ref[p, pl.ds(0, N_META), :])
        out_ref[0, pl.ds(0, N_META), p * pw:(p + 1) * pw] = _na_attend(
            q_ref[p, pl.ds(0, N_META), :], [k_meta[p]], [z_ref[pl.ds(0, N_META), v_cols].astype(BF16)],
            [None]).astype(BF16)

    sub = lax.broadcasted_iota(jnp.int32, (pw, pw), 0)
    lane = lax.broadcasted_iota(jnp.int32, (pw, pw), 1)

    def row_start(r):
        return jnp.clip(r - NA_WIN_ROWS // 2, 0, n_grid_rows - NA_WIN_ROWS)

    def stage_scores(r, slot):
        rs = row_start(r)
        q0 = pl.multiple_of(N_META + r * GRID_W, 8)
        k0 = pl.multiple_of(N_META + rs * GRID_W, 8)
        for p in pairs:
            bd = _head_pair_block_diag(q_ref[p, pl.ds(q0, GRID_W), :])
            bias = bias_s_ref[p, pl.ds(NA_WIN_ROWS - 1 - (r - rs), NA_WIN_ROWS)].reshape(band, pw)
            s_ref[p, slot, pl.ds(0, band), :] = _dot_nt(k_ref[p, pl.ds(k0, band), :], bd) + bias
            s_ref[p, slot, pl.ds(band, N_META), :] = _dot_nt(k_meta[p], bd)

    def stage_softmax(slot):
        for p in pairs:
            s = s_ref[p, slot]
            e = jnp.exp(s - s.max(axis=0, keepdims=True))
            den_ref[p, slot] = e.sum(axis=0, keepdims=True)
            p_ref[p, slot] = e.astype(BF16)

    def stage_values(r, slot):
        rs = row_start(r)
        par = rs % 2
        l0 = pl.multiple_of((rs - par) * GRID_W, LANES)
        for p in pairs:
            o_t = (_dot(vt_ref[p, par, :, pl.ds(l0, band)], p_ref[p, slot, pl.ds(0, band), :])
                   + _dot(vt_meta[p], p_ref[p, slot, pl.ds(band, N_META), :]))
            ot_ref[p, r] = o_t / den_ref[p, slot]

    stage_scores(0, 0)
    stage_softmax(0)
    stage_scores(1, 1)

    def pipe_body(j, carry):
        i0 = 2 + 2 * j
        stage_values(i0 - 2, 0)
        stage_softmax(1)
        stage_scores(i0, 0)
        stage_values(i0 - 1, 1)
        stage_softmax(0)
        stage_scores(i0 + 1, 1)
        return carry

    lax.fori_loop(0, (n_grid_rows - 2) // 2, pipe_body, 0)
    stage_values(n_grid_rows - 2, 0)
    stage_softmax(1)
    stage_values(n_grid_rows - 1, 1)

    for p in pairs:
        for c in range(n_tblocks):
            oa, ob = ot_ref[p, 2 * c], ot_ref[p, 2 * c + 1]
            sel_a = jnp.where(sub < dh, oa, pltpu.roll(oa, dh, 1))
            sel_b = jnp.where(sub < dh, pltpu.roll(ob, dh, 1), ob)
            out_ref[0, pl.ds(N_META + c * LANES, LANES), p * pw:(p + 1) * pw] = (
                jnp.where(lane < dh, sel_a, sel_b).T.astype(BF16))


def _na_call(h, gn, w, qg, kg, bias):
    bsz, n_rows, d = h.shape
    pw = 2 * NA_HEAD_DIM
    gw = NA_PAIRS * pw
    n_steps = w.shape[1] // (3 * gw)
    nw = 3 * gw
    band = NA_WIN_ROWS * GRID_W
    n_real = n_rows - N_META
    heads_per_step = 2 * NA_PAIRS
    qkg = jnp.concatenate([jnp.tile(qg, heads_per_step) * NA_HEAD_DIM ** -0.5,
                           jnp.tile(kg, heads_per_step)])[None, :].astype(F32)
    group_spec = lambda g: pl.BlockSpec((d, gw), functools.partial(lambda g, b, s: (0, g * n_steps + s), g))
    return pl.pallas_call(
        _na_kernel,
        grid=(bsz, n_steps),
        in_specs=[
            pl.BlockSpec((1, n_rows, d), lambda b, s: (b, 0, 0)),
            pl.BlockSpec((1, d), lambda b, s: (0, 0)),
            group_spec(0), group_spec(1), group_spec(2),
            pl.BlockSpec((1, 2 * gw), lambda b, s: (0, 0)),
            pl.BlockSpec((heads_per_step,) + bias.shape[1:], lambda b, s: (s, 0, 0, 0)),
        ],
        out_specs=pl.BlockSpec((1, n_rows, NA_PAIRS * pw), lambda b, s: (b, 0, s)),
        out_shape=jax.ShapeDtypeStruct((bsz, n_rows, n_steps * NA_PAIRS * pw), BF16),
        scratch_shapes=[
            pltpu.VMEM((n_rows, d), BF16),
            pltpu.VMEM((n_rows, nw), F32),
            pltpu.VMEM((NA_PAIRS, n_rows, pw), BF16),
            pltpu.VMEM((NA_PAIRS, n_rows, pw), BF16),
            pltpu.VMEM((NA_PAIRS, 2, pw, n_real), BF16),
            pltpu.VMEM((NA_PAIRS, n_real // GRID_W, pw, pw), F32),
            pltpu.VMEM((NA_PAIRS, 2, band + N_META, pw), F32),
            pltpu.VMEM((NA_PAIRS, 2, band + N_META, pw), BF16),
            pltpu.VMEM((NA_PAIRS, 2, 1, pw), F32),
            pltpu.VMEM((d, nw), BF16),
            pltpu.VMEM((NA_PAIRS, bias.shape[1], GRID_W, pw), F32),
        ],
        compiler_params=_cparams(2),
        name="neighbourhood_attention",
    )(h, gn, w, w, w, qkg, bias)


def _lane_prefix_exclusive(x, tri_strict):
    n_blocks = x.shape[1] // LANES
    carry = jnp.zeros((x.shape[0], 1), F32)
    pieces = []
    for j in range(n_blocks):
        blk = x[:, j * LANES:(j + 1) * LANES]
        pieces.append(_dot(blk.astype(BF16), tri_strict) + carry)
        carry = carry + jnp.sum(blk, axis=-1, keepdims=True)
    return jnp.concatenate(pieces, axis=1)


def _aligned_row_blocks(n_rows, step=512):
    blocks, r0 = [], 0
    while r0 < n_rows:
        nr = min(step, n_rows - r0)
        blocks.append((r0, nr))
        r0 += nr
    return blocks


def _outproj_kernel(n_mix, *refs):
    h_ref = refs[0]
    mix_refs = refs[1:1 + n_mix]
    wo_ref, fg_ref, wr_ref, hn_ref, u2_ref, lg_ref = refs[1 + n_mix:7 + n_mix]
    acc = h_ref[0]
    k0 = 0
    for m_ref in mix_refs:
        kw = m_ref.shape[2]
        acc = acc + _dot(m_ref[0], wo_ref[pl.ds(k0, kw), :])
        k0 += kw
    hn_ref[0] = acc
    u = _rms(acc, fg_ref[...])
    u_hi = u.astype(BF16)
    u2_ref[0] = u_hi
    u_lo = (u - u_hi.astype(F32)).astype(BF16)
    hh_hl = _dot(u_hi, wr_ref[...])
    lg_ref[0] = hh_hl[:, 0:LANES] + hh_hl[:, LANES:2 * LANES] + _dot(u_lo, wr_ref[:, 0:LANES])


def _outproj_call(h, mixes, wo, fg, wr):
    bsz, n_rows, d = h.shape
    n_mix = len(mixes)
    nr = _row_blocks(n_rows)[0][1]
    row_spec = lambda w: pl.BlockSpec((1, nr, w), lambda b, r: (b, r, 0))
    return pl.pallas_call(
        functools.partial(_outproj_kernel, n_mix),
        grid=(bsz, n_rows // nr),
        in_specs=[row_spec(d)] + [row_spec(m.shape[2]) for m in mixes] + [
            pl.BlockSpec((d, d), lambda b, r: (0, 0)),
            pl.BlockSpec((1, d), lambda b, r: (0, 0)),
            pl.BlockSpec((d, 2 * LANES), lambda b, r: (0, 0)),
        ],
        out_specs=[row_spec(d), row_spec(d), row_spec(LANES)],
        out_shape=[
            jax.ShapeDtypeStruct((bsz, n_rows, d), F32),
            jax.ShapeDtypeStruct((bsz, n_rows, d), BF16),
            jax.ShapeDtypeStruct((bsz, n_rows, LANES), F32),
        ],
        compiler_params=_cparams(2),
        name="outproj_norm_logits",
    )(h, *mixes, wo, fg, wr)


def _route_kernel(cap, lg_ref, slot_ref, gate_ref, lgc_ref):
    n_rows = lg_ref.shape[1]
    n_pad = slot_ref.shape[2]
    lgc_ref[pl.ds(0, n_rows), :] = lg_ref[0]
    lgc_ref[pl.ds(n_rows, n_pad - n_rows), :] = jnp.zeros((n_pad - n_rows, LANES), F32)

    logits = jnp.concatenate(
        [lgc_ref[pl.ds(c * LANES, LANES), :].T[0:N_EXPERTS, :] for c in range(n_pad // LANES)], axis=1)
    ex = jnp.exp(logits - jnp.max(logits, axis=0, keepdims=True))
    aff = ex / jnp.sum(ex, axis=0, keepdims=True)
    tok = lax.broadcasted_iota(jnp.int32, aff.shape, 1)
    aff = jnp.where(tok < n_rows, aff, -1.0)

    def count_ge(x):
        return jnp.sum(jnp.where(aff >= x, 1.0, 0.0), axis=-1, keepdims=True)

    capf = float(cap)

    def refine(base, cands):
        best = base
        for cand in cands:
            best = jnp.where(count_ge(cand) >= capf, cand, best)
        return best

    tiny = jnp.full((aff.shape[0], 1), 2.0 ** -126, F32)
    ok0 = count_ge(tiny) >= capf
    p = tiny
    p = refine(p, [p * (2.0 ** (16 * k)) for k in range(1, 8)])
    p = refine(p, [p * (2.0 ** (2 * k)) for k in range(1, 8)])
    p = refine(p, [p * 2.0])
    m = p
    for i in range(1, 8):
        step = p * (2.0 ** (-3 * i))
        m = refine(m, [m + k * step for k in range(1, 8)])
    step = p * (2.0 ** -23)
    m = refine(m, [m + k * step for k in range(1, 4)])
    thr = jnp.where(ok0, m, 0.0)

    gt = aff > thr
    eq = aff == thr
    need = capf - jnp.sum(jnp.where(gt, 1.0, 0.0), axis=-1, keepdims=True)
    r = lax.broadcasted_iota(jnp.int32, (LANES, LANES), 0)
    c = lax.broadcasted_iota(jnp.int32, (LANES, LANES), 1)
    tri_strict = jnp.where(r < c, 1.0, 0.0).astype(BF16)
    eq_rank = _lane_prefix_exclusive(jnp.where(eq, 1.0, 0.0), tri_strict)
    sel = gt | (eq & (eq_rank < need))
    pos = _lane_prefix_exclusive(jnp.where(sel, 1.0, 0.0), tri_strict)
    slot_ref[0] = jnp.where(sel, pos, -1.0).astype(jnp.int32)
    gate_ref[0] = jnp.where(sel, aff, 0.0)


def _route_call(lg, cap):
    bsz, n_rows, _ = lg.shape
    n_pad = ((n_rows + LANES - 1) // LANES) * LANES
    return pl.pallas_call(
        functools.partial(_route_kernel, cap),
        grid=(bsz,),
        in_specs=[pl.BlockSpec((1, n_rows, LANES), lambda b: (b, 0, 0))],
        out_specs=[
            pl.BlockSpec((1, N_EXPERTS, n_pad), lambda b: (b, 0, 0)),
            pl.BlockSpec((1, N_EXPERTS, n_pad), lambda b: (b, 0, 0)),
        ],
        out_shape=[
            jax.ShapeDtypeStruct((bsz, N_EXPERTS, n_pad), jnp.int32),
            jax.ShapeDtypeStruct((bsz, N_EXPERTS, n_pad), F32),
        ],
        scratch_shapes=[pltpu.VMEM((n_pad, LANES), F32)],
        compiler_params=_cparams(1),
        name="expert_choice_router",
    )(lg)


def _gather_kernel(u2_ref, slot_ref, xs_ref):
    cap_pad = xs_ref.shape[2]
    n_rows = u2_ref.shape[1]
    srow_id = lax.broadcasted_iota(jnp.int32, (cap_pad, n_rows), 0)

    def body(i, carry):
        pieces = [jnp.where(srow_id == slot_ref[0, pl.ds(i * GATHER_GROUP + j, 1), 0:n_rows],
                            1.0, 0.0).astype(BF16)
                  for j in range(GATHER_GROUP)]
        rows = _dot(jnp.concatenate(pieces, axis=0), u2_ref[0]).astype(BF16)
        for j in range(GATHER_GROUP):
            xs_ref[0, i * GATHER_GROUP + j] = rows[j * cap_pad:(j + 1) * cap_pad, :]
        return carry

    lax.fori_loop(0, N_EXPERTS // GATHER_GROUP, body, 0)


def _gather_call(u2, slot, cap_pad):
    bsz, n_rows, d = u2.shape
    n_pad = slot.shape[2]
    return pl.pallas_call(
        _gather_kernel,
        grid=(bsz,),
        in_specs=[
            pl.BlockSpec((1, n_rows, d), lambda b: (b, 0, 0)),
            pl.BlockSpec((1, N_EXPERTS, n_pad), lambda b: (b, 0, 0)),
        ],
        out_specs=pl.BlockSpec((1, N_EXPERTS, cap_pad, d), lambda b: (b, 0, 0, 0)),
        out_shape=jax.ShapeDtypeStruct((bsz, N_EXPERTS, cap_pad, d), BF16),
        compiler_params=_cparams(1),
        name="expert_gather",
    )(u2, slot)


def _expert_kernel(xs_ref, wg_ref, wu_ref, wd_ref, ys_ref, wgb_ref, wub_ref, wdb_ref):
    bb, _, cap_pad, d = xs_ref.shape

    @pl.when(pl.program_id(1) == 0)
    def _():
        wgb_ref[...] = wg_ref[0].astype(BF16)
        wub_ref[...] = wu_ref[0].astype(BF16)
        wdb_ref[...] = wd_ref[0].astype(BF16)

    x = xs_ref[...].reshape(bb * cap_pad, d)
    g = _dot(x, wgb_ref[...])
    u = _dot(x, wub_ref[...])
    hdn = (g * _sigmoid(g) * u).astype(BF16)
    ys_ref[...] = _dot(hdn, wdb_ref[...]).astype(BF16).reshape(ys_ref.shape)


def _expert_call(xs, wg, wu, wd, layer):
    bsz, n_e, cap_pad, d = xs.shape
    ff = wg.shape[-1]
    seq_blocks = 2 if bsz % 2 == 0 else 1
    bb = bsz // seq_blocks
    return pl.pallas_call(
        _expert_kernel,
        grid=(n_e, seq_blocks),
        in_specs=[
            pl.BlockSpec((bb, 1, cap_pad, d), lambda e, m: (m, e, 0, 0)),
            pl.BlockSpec((1, None, d, ff), lambda e, m: (layer, e, 0, 0)),
            pl.BlockSpec((1, None, d, ff), lambda e, m: (layer, e, 0, 0)),
            pl.BlockSpec((1, None, ff, d), lambda e, m: (layer, e, 0, 0)),
        ],
        out_specs=pl.BlockSpec((bb, 1, cap_pad, d), lambda e, m: (m, e, 0, 0)),
        out_shape=jax.ShapeDtypeStruct((bsz, n_e, cap_pad, d), BF16),
        scratch_shapes=[pltpu.VMEM((d, ff), BF16), pltpu.VMEM((d, ff), BF16), pltpu.VMEM((ff, d), BF16)],
        compiler_params=_cparams(2),
        name="expert_swiglu",
    )(xs, wg, wu, wd)


def _combine_kernel(h_ref, ym_ref, yt_ref, slot_ref, gate_ref, out_ref):
    n_rows = h_ref.shape[1]
    n_out = out_ref.shape[1]
    skip = n_rows - n_out
    _, e_grp, n_main, d = ym_ref.shape
    _, n_e, n_tail, _ = yt_ref.shape
    n_pad = slot_ref.shape[2]
    g = pl.program_id(1)

    @pl.when(g == 0)
    def _():
        out_ref[0] = h_ref[0, pl.ds(skip, n_out), :]

    def weights_for(e, first_slot, n_slots):
        srow_id = lax.broadcasted_iota(jnp.int32, (n_slots, n_pad), 0) + first_slot
        hit = srow_id == slot_ref[0, pl.ds(e, 1), :]
        return jnp.where(hit, gate_ref[0, pl.ds(e, 1), :], 0.0).astype(BF16)

    def accumulate(weights, ys):
        for t0, tn in _aligned_row_blocks(n_pad, 1024):
            lo, hi = max(t0, skip), min(t0 + tn, n_rows)
            if hi <= lo:
                continue
            part = _dot_tn(weights[:, t0:t0 + tn], ys)
            out_ref[0, pl.ds(lo - skip, hi - lo), :] += part[lo - t0:hi - t0, :]

    @pl.when(g == 0)
    def _():
        weights = jnp.concatenate([weights_for(e, n_main, n_tail) for e in range(n_e)], axis=0)
        accumulate(weights, yt_ref[0].reshape(n_e * n_tail, d))

    @pl.when(g > 0)
    def _():
        weights = jnp.concatenate([weights_for((g - 1) * e_grp + j, 0, n_main) for j in range(e_grp)], axis=0)
        accumulate(weights, ym_ref[0].reshape(e_grp * n_main, d))


def _combine_call(h, ys, slot, gate, n_out, e_grp=4):
    bsz, n_rows, d = h.shape
    _, n_e, cap_pad, _ = ys.shape
    n_pad = slot.shape[2]
    n_main_steps = n_e // e_grp
    n_tail = cap_pad - MXU_DEPTH
    return pl.pallas_call(
        _combine_kernel,
        grid=(bsz, n_main_steps + 1),
        in_specs=[
            pl.BlockSpec((1, n_rows, d), lambda b, g: (b, 0, 0)),
            pl.BlockSpec((1, e_grp, MXU_DEPTH, d), lambda b, g: (b, jnp.maximum(g - 1, 0), 0, 0)),
            pl.BlockSpec((1, n_e, n_tail, d), lambda b, g: (b, 0, MXU_DEPTH // n_tail, 0)),
            pl.BlockSpec((1, n_e, n_pad), lambda b, g: (b, 0, 0)),
            pl.BlockSpec((1, n_e, n_pad), lambda b, g: (b, 0, 0)),
        ],
        out_specs=pl.BlockSpec((1, n_out, d), lambda b, g: (b, 0, 0)),
        out_shape=jax.ShapeDtypeStruct((bsz, n_out, d), F32),
        compiler_params=_cparams(2),
        name="expert_combine",
    )(h, ys, ys, slot, gate)


def _even_weights(w_in, conv_w, gate_b, decay_logit):
    mw = N_HEADS * HEAD_DIM
    gates = w_in[:, 4 * mw:4 * mw + 4 * N_HEADS]
    r0 = 4 * mw + 4 * N_HEADS

    d = w_in.shape[0]
    steps, hw = N_HEADS // M_HPS, M_HPS * HEAD_DIM
    used = 2 * M_HPS

    def gate_cols(fw_off, bw_off):
        cols = jnp.stack([gates[:, fw_off:fw_off + N_HEADS], gates[:, bw_off:bw_off + N_HEADS]], axis=-1)
        cols = cols.reshape(d, steps, used).transpose(1, 0, 2)
        return jnp.pad(cols, ((0, 0), (0, 0), (0, LANES - used)))

    def gate_bias(fw_off, bw_off):
        b = jnp.stack([gate_b[fw_off:fw_off + N_HEADS], gate_b[bw_off:bw_off + N_HEADS]], axis=-1)
        return jnp.pad(b.reshape(steps, used), ((0, 0), (0, LANES - used)))

    w_m = w_in[:, 0:4 * mw].astype(BF16)
    w_g = jnp.concatenate([gate_cols(0, 2 * N_HEADS), gate_cols(N_HEADS, 3 * N_HEADS)], axis=-1).astype(BF16)
    gb = jnp.concatenate([gate_bias(0, 2 * N_HEADS), gate_bias(N_HEADS, 3 * N_HEADS)], axis=-1)[:, None, :]
    conv = jnp.concatenate([conv_w[:, :mw].reshape(CONV_K, steps, hw),
                            conv_w[:, mw:].reshape(CONV_K, steps, hw)], axis=-1).transpose(1, 0, 2)
    w_r = w_in[:, r0:r0 + 4 * mw].astype(BF16)
    dl = jnp.pad(decay_logit.T, ((0, 0), (0, LANES - 2)))[:, None, :]
    return w_m, w_g, gb.astype(F32), conv.astype(F32), w_r, dl.astype(F32)


def _rotary_tables(n_rows):
    half = HEAD_DIM // 2
    inv = ROPE_BASE ** (-jnp.arange(half, dtype=F32) / half)
    ang = jnp.arange(n_rows, dtype=F32)[:, None] * inv[None, :]
    cos, sin = jnp.cos(ang), jnp.sin(ang)
    return jnp.concatenate([cos, cos], axis=-1), jnp.concatenate([-sin, sin], axis=-1)


def _na_bias_table(rpb):
    col = np.arange(GRID_W)
    col_start = np.clip(col - NA_WIN_COLS // 2, 0, GRID_W - NA_WIN_COLS)
    col_in = (col[None, :] >= col_start[:, None]) & (col[None, :] < col_start[:, None] + NA_WIN_COLS)
    dc_idx = np.clip(col[None, :] - col[:, None], -(NA_WIN_COLS - 1), NA_WIN_COLS - 1) + NA_WIN_COLS - 1
    selector = (np.arange(rpb.shape[2])[:, None, None] == dc_idx.T[None]).astype(np.float32)
    rpb_cols = jnp.einsum('hrd,dkq->hrkq', rpb.astype(F32), selector, precision=lax.Precision.HIGHEST)
    return jnp.where(col_in.T[None, None], rpb_cols, NEG)


def _ffn(h, mixes, wo, fg, wr, wg, wu, wd, layer, n_out):
    n_rows = h.shape[1]
    cap = CAP_FACTOR * n_rows // N_EXPERTS
    cap_pad = ((cap + BF16_ROWS - 1) // BF16_ROWS) * BF16_ROWS
    wr_hi = wr.astype(BF16)
    wr_lo = (wr - wr_hi.astype(F32)).astype(BF16)
    pad = ((0, 0), (0, LANES - N_EXPERTS))
    wr_pieces = jnp.concatenate([jnp.pad(wr_hi, pad), jnp.pad(wr_lo, pad)], axis=1)
    hn, u2, lg = _outproj_call(h, mixes, wo.astype(BF16), fg[None, :], wr_pieces)
    slot, gate = _route_call(lg, cap)
    xs = _gather_call(u2, slot, cap_pad)
    ys = _expert_call(xs, wg, wu, wd, layer)
    return _combine_call(hn, ys, slot, gate, n_out)


def kernel(x, meta_tokens, attn_norm_g, ffn_norm_g, even_w_in, even_conv_w, even_gate_b, even_m_norm_g, even_ret_decay_logit, even_r_norm_g, even_w_out, odd_w_in, odd_q_norm_g, odd_k_norm_g, odd_rpb, odd_w_out, router_w, expert_w_gate, expert_w_up, expert_w_down):
    bsz = x.shape[0]
    depth = attn_norm_g.shape[0]
    meta = jnp.broadcast_to(meta_tokens.astype(x.dtype)[None], (bsz,) + meta_tokens.shape)
    h = jnp.concatenate([meta, x], axis=1)
    n_rows = h.shape[1]
    cos2, sin2 = _rotary_tables(n_rows)
    for layer in range(depth):
        j = layer // 2
        gn = attn_norm_g[layer][None, :]
        if layer % 2 == 0:
            w_m, w_g, gb, conv, w_r, dl = _even_weights(even_w_in[j], even_conv_w[j], even_gate_b[j],
                                                        even_ret_decay_logit[j])
            m_out = _mlstm_call(h, gn, w_m, w_g, conv, gb,
                                even_m_norm_g[j].reshape(N_HEADS // M_HPS, 1, M_HPS * HEAD_DIM))
            r_out = _ret_call(h, gn, w_r, cos2, sin2, dl, even_r_norm_g[j].reshape(N_HEADS, 1, HEAD_DIM))
            mixes, wo = [m_out, r_out], even_w_out[j]
        else:
            a_out = _na_call(h, gn, odd_w_in[j].astype(BF16), odd_q_norm_g[j], odd_k_norm_g[j],
                             _na_bias_table(odd_rpb[j]))
            mixes, wo = [a_out], odd_w_out[j]
        n_out = n_rows - N_META if layer == depth - 1 else n_rows
        h = _ffn(h, mixes, wo, ffn_norm_g[layer], router_w[layer],
                 expert_w_gate, expert_w_up, expert_w_down, layer, n_out)
    return h
```

```python
import functools

import jax
import jax.numpy as jnp
import numpy as np
from jax import lax
from jax.experimental import pallas as pl
from jax.experimental.pallas import tpu as pltpu

F32 = jnp.float32
BF16 = jnp.bfloat16

LANES = 128
BF16_ROWS = 16
MXU_DEPTH = 256
N_META = 16
GRID_W = 64
EPS = 1e-6
HEAD_DIM = 128
N_HEADS = 4
M_HPS = 2
R_HPS = 2
CONV_K = 5
CONV_PAD = 8
ROPE_BASE = 10000.0
NA_HEAD_DIM = 64
NA_WIN_ROWS = 8
NA_WIN_COLS = 16
NA_PAIRS = 2
N_EXPERTS = 16
CAP_FACTOR = 2
GATHER_GROUP = 2
NEG = -1e30
SCAN_T = 256
VMEM_LIMIT = 56 * 1024 * 1024


def _cparams(n_axes):
    return pltpu.CompilerParams(
        dimension_semantics=("arbitrary",) * n_axes, vmem_limit_bytes=VMEM_LIMIT)


def _row_blocks(n_rows):
    for nb in (3, 2, 4, 6, 1):
        if n_rows % (8 * nb) == 0:
            step = n_rows // nb
            return [(i * step, step) for i in range(nb)]
    return [(0, n_rows)]


def _rms(x, g):
    return x * lax.rsqrt(jnp.mean(x * x, axis=-1, keepdims=True) + EPS) * g


def _sigmoid(x):
    return 1.0 / (1.0 + jnp.exp(-x))


def _log_sigmoid(x):
    return jnp.minimum(x, 0.0) - jnp.log(1.0 + jnp.exp(-jnp.abs(x)))


def _dot(a, b):
    return jnp.dot(a, b, preferred_element_type=F32)


def _dot_nt(a, b):
    return lax.dot_general(a, b, (((1,), (1,)), ((), ())), preferred_element_type=F32)


def _dot_tn(a, b):
    return lax.dot_general(a, b, (((0,), (0,)), ((), ())), preferred_element_type=F32)


def _split3(x):
    hi = x.astype(BF16)
    r1 = x - hi.astype(F32)
    mid = r1.astype(BF16)
    lo = (r1 - mid.astype(F32)).astype(BF16)
    return hi, mid, lo


def _tri_prefix(tri_bf16, x):
    hi, mid, lo = _split3(x)
    return _dot(tri_bf16, hi) + _dot(tri_bf16, mid) + _dot(tri_bf16, lo)


def _norm_to_scratch(h_ref, gn_ref, un_ref):
    n_rows = h_ref.shape[1]
    for r0, nr in _row_blocks(n_rows):
        x = h_ref[0, pl.ds(r0, nr), :]
        un_ref[pl.ds(r0, nr), :] = _rms(x, gn_ref[...]).astype(BF16)


def _assemble_weights(w_refs, wcat_ref):
    c0 = 0
    for w_ref in w_refs:
        w = w_ref[...] if len(w_ref.shape) == 2 else w_ref[0]
        wcat_ref[:, c0:c0 + w.shape[1]] = w
        c0 += w.shape[1]


def _project(un_ref, wcat_ref, z_ref):
    n_rows = un_ref.shape[0]
    for r0, nr in _row_blocks(n_rows):
        z_ref[pl.ds(r0, nr), :] = _dot(un_ref[pl.ds(r0, nr), :], wcat_ref[...])


def _tri_mask(n, lower):
    r = lax.broadcasted_iota(jnp.int32, (n, n), 0)
    c = lax.broadcasted_iota(jnp.int32, (n, n), 1)
    return (c <= r) if lower else (c >= r)


def _mlstm_chunk(qc, kt, v_aug, a_row, f_col, mask, state):
    c_aug, g_prev = state
    qb = qc.astype(BF16)
    cm = jnp.max(jnp.where(mask, a_row, NEG), axis=-1, keepdims=True)
    g_col = jnp.maximum(g_prev, cm)
    dm = jnp.exp(jnp.where(mask, a_row - g_col, NEG))
    s = _dot(qb, kt.astype(BF16)) * dm
    w_inter = jnp.exp(g_prev - g_col)
    tot = w_inter * _dot(qb, c_aug.astype(BF16)) + _dot(s.astype(BF16), v_aug)
    den = tot[:, HEAD_DIM:2 * HEAD_DIM][:, 0:1]
    out = tot[:, 0:HEAD_DIM] / jnp.maximum(jnp.abs(den), jnp.exp(-(f_col + g_col)))
    g_end = jnp.maximum(g_prev, jnp.max(a_row, axis=-1, keepdims=True))
    ktw = (kt * jnp.exp(a_row - g_end)).astype(BF16)
    c_new = jnp.exp(g_prev - g_end) * c_aug + _dot(ktw, v_aug)
    return out, (c_new, g_end)


def _mlstm_kernel(h_ref, gn_ref, wq_ref, wk_ref, wv_ref, wo_ref, wg_ref, conv_ref, gb_ref, ng_ref, out_ref,
                  un_ref, z_ref, zc_ref, acc_ref, acol_ref, fcol_ref, atm_ref, atr_ref, kt_ref, vaug_ref,
                  wcat_ref):
    n_rows = h_ref.shape[1]
    n_real = n_rows - N_META
    t = SCAN_T
    n_chunks = n_real // t
    hw = M_HPS * HEAD_DIM
    heads = range(M_HPS)

    @pl.when(pl.program_id(1) == 0)
    def _():
        _norm_to_scratch(h_ref, gn_ref, un_ref)

    _assemble_weights([wq_ref, wk_ref, wv_ref, wo_ref, wg_ref], wcat_ref)
    _project(un_ref, wcat_ref, z_ref)

    zc_ref[pl.ds(0, CONV_PAD), :] = jnp.zeros((CONV_PAD, 2 * hw), F32)
    zc_ref[pl.ds(CONV_PAD + n_rows, CONV_PAD), :] = jnp.zeros((CONV_PAD, 2 * hw), F32)
    for r0, nr in _row_blocks(n_rows):
        zc_ref[pl.ds(CONV_PAD + r0, nr), :] = z_ref[pl.ds(r0, nr), 0:2 * hw]
    lane2 = lax.broadcasted_iota(jnp.int32, (1, 2 * hw), 1)
    qk_scale = jnp.where(lane2 >= hw, HEAD_DIM ** -0.5, 1.0).astype(F32)
    for r0, nr in _row_blocks(n_rows):
        acc = jnp.zeros((nr, 2 * hw), F32)
        for j in range(CONV_K):
            off = CONV_PAD - (CONV_K - 1) // 2 + j + r0
            acc = acc + zc_ref[pl.ds(off, nr), :] * conv_ref[0, pl.ds(j, 1), :]
        z_ref[pl.ds(r0, nr), 0:2 * hw] = acc * _sigmoid(acc) * qk_scale

    gi_off, gf_off = 4 * hw, 4 * hw + LANES
    bias_i = gb_ref[0, :, 0:LANES]
    bias_f = gb_ref[0, :, LANES:2 * LANES]
    lane = lax.broadcasted_iota(jnp.int32, (1, LANES), 1)
    tri_m = jnp.where(_tri_mask(N_META, True), 1.0, 0.0).astype(BF16)
    lf_m = _log_sigmoid(z_ref[pl.ds(0, N_META), gf_off:gf_off + LANES] + bias_f)
    f_meta = _tri_prefix(tri_m, lf_m)
    f_meta_end = f_meta[N_META - 1:N_META, :]
    tri_b = jnp.where(_tri_mask(LANES, True), 1.0, 0.0).astype(BF16)

    def prefix_body(c, carry):
        r0 = pl.multiple_of(N_META + c * LANES, 8)
        lf = _log_sigmoid(z_ref[pl.ds(r0, LANES), gf_off:gf_off + LANES] + bias_f)
        p = _tri_prefix(tri_b, lf) + carry
        fcol_ref[pl.ds(r0, LANES), :] = p
        return p[LANES - 1:LANES, :]

    total = lax.fori_loop(0, n_real // LANES, prefix_body, jnp.zeros((1, LANES), F32), unroll=4)

    fcol_ref[pl.ds(0, N_META), :] = f_meta
    acol_ref[pl.ds(0, N_META), :] = z_ref[pl.ds(0, N_META), gi_off:gi_off + LANES] + bias_i - f_meta

    def finish_body(c, carry):
        r0 = pl.multiple_of(N_META + c * LANES, 8)
        lf = _log_sigmoid(z_ref[pl.ds(r0, LANES), gf_off:gf_off + LANES] + bias_f)
        p = fcol_ref[pl.ds(r0, LANES), :]
        f = f_meta_end + jnp.where(lane % 2 == 0, p, total - p + lf)
        fcol_ref[pl.ds(r0, LANES), :] = f
        a = z_ref[pl.ds(r0, LANES), gi_off:gi_off + LANES] + bias_i - f
        acol_ref[pl.ds(r0, LANES), :] = a
        l0 = pl.multiple_of(c * LANES, LANES)
        atr_ref[:, pl.ds(l0, LANES)] = a.T[0:8, :]
        for hh in heads:
            kt_ref[hh, :, pl.ds(l0, LANES)] = z_ref[pl.ds(r0, LANES), hw + hh * HEAD_DIM:hw + (hh + 1) * HEAD_DIM].T
            vaug_ref[hh, pl.ds(r0, LANES), :] = jnp.concatenate(
                [z_ref[pl.ds(r0, LANES), 2 * hw + hh * HEAD_DIM:2 * hw + (hh + 1) * HEAD_DIM], ones_col],
                axis=1).astype(BF16)
        return carry

    ones_col = jnp.where(lax.broadcasted_iota(jnp.int32, (LANES, LANES), 1) == 0, 1.0, 0.0)
    ones_col_m = jnp.where(lax.broadcasted_iota(jnp.int32, (N_META, LANES), 1) == 0, 1.0, 0.0)
    lax.fori_loop(0, n_real // LANES, finish_body, 0, unroll=4)
    atm_ref[...] = acol_ref[pl.ds(0, LANES), :].T[0:8, :]

    chains = [(hh, d) for hh in heads for d in (0, 1)]
    mask_m = _tri_mask(N_META, True)
    states = []
    for hh, d in chains:
        g_lane = 2 * hh + d
        kt_meta = z_ref[pl.ds(0, LANES), hw + hh * HEAD_DIM:hw + (hh + 1) * HEAD_DIM].T[:, 0:N_META]
        vaug_meta = jnp.concatenate(
            [z_ref[pl.ds(0, N_META), 2 * hw + hh * HEAD_DIM:2 * hw + (hh + 1) * HEAD_DIM], ones_col_m],
            axis=1).astype(BF16)
        state = (jnp.zeros((HEAD_DIM, 2 * HEAD_DIM), F32), jnp.zeros((1, 1), F32))
        out_m, state = _mlstm_chunk(
            z_ref[pl.ds(0, N_META), hh * HEAD_DIM:(hh + 1) * HEAD_DIM], kt_meta, vaug_meta,
            atm_ref[pl.ds(g_lane, 1), 0:N_META], fcol_ref[pl.ds(0, N_META), g_lane:g_lane + 1], mask_m, state)
        acc_ref[hh, d, pl.ds(0, N_META), :] = out_m
        states.append(state)

    def body(i, carry):
        new = []
        for idx, (hh, d) in enumerate(chains):
            g_lane = 2 * hh + d
            c = i if d == 0 else n_chunks - 1 - i
            r0 = pl.multiple_of(N_META + c * t, 8)
            l0 = pl.multiple_of(c * t, LANES)
            out, st = _mlstm_chunk(
                z_ref[pl.ds(r0, t), hh * HEAD_DIM:(hh + 1) * HEAD_DIM], kt_ref[hh, :, pl.ds(l0, t)],
                vaug_ref[hh, pl.ds(r0, t), :], atr_ref[pl.ds(g_lane, 1), pl.ds(l0, t)],
                fcol_ref[pl.ds(r0, t), g_lane:g_lane + 1], _tri_mask(t, d == 0), carry[idx])
            acc_ref[hh, d, pl.ds(r0, t), :] = out
            new.append(st)
        return tuple(new)

    lax.fori_loop(0, n_chunks, body, tuple(states))

    for hh in heads:
        cols = slice(hh * HEAD_DIM, (hh + 1) * HEAD_DIM)
        for r0, nr in _row_blocks(n_rows):
            hs = acc_ref[hh, 0, pl.ds(r0, nr), :] + acc_ref[hh, 1, pl.ds(r0, nr), :]
            o = z_ref[pl.ds(r0, nr), 3 * hw + hh * HEAD_DIM:3 * hw + (hh + 1) * HEAD_DIM]
            out_ref[0, pl.ds(r0, nr), cols] = (_sigmoid(o) * _rms(hs, ng_ref[0][:, cols])).astype(BF16)


def _head_column_specs(d, n_groups, heads_per_step=1):
    steps = N_HEADS // heads_per_step
    return [pl.BlockSpec((d, heads_per_step * HEAD_DIM), functools.partial(lambda g, b, s: (0, g * steps + s), g))
            for g in range(n_groups)]


def _mlstm_call(h, gn, w, wg, conv, gb, ng):
    bsz, n_rows, d = h.shape
    hw = M_HPS * HEAD_DIM
    nw = 4 * hw + wg.shape[-1]
    n_real = n_rows - N_META
    return pl.pallas_call(
        _mlstm_kernel,
        grid=(bsz, N_HEADS // M_HPS),
        in_specs=[
            pl.BlockSpec((1, n_rows, d), lambda b, s: (b, 0, 0)),
            pl.BlockSpec((1, d), lambda b, s: (0, 0)),
            *_head_column_specs(d, 4, M_HPS),
            pl.BlockSpec((1, d, wg.shape[-1]), lambda b, s: (s, 0, 0)),
            pl.BlockSpec((1, CONV_K, 2 * hw), lambda b, s: (s, 0, 0)),
            pl.BlockSpec((1, 1, 2 * LANES), lambda b, s: (s, 0, 0)),
            pl.BlockSpec((1, 1, hw), lambda b, s: (s, 0, 0)),
        ],
        out_specs=pl.BlockSpec((1, n_rows, hw), lambda b, s: (b, 0, s)),
        out_shape=jax.ShapeDtypeStruct((bsz, n_rows, N_HEADS * HEAD_DIM), BF16),
        scratch_shapes=[
            pltpu.VMEM((n_rows, d), BF16),
            pltpu.VMEM((n_rows, nw), F32),
            pltpu.VMEM((n_rows + 2 * CONV_PAD, 2 * hw), F32),
            pltpu.VMEM((M_HPS, 2, n_rows, HEAD_DIM), F32),
            pltpu.VMEM((n_rows, LANES), F32),
            pltpu.VMEM((n_rows, LANES), F32),
            pltpu.VMEM((8, LANES), F32),
            pltpu.VMEM((8, n_real), F32),
            pltpu.VMEM((M_HPS, HEAD_DIM, n_real), F32),
            pltpu.VMEM((M_HPS, n_rows, 2 * HEAD_DIM), BF16),
            pltpu.VMEM((d, nw), BF16),
        ],
        compiler_params=_cparams(2),
        name="mlstm_heads",
    )(h, gn, w, w, w, w, wg, conv, gb, ng)


def _ret_chunk(qc, kt, vc, dmat, dq, dk_row, dchunk, r_st):
    qb, vb = qc.astype(BF16), vc.astype(BF16)
    s = _dot(qb, kt.astype(BF16)) * dmat
    out = _dot(s.astype(BF16), vb) + dq * _dot(qb, r_st.astype(BF16))
    r_new = dchunk * r_st + _dot((kt * dk_row).astype(BF16), vb)
    return out, r_new


def _decay_tables(n, lg, forward):
    r = lax.broadcasted_iota(jnp.int32, (n, n), 0)
    c = lax.broadcasted_iota(jnp.int32, (n, n), 1)
    pos_q = lax.broadcasted_iota(jnp.int32, (n, 1), 0).astype(F32)
    pos_k = lax.broadcasted_iota(jnp.int32, (1, n), 1).astype(F32)
    if forward:
        dist, mask = (r - c).astype(F32), c <= r
        dq, dk = jnp.exp((pos_q + 1.0) * lg), jnp.exp((n - 1.0 - pos_k) * lg)
    else:
        dist, mask = (c - r).astype(F32), c >= r
        dq, dk = jnp.exp((n - pos_q) * lg), jnp.exp(pos_k * lg)
    dmat = jnp.exp(jnp.where(mask, dist * lg, NEG))
    return dmat, dq, dk, jnp.exp(n * lg)


def _ret_kernel(h_ref, gn_ref, wq_ref, wk_ref, wv_ref, wg_ref, cos_ref, sin_ref, dl_ref, ng_ref, out_ref,
                un_ref, z_ref, acc_ref, dmat_ref, dq_ref, kt_ref, wcat_ref):
    n_rows = h_ref.shape[1]
    n_real = n_rows - N_META
    t = SCAN_T
    n_chunks = n_real // t
    hw = R_HPS * HEAD_DIM
    heads = range(R_HPS)

    def cols(group, hh):
        return slice(group * hw + hh * HEAD_DIM, group * hw + (hh + 1) * HEAD_DIM)

    @pl.when(pl.program_id(1) == 0)
    def _():
        _norm_to_scratch(h_ref, gn_ref, un_ref)

    _assemble_weights([wq_ref, wk_ref, wv_ref, wg_ref], wcat_ref)
    _project(un_ref, wcat_ref, z_ref)

    for hh in heads:
        for r0, nr in _row_blocks(n_rows):
            cs, sn = cos_ref[pl.ds(r0, nr), :], sin_ref[pl.ds(r0, nr), :]
            q = z_ref[pl.ds(r0, nr), cols(0, hh)]
            z_ref[pl.ds(r0, nr), cols(0, hh)] = q * cs + pltpu.roll(q, HEAD_DIM // 2, 1) * sn
            k = z_ref[pl.ds(r0, nr), cols(1, hh)]
            z_ref[pl.ds(r0, nr), cols(1, hh)] = (
                (k * cs + pltpu.roll(k, HEAD_DIM // 2, 1) * sn) * HEAD_DIM ** -0.5)

    for hh in heads:
        for c in range(n_real // LANES):
            kt_ref[hh, :, c * LANES:(c + 1) * LANES] = z_ref[pl.ds(N_META + c * LANES, LANES), cols(1, hh)].T

    lg_all = _log_sigmoid(dl_ref[0])
    chains = [(hh, d) for hh in heads for d in (0, 1)]
    states, dk_rows, dchunks = [], [], []
    for hh, d in chains:
        lg = lg_all[:, 2 * hh + d:2 * hh + d + 1]
        dmat_m, _, dk_m, dch_m = _decay_tables(N_META, lg, True)
        dmat, dq, dk, dch = _decay_tables(t, lg, d == 0)
        dmat_ref[hh, d] = dmat
        dq_ref[hh, d] = jnp.broadcast_to(dq, (t, LANES))
        dk_rows.append(dk)
        dchunks.append(dch)
        kt_meta = z_ref[pl.ds(0, LANES), cols(1, hh)].T[:, 0:N_META]
        out_m, r_st = _ret_chunk(
            z_ref[pl.ds(0, N_META), cols(0, hh)], kt_meta, z_ref[pl.ds(0, N_META), cols(2, hh)],
            dmat_m, jnp.zeros((N_META, 1), F32), dk_m, dch_m, jnp.zeros((HEAD_DIM, HEAD_DIM), F32))
        acc_ref[hh, d, pl.ds(0, N_META), :] = out_m
        states.append(r_st)

    def body(i, carry):
        new = []
        for idx, (hh, d) in enumerate(chains):
            c = i if d == 0 else n_chunks - 1 - i
            r0 = pl.multiple_of(N_META + c * t, 8)
            l0 = pl.multiple_of(c * t, LANES)
            out, st = _ret_chunk(
                z_ref[pl.ds(r0, t), cols(0, hh)], kt_ref[hh, :, pl.ds(l0, t)], z_ref[pl.ds(r0, t), cols(2, hh)],
                dmat_ref[hh, d], dq_ref[hh, d], dk_rows[idx], dchunks[idx], carry[idx])
            acc_ref[hh, d, pl.ds(r0, t), :] = out
            new.append(st)
        return tuple(new)

    lax.fori_loop(0, n_chunks, body, tuple(states))

    for hh in heads:
        oc = slice(hh * HEAD_DIM, (hh + 1) * HEAD_DIM)
        for r0, nr in _row_blocks(n_rows):
            hs = acc_ref[hh, 0, pl.ds(r0, nr), :] + acc_ref[hh, 1, pl.ds(r0, nr), :]
            g = z_ref[pl.ds(r0, nr), cols(3, hh)]
            out_ref[0, pl.ds(r0, nr), oc] = (g * _sigmoid(g) * _rms(hs, ng_ref[0][:, oc])).astype(BF16)


def _ret_call(h, gn, w, cos2, sin2, dl, ng):
    bsz, n_rows, d = h.shape
    hw = R_HPS * HEAD_DIM
    nw = 4 * hw
    return pl.pallas_call(
        _ret_kernel,
        grid=(bsz, N_HEADS // R_HPS),
        in_specs=[
            pl.BlockSpec((1, n_rows, d), lambda b, s: (b, 0, 0)),
            pl.BlockSpec((1, d), lambda b, s: (0, 0)),
            *_head_column_specs(d, 4, R_HPS),
            pl.BlockSpec((n_rows, HEAD_DIM), lambda b, s: (0, 0)),
            pl.BlockSpec((n_rows, HEAD_DIM), lambda b, s: (0, 0)),
            pl.BlockSpec((1, 1, LANES), lambda b, s: (s, 0, 0)),
            pl.BlockSpec((1, 1, hw), lambda b, s: (s, 0, 0)),
        ],
        out_specs=pl.BlockSpec((1, n_rows, hw), lambda b, s: (b, 0, s)),
        out_shape=jax.ShapeDtypeStruct((bsz, n_rows, N_HEADS * HEAD_DIM), BF16),
        scratch_shapes=[
            pltpu.VMEM((n_rows, d), BF16),
            pltpu.VMEM((n_rows, nw), F32),
            pltpu.VMEM((R_HPS, 2, n_rows, HEAD_DIM), F32),
            pltpu.VMEM((R_HPS, 2, SCAN_T, SCAN_T), F32),
            pltpu.VMEM((R_HPS, 2, SCAN_T, LANES), F32),
            pltpu.VMEM((R_HPS, HEAD_DIM, n_rows - N_META), F32),
            pltpu.VMEM((d, nw), BF16),
        ],
        compiler_params=_cparams(2),
        name="retention_heads",
    )(h, gn, w, w, w, w, cos2, sin2, dl, ng)


def _head_pair_block_diag(qp):
    lane = lax.broadcasted_iota(jnp.int32, qp.shape, 1)
    zero = jnp.zeros_like(qp)
    return jnp.concatenate([jnp.where(lane < NA_HEAD_DIM, qp, zero),
                            jnp.where(lane >= NA_HEAD_DIM, qp, zero)], axis=0)


def _na_attend(qp, keys, values, bias_t):
    n = qp.shape[0]
    bd = _head_pair_block_diag(qp)
    scores = []
    for kb, bt in zip(keys, bias_t):
        s = _dot_nt(kb, bd)
        scores.append(s if bt is None else s + bt)
    m = scores[0].max(axis=0, keepdims=True)
    for s in scores[1:]:
        m = jnp.maximum(m, s.max(axis=0, keepdims=True))
    probs = [jnp.exp(s - m) for s in scores]
    den = probs[0].sum(axis=0, keepdims=True)
    for p in probs[1:]:
        den = den + p.sum(axis=0, keepdims=True)
    inv = 1.0 / den
    o2 = None
    for p, vb in zip(probs, values):
        part = _dot_tn((p * inv).astype(BF16), vb)
        o2 = part if o2 is None else o2 + part
    lane = lax.broadcasted_iota(jnp.int32, (n, 2 * NA_HEAD_DIM), 1)
    return jnp.where(lane < NA_HEAD_DIM, o2[0:n, :], o2[n:2 * n, :])


def _na_kernel(h_ref, gn_ref, wq_ref, wk_ref, wv_ref, qkg_ref, bias_ref, out_ref,
               un_ref, z_ref, q_ref, k_ref, vt_ref, ot_ref, s_ref, p_ref, den_ref, wcat_ref, bias_s_ref):
    n_rows = h_ref.shape[1]
    n_real = n_rows - N_META
    n_grid_rows = n_real // GRID_W
    dh = NA_HEAD_DIM
    pw = 2 * dh
    gw = NA_PAIRS * pw
    band = NA_WIN_ROWS * GRID_W
    pairs = range(NA_PAIRS)
    v_off = 2 * gw

    @pl.when(pl.program_id(1) == 0)
    def _():
        _norm_to_scratch(h_ref, gn_ref, un_ref)

    _assemble_weights([wq_ref, wk_ref, wv_ref], wcat_ref)
    _project(un_ref, wcat_ref, z_ref)

    for p in pairs:
        for dr in range(bias_ref.shape[1]):
            bias_s_ref[p, dr] = jnp.concatenate([bias_ref[2 * p, dr], bias_ref[2 * p + 1, dr]], axis=1)

    r = lax.broadcasted_iota(jnp.int32, (gw, gw), 0)
    c = lax.broadcasted_iota(jnp.int32, (gw, gw), 1)
    head_ones = jnp.where(r // dh == c // dh, 1.0, 0.0).astype(BF16)
    for g, dst_ref in enumerate((q_ref, k_ref)):
        for r0, nr in _row_blocks(n_rows):
            x = z_ref[pl.ds(r0, nr), g * gw:(g + 1) * gw]
            ssq = _dot((x * x).astype(BF16), head_ones)
            y = (x * lax.rsqrt(ssq * (1.0 / dh) + EPS) * qkg_ref[:, g * gw:(g + 1) * gw]).astype(BF16)
            for p in pairs:
                dst_ref[p, pl.ds(r0, nr), :] = y[:, p * pw:(p + 1) * pw]

    n_tblocks = n_real // LANES
    k_meta, vt_meta = [], []
    for p in pairs:
        v_cols = slice(v_off + p * pw, v_off + (p + 1) * pw)
        for cpy in (0, 1):
            for c in range(n_tblocks - cpy):
                vb = z_ref[pl.ds(N_META + cpy * GRID_W + c * LANES, LANES), v_cols]
                vt_ref[p, cpy, :, c * LANES:(c + 1) * LANES] = vb.T.astype(BF16)
        vt_ref[p, 1, :, (n_tblocks - 1) * LANES:n_tblocks * LANES] = jnp.zeros((pw, LANES), BF16)
        vt_meta.append(z_ref[pl.ds(0, LANES), v_cols].T[:, 0:N_META].astype(BF16))
        k_meta.append(k_ref[p, pl.ds(0, N_META), :])
        out_ref[0, pl.ds(0, N_META), p * pw:(p + 1) * pw] = _na_attend(
            q_ref[p, pl.ds(0, N_META), :], [k_meta[p]], [z_ref[pl.ds(0, N_META), v_cols].astype(BF16)],
            [None]).astype(BF16)

    sub = lax.broadcasted_iota(jnp.int32, (pw, pw), 0)
    lane = lax.broadcasted_iota(jnp.int32, (pw, pw), 1)

    def row_start(r):
        return jnp.clip(r - NA_WIN_ROWS // 2, 0, n_grid_rows - NA_WIN_ROWS)

    def stage_scores(r, slot):
        rs = row_start(r)
        q0 = pl.multiple_of(N_META + r * GRID_W, 8)
        k0 = pl.multiple_of(N_META + rs * GRID_W, 8)
        for p in pairs:
            bd = _head_pair_block_diag(q_ref[p, pl.ds(q0, GRID_W), :])
            bias = bias_s_ref[p, pl.ds(NA_WIN_ROWS - 1 - (r - rs), NA_WIN_ROWS)].reshape(band, pw)
            s_ref[p, slot, pl.ds(0, band), :] = _dot_nt(k_ref[p, pl.ds(k0, band), :], bd) + bias
            s_ref[p, slot, pl.ds(band, N_META), :] = _dot_nt(k_meta[p], bd)

    def stage_softmax(slot):
        for p in pairs:
            s = s_ref[p, slot]
            e = jnp.exp(s - s.max(axis=0, keepdims=True))
            den_ref[p, slot] = e.sum(axis=0, keepdims=True)
            p_ref[p, slot] = e.astype(BF16)

    def stage_values(r, slot):
        rs = row_start(r)
        par = rs % 2
        l0 = pl.multiple_of((rs - par) * GRID_W, LANES)
        for p in pairs:
            o_t = (_dot(vt_ref[p, par, :, pl.ds(l0, band)], p_ref[p, slot, pl.ds(0, band), :])
                   + _dot(vt_meta[p], p_ref[p, slot, pl.ds(band, N_META), :]))
            ot_ref[p, r] = o_t / den_ref[p, slot]

    stage_scores(0, 0)
    stage_softmax(0)
    stage_scores(1, 1)

    def pipe_body(j, carry):
        i0 = 2 + 2 * j
        stage_values(i0 - 2, 0)
        stage_softmax(1)
        stage_scores(i0, 0)
        stage_values(i0 - 1, 1)
        stage_softmax(0)
        stage_scores(i0 + 1, 1)
        return carry

    lax.fori_loop(0, (n_grid_rows - 2) // 2, pipe_body, 0)
    stage_values(n_grid_rows - 2, 0)
    stage_softmax(1)
    stage_values(n_grid_rows - 1, 1)

    for p in pairs:
        for c in range(n_tblocks):
            oa, ob = ot_ref[p, 2 * c], ot_ref[p, 2 * c + 1]
            sel_a = jnp.where(sub < dh, oa, pltpu.roll(oa, dh, 1))
            sel_b = jnp.where(sub < dh, pltpu.roll(ob, dh, 1), ob)
            out_ref[0, pl.ds(N_META + c * LANES, LANES), p * pw:(p + 1) * pw] = (
                jnp.where(lane < dh, sel_a, sel_b).T.astype(BF16))


def _na_call(h, gn, w, qg, kg, bias):
    bsz, n_rows, d = h.shape
    pw = 2 * NA_HEAD_DIM
    gw = NA_PAIRS * pw
    n_steps = w.shape[1] // (3 * gw)
    nw = 3 * gw
    band = NA_WIN_ROWS * GRID_W
    n_real = n_rows - N_META
    heads_per_step = 2 * NA_PAIRS
    qkg = jnp.concatenate([jnp.tile(qg, heads_per_step) * NA_HEAD_DIM ** -0.5,
                           jnp.tile(kg, heads_per_step)])[None, :].astype(F32)
    group_spec = lambda g: pl.BlockSpec((d, gw), functools.partial(lambda g, b, s: (0, g * n_steps + s), g))
    return pl.pallas_call(
        _na_kernel,
        grid=(bsz, n_steps),
        in_specs=[
            pl.BlockSpec((1, n_rows, d), lambda b, s: (b, 0, 0)),
            pl.BlockSpec((1, d), lambda b, s: (0, 0)),
            group_spec(0), group_spec(1), group_spec(2),
            pl.BlockSpec((1, 2 * gw), lambda b, s: (0, 0)),
            pl.BlockSpec((heads_per_step,) + bias.shape[1:], lambda b, s: (s, 0, 0, 0)),
        ],
        out_specs=pl.BlockSpec((1, n_rows, NA_PAIRS * pw), lambda b, s: (b, 0, s)),
        out_shape=jax.ShapeDtypeStruct((bsz, n_rows, n_steps * NA_PAIRS * pw), BF16),
        scratch_shapes=[
            pltpu.VMEM((n_rows, d), BF16),
            pltpu.VMEM((n_rows, nw), F32),
            pltpu.VMEM((NA_PAIRS, n_rows, pw), BF16),
            pltpu.VMEM((NA_PAIRS, n_rows, pw), BF16),
            pltpu.VMEM((NA_PAIRS, 2, pw, n_real), BF16),
            pltpu.VMEM((NA_PAIRS, n_real // GRID_W, pw, pw), F32),
            pltpu.VMEM((NA_PAIRS, 2, band + N_META, pw), F32),
            pltpu.VMEM((NA_PAIRS, 2, band + N_META, pw), BF16),
            pltpu.VMEM((NA_PAIRS, 2, 1, pw), F32),
            pltpu.VMEM((d, nw), BF16),
            pltpu.VMEM((NA_PAIRS, bias.shape[1], GRID_W, pw), F32),
        ],
        compiler_params=_cparams(2),
        name="neighbourhood_attention",
    )(h, gn, w, w, w, qkg, bias)


def _lane_prefix_exclusive(x, tri_strict):
    n_blocks = x.shape[1] // LANES
    carry = jnp.zeros((x.shape[0], 1), F32)
    pieces = []
    for j in range(n_blocks):
        blk = x[:, j * LANES:(j + 1) * LANES]
        pieces.append(_dot(blk.astype(BF16), tri_strict) + carry)
        carry = carry + jnp.sum(blk, axis=-1, keepdims=True)
    return jnp.concatenate(pieces, axis=1)


def _aligned_row_blocks(n_rows, step=512):
    blocks, r0 = [], 0
    while r0 < n_rows:
        nr = min(step, n_rows - r0)
        blocks.append((r0, nr))
        r0 += nr
    return blocks


def _outproj_kernel(n_mix, *refs):
    h_ref = refs[0]
    mix_refs = refs[1:1 + n_mix]
    wo_ref, fg_ref, wr_ref, hn_ref, u2_ref, lg_ref = refs[1 + n_mix:7 + n_mix]
    acc = h_ref[0]
    k0 = 0
    for m_ref in mix_refs:
        kw = m_ref.shape[2]
        acc = acc + _dot(m_ref[0], wo_ref[pl.ds(k0, kw), :])
        k0 += kw
    hn_ref[0] = acc
    u = _rms(acc, fg_ref[...])
    u_hi = u.astype(BF16)
    u2_ref[0] = u_hi
    u_lo = (u - u_hi.astype(F32)).astype(BF16)
    hh_hl = _dot(u_hi, wr_ref[...])
    lg_ref[0] = hh_hl[:, 0:LANES] + hh_hl[:, LANES:2 * LANES] + _dot(u_lo, wr_ref[:, 0:LANES])


def _outproj_call(h, mixes, wo, fg, wr):
    bsz, n_rows, d = h.shape
    n_mix = len(mixes)
    nr = _row_blocks(n_rows)[0][1]
    row_spec = lambda w: pl.BlockSpec((1, nr, w), lambda b, r: (b, r, 0))
    return pl.pallas_call(
        functools.partial(_outproj_kernel, n_mix),
        grid=(bsz, n_rows // nr),
        in_specs=[row_spec(d)] + [row_spec(m.shape[2]) for m in mixes] + [
            pl.BlockSpec((d, d), lambda b, r: (0, 0)),
            pl.BlockSpec((1, d), lambda b, r: (0, 0)),
            pl.BlockSpec((d, 2 * LANES), lambda b, r: (0, 0)),
        ],
        out_specs=[row_spec(d), row_spec(d), row_spec(LANES)],
        out_shape=[
            jax.ShapeDtypeStruct((bsz, n_rows, d), F32),
            jax.ShapeDtypeStruct((bsz, n_rows, d), BF16),
            jax.ShapeDtypeStruct((bsz, n_rows, LANES), F32),
        ],
        compiler_params=_cparams(2),
        name="outproj_norm_logits",
    )(h, *mixes, wo, fg, wr)


def _route_kernel(cap, lg_ref, slot_ref, gate_ref, lgc_ref):
    n_rows = lg_ref.shape[1]
    n_pad = slot_ref.shape[2]
    lgc_ref[pl.ds(0, n_rows), :] = lg_ref[0]
    lgc_ref[pl.ds(n_rows, n_pad - n_rows), :] = jnp.zeros((n_pad - n_rows, LANES), F32)

    logits = jnp.concatenate(
        [lgc_ref[pl.ds(c * LANES, LANES), :].T[0:N_EXPERTS, :] for c in range(n_pad // LANES)], axis=1)
    ex = jnp.exp(logits - jnp.max(logits, axis=0, keepdims=True))
    aff = ex / jnp.sum(ex, axis=0, keepdims=True)
    tok = lax.broadcasted_iota(jnp.int32, aff.shape, 1)
    aff = jnp.where(tok < n_rows, aff, -1.0)

    def count_ge(x):
        return jnp.sum(jnp.where(aff >= x, 1.0, 0.0), axis=-1, keepdims=True)

    capf = float(cap)

    def refine(base, cands):
        best = base
        for cand in cands:
            best = jnp.where(count_ge(cand) >= capf, cand, best)
        return best

    tiny = jnp.full((aff.shape[0], 1), 2.0 ** -126, F32)
    ok0 = count_ge(tiny) >= capf
    p = tiny
    p = refine(p, [p * (2.0 ** (16 * k)) for k in range(1, 8)])
    p = refine(p, [p * (2.0 ** (2 * k)) for k in range(1, 8)])
    p = refine(p, [p * 2.0])
    m = p
    for i in range(1, 8):
        step = p * (2.0 ** (-3 * i))
        m = refine(m, [m + k * step for k in range(1, 8)])
    step = p * (2.0 ** -23)
    m = refine(m, [m + k * step for k in range(1, 4)])
    thr = jnp.where(ok0, m, 0.0)

    gt = aff > thr
    eq = aff == thr
    need = capf - jnp.sum(jnp.where(gt, 1.0, 0.0), axis=-1, keepdims=True)
    r = lax.broadcasted_iota(jnp.int32, (LANES, LANES), 0)
    c = lax.broadcasted_iota(jnp.int32, (LANES, LANES), 1)
    tri_strict = jnp.where(r < c, 1.0, 0.0).astype(BF16)
    eq_rank = _lane_prefix_exclusive(jnp.where(eq, 1.0, 0.0), tri_strict)
    sel = gt | (eq & (eq_rank < need))
    pos = _lane_prefix_exclusive(jnp.where(sel, 1.0, 0.0), tri_strict)
    slot_ref[0] = jnp.where(sel, pos, -1.0).astype(jnp.int32)
    gate_ref[0] = jnp.where(sel, aff, 0.0)


def _route_call(lg, cap):
    bsz, n_rows, _ = lg.shape
    n_pad = ((n_rows + LANES - 1) // LANES) * LANES
    return pl.pallas_call(
        functools.partial(_route_kernel, cap),
        grid=(bsz,),
        in_specs=[pl.BlockSpec((1, n_rows, LANES), lambda b: (b, 0, 0))],
        out_specs=[
            pl.BlockSpec((1, N_EXPERTS, n_pad), lambda b: (b, 0, 0)),
            pl.BlockSpec((1, N_EXPERTS, n_pad), lambda b: (b, 0, 0)),
        ],
        out_shape=[
            jax.ShapeDtypeStruct((bsz, N_EXPERTS, n_pad), jnp.int32),
            jax.ShapeDtypeStruct((bsz, N_EXPERTS, n_pad), F32),
        ],
        scratch_shapes=[pltpu.VMEM((n_pad, LANES), F32)],
        compiler_params=_cparams(1),
        name="expert_choice_router",
    )(lg)


def _gather_kernel(u2_ref, slot_ref, xs_ref):
    cap_pad = xs_ref.shape[2]
    n_rows = u2_ref.shape[1]
    srow_id = lax.broadcasted_iota(jnp.int32, (cap_pad, n_rows), 0)

    def body(i, carry):
        pieces = [jnp.where(srow_id == slot_ref[0, pl.ds(i * GATHER_GROUP + j, 1), 0:n_rows],
                            1.0, 0.0).astype(BF16)
                  for j in range(GATHER_GROUP)]
        rows = _dot(jnp.concatenate(pieces, axis=0), u2_ref[0]).astype(BF16)
        for j in range(GATHER_GROUP):
            xs_ref[0, i * GATHER_GROUP + j] = rows[j * cap_pad:(j + 1) * cap_pad, :]
        return carry

    lax.fori_loop(0, N_EXPERTS // GATHER_GROUP, body, 0)


def _gather_call(u2, slot, cap_pad):
    bsz, n_rows, d = u2.shape
    n_pad = slot.shape[2]
    return pl.pallas_call(
        _gather_kernel,
        grid=(bsz,),
        in_specs=[
            pl.BlockSpec((1, n_rows, d), lambda b: (b, 0, 0)),
            pl.BlockSpec((1, N_EXPERTS, n_pad), lambda b: (b, 0, 0)),
        ],
        out_specs=pl.BlockSpec((1, N_EXPERTS, cap_pad, d), lambda b: (b, 0, 0, 0)),
        out_shape=jax.ShapeDtypeStruct((bsz, N_EXPERTS, cap_pad, d), BF16),
        compiler_params=_cparams(1),
        name="expert_gather",
    )(u2, slot)


def _expert_kernel(xs_ref, wg_ref, wu_ref, wd_ref, ys_ref, wgb_ref, wub_ref, wdb_ref):
    bb, _, cap_pad, d = xs_ref.shape

    @pl.when(pl.program_id(1) == 0)
    def _():
        wgb_ref[...] = wg_ref[0].astype(BF16)
        wub_ref[...] = wu_ref[0].astype(BF16)
        wdb_ref[...] = wd_ref[0].astype(BF16)

    x = xs_ref[...].reshape(bb * cap_pad, d)
    g = _dot(x, wgb_ref[...])
    u = _dot(x, wub_ref[...])
    hdn = (g * _sigmoid(g) * u).astype(BF16)
    ys_ref[...] = _dot(hdn, wdb_ref[...]).astype(BF16).reshape(ys_ref.shape)


def _expert_call(xs, wg, wu, wd, layer):
    bsz, n_e, cap_pad, d = xs.shape
    ff = wg.shape[-1]
    seq_blocks = 2 if bsz % 2 == 0 else 1
    bb = bsz // seq_blocks
    return pl.pallas_call(
        _expert_kernel,
        grid=(n_e, seq_blocks),
        in_specs=[
            pl.BlockSpec((bb, 1, cap_pad, d), lambda e, m: (m, e, 0, 0)),
            pl.BlockSpec((1, None, d, ff), lambda e, m: (layer, e, 0, 0)),
            pl.BlockSpec((1, None, d, ff), lambda e, m: (layer, e, 0, 0)),
            pl.BlockSpec((1, None, ff, d), lambda e, m: (layer, e, 0, 0)),
        ],
        out_specs=pl.BlockSpec((bb, 1, cap_pad, d), lambda e, m: (m, e, 0, 0)),
        out_shape=jax.ShapeDtypeStruct((bsz, n_e, cap_pad, d), BF16),
        scratch_shapes=[pltpu.VMEM((d, ff), BF16), pltpu.VMEM((d, ff), BF16), pltpu.VMEM((ff, d), BF16)],
        compiler_params=_cparams(2),
        name="expert_swiglu",
    )(xs, wg, wu, wd)


def _combine_kernel(h_ref, ym_ref, yt_ref, slot_ref, gate_ref, out_ref):
    n_rows = h_ref.shape[1]
    n_out = out_ref.shape[1]
    skip = n_rows - n_out
    _, e_grp, n_main, d = ym_ref.shape
    _, n_e, n_tail, _ = yt_ref.shape
    n_pad = slot_ref.shape[2]
    g = pl.program_id(1)

    @pl.when(g == 0)
    def _():
        out_ref[0] = h_ref[0, pl.ds(skip, n_out), :]

    def weights_for(e, first_slot, n_slots):
        srow_id = lax.broadcasted_iota(jnp.int32, (n_slots, n_pad), 0) + first_slot
        hit = srow_id == slot_ref[0, pl.ds(e, 1), :]
        return jnp.where(hit, gate_ref[0, pl.ds(e, 1), :], 0.0).astype(BF16)

    def accumulate(weights, ys):
        for t0, tn in _aligned_row_blocks(n_pad, 1024):
            lo, hi = max(t0, skip), min(t0 + tn, n_rows)
            if hi <= lo:
                continue
            part = _dot_tn(weights[:, t0:t0 + tn], ys)
            out_ref[0, pl.ds(lo - skip, hi - lo), :] += part[lo - t0:hi - t0, :]

    @pl.when(g == 0)
    def _():
        weights = jnp.concatenate([weights_for(e, n_main, n_tail) for e in range(n_e)], axis=0)
        accumulate(weights, yt_ref[0].reshape(n_e * n_tail, d))

    @pl.when(g > 0)
    def _():
        weights = jnp.concatenate([weights_for((g - 1) * e_grp + j, 0, n_main) for j in range(e_grp)], axis=0)
        accumulate(weights, ym_ref[0].reshape(e_grp * n_main, d))


def _combine_call(h, ys, slot, gate, n_out, e_grp=4):
    bsz, n_rows, d = h.shape
    _, n_e, cap_pad, _ = ys.shape
    n_pad = slot.shape[2]
    n_main_steps = n_e // e_grp
    n_tail = cap_pad - MXU_DEPTH
    return pl.pallas_call(
        _combine_kernel,
        grid=(bsz, n_main_steps + 1),
        in_specs=[
            pl.BlockSpec((1, n_rows, d), lambda b, g: (b, 0, 0)),
            pl.BlockSpec((1, e_grp, MXU_DEPTH, d), lambda b, g: (b, jnp.maximum(g - 1, 0), 0, 0)),
            pl.BlockSpec((1, n_e, n_tail, d), lambda b, g: (b, 0, MXU_DEPTH // n_tail, 0)),
            pl.BlockSpec((1, n_e, n_pad), lambda b, g: (b, 0, 0)),
            pl.BlockSpec((1, n_e, n_pad), lambda b, g: (b, 0, 0)),
        ],
        out_specs=pl.BlockSpec((1, n_out, d), lambda b, g: (b, 0, 0)),
        out_shape=jax.ShapeDtypeStruct((bsz, n_out, d), F32),
        compiler_params=_cparams(2),
        name="expert_combine",
    )(h, ys, ys, slot, gate)


def _even_weights(w_in, conv_w, gate_b, decay_logit):
    mw = N_HEADS * HEAD_DIM
    gates = w_in[:, 4 * mw:4 * mw + 4 * N_HEADS]
    r0 = 4 * mw + 4 * N_HEADS

    d = w_in.shape[0]
    steps, hw = N_HEADS // M_HPS, M_HPS * HEAD_DIM
    used = 2 * M_HPS

    def gate_cols(fw_off, bw_off):
        cols = jnp.stack([gates[:, fw_off:fw_off + N_HEADS], gates[:, bw_off:bw_off + N_HEADS]], axis=-1)
        cols = cols.reshape(d, steps, used).transpose(1, 0, 2)
        return jnp.pad(cols, ((0, 0), (0, 0), (0, LANES - used)))

    def gate_bias(fw_off, bw_off):
        b = jnp.stack([gate_b[fw_off:fw_off + N_HEADS], gate_b[bw_off:bw_off + N_HEADS]], axis=-1)
        return jnp.pad(b.reshape(steps, used), ((0, 0), (0, LANES - used)))

    w_m = w_in[:, 0:4 * mw].astype(BF16)
    w_g = jnp.concatenate([gate_cols(0, 2 * N_HEADS), gate_cols(N_HEADS, 3 * N_HEADS)], axis=-1).astype(BF16)
    gb = jnp.concatenate([gate_bias(0, 2 * N_HEADS), gate_bias(N_HEADS, 3 * N_HEADS)], axis=-1)[:, None, :]
    conv = jnp.concatenate([conv_w[:, :mw].reshape(CONV_K, steps, hw),
                            conv_w[:, mw:].reshape(CONV_K, steps, hw)], axis=-1).transpose(1, 0, 2)
    w_r = w_in[:, r0:r0 + 4 * mw].astype(BF16)
    dl = jnp.pad(decay_logit.T.reshape(N_HEADS // R_HPS, 2 * R_HPS),
                 ((0, 0), (0, LANES - 2 * R_HPS)))[:, None, :]
    return w_m, w_g, gb.astype(F32), conv.astype(F32), w_r, dl.astype(F32)


def _rotary_tables(n_rows):
    half = HEAD_DIM // 2
    inv = ROPE_BASE ** (-jnp.arange(half, dtype=F32) / half)
    ang = jnp.arange(n_rows, dtype=F32)[:, None] * inv[None, :]
    cos, sin = jnp.cos(ang), jnp.sin(ang)
    return jnp.concatenate([cos, cos], axis=-1), jnp.concatenate([-sin, sin], axis=-1)


def _na_bias_table(rpb):
    col = np.arange(GRID_W)
    col_start = np.clip(col - NA_WIN_COLS // 2, 0, GRID_W - NA_WIN_COLS)
    col_in = (col[None, :] >= col_start[:, None]) & (col[None, :] < col_start[:, None] + NA_WIN_COLS)
    dc_idx = np.clip(col[None, :] - col[:, None], -(NA_WIN_COLS - 1), NA_WIN_COLS - 1) + NA_WIN_COLS - 1
    selector = (np.arange(rpb.shape[2])[:, None, None] == dc_idx.T[None]).astype(np.float32)
    rpb_cols = jnp.einsum('hrd,dkq->hrkq', rpb.astype(F32), selector, precision=lax.Precision.HIGHEST)
    return jnp.where(col_in.T[None, None], rpb_cols, NEG)


def _ffn(h, mixes, wo, fg, wr, wg, wu, wd, layer, n_out):
    n_rows = h.shape[1]
    cap = CAP_FACTOR * n_rows // N_EXPERTS
    cap_pad = ((cap + BF16_ROWS - 1) // BF16_ROWS) * BF16_ROWS
    wr_hi = wr.astype(BF16)
    wr_lo = (wr - wr_hi.astype(F32)).astype(BF16)
    pad = ((0, 0), (0, LANES - N_EXPERTS))
    wr_pieces = jnp.concatenate([jnp.pad(wr_hi, pad), jnp.pad(wr_lo, pad)], axis=1)
    hn, u2, lg = _outproj_call(h, mixes, wo.astype(BF16), fg[None, :], wr_pieces)
    slot, gate = _route_call(lg, cap)
    xs = _gather_call(u2, slot, cap_pad)
    ys = _expert_call(xs, wg, wu, wd, layer)
    return _combine_call(hn, ys, slot, gate, n_out)


def kernel(x, meta_tokens, attn_norm_g, ffn_norm_g, even_w_in, even_conv_w, even_gate_b, even_m_norm_g, even_ret_decay_logit, even_r_norm_g, even_w_out, odd_w_in, odd_q_norm_g, odd_k_norm_g, odd_rpb, odd_w_out, router_w, expert_w_gate, expert_w_up, expert_w_down):
    bsz = x.shape[0]
    depth = attn_norm_g.shape[0]
    meta = jnp.broadcast_to(meta_tokens.astype(x.dtype)[None], (bsz,) + meta_tokens.shape)
    h = jnp.concatenate([meta, x], axis=1)
    n_rows = h.shape[1]
    cos2, sin2 = _rotary_tables(n_rows)
    for layer in range(depth):
        j = layer // 2
        gn = attn_norm_g[layer][None, :]
        if layer % 2 == 0:
            w_m, w_g, gb, conv, w_r, dl = _even_weights(even_w_in[j], even_conv_w[j], even_gate_b[j],
                                                        even_ret_decay_logit[j])
            m_out = _mlstm_call(h, gn, w_m, w_g, conv, gb,
                                even_m_norm_g[j].reshape(N_HEADS // M_HPS, 1, M_HPS * HEAD_DIM))
            r_out = _ret_call(h, gn, w_r, cos2, sin2, dl,
                              even_r_norm_g[j].reshape(N_HEADS // R_HPS, 1, R_HPS * HEAD_DIM))
            mixes, wo = [m_out, r_out], even_w_out[j]
        else:
            a_out = _na_call(h, gn, odd_w_in[j].astype(BF16), odd_q_norm_g[j], odd_k_norm_g[j],
                             _na_bias_table(odd_rpb[j]))
            mixes, wo = [a_out], odd_w_out[j]
        n_out = n_rows - N_META if layer == depth - 1 else n_rows
        h = _ffn(h, mixes, wo, ffn_norm_g[layer], router_w[layer],
                 expert_w_gate, expert_w_up, expert_w_down, layer, n_out)
    return h
```

```python
import functools

import jax
import jax.numpy as jnp
import numpy as np
from jax import lax
from jax.experimental import pallas as pl
from jax.experimental.pallas import tpu as pltpu

F32 = jnp.float32
BF16 = jnp.bfloat16

LANES = 128
BF16_ROWS = 16
MXU_DEPTH = 256
N_META = 16
GRID_W = 64
EPS = 1e-6
HEAD_DIM = 128
N_HEADS = 4
M_HPS = 2
R_HPS = 2
CONV_K = 5
CONV_PAD = 8
ROPE_BASE = 10000.0
NA_HEAD_DIM = 64
NA_WIN_ROWS = 8
NA_WIN_COLS = 16
NA_PAIRS = 2
N_EXPERTS = 16
CAP_FACTOR = 2
GATHER_GROUP = 2
NEG = -1e30
SCAN_T = 256
VMEM_LIMIT = 56 * 1024 * 1024


def _cparams(n_axes):
    return pltpu.CompilerParams(
        dimension_semantics=("arbitrary",) * n_axes, vmem_limit_bytes=VMEM_LIMIT)


def _row_blocks(n_rows):
    for nb in (3, 2, 4, 6, 1):
        if n_rows % (8 * nb) == 0:
            step = n_rows // nb
            return [(i * step, step) for i in range(nb)]
    return [(0, n_rows)]


def _rms(x, g):
    return x * lax.rsqrt(jnp.mean(x * x, axis=-1, keepdims=True) + EPS) * g


def _sigmoid(x):
    return 1.0 / (1.0 + jnp.exp(-x))


def _log_sigmoid(x):
    return jnp.minimum(x, 0.0) - jnp.log(1.0 + jnp.exp(-jnp.abs(x)))


def _dot(a, b):
    return jnp.dot(a, b, preferred_element_type=F32)


def _dot_nt(a, b):
    return lax.dot_general(a, b, (((1,), (1,)), ((), ())), preferred_element_type=F32)


def _dot_tn(a, b):
    return lax.dot_general(a, b, (((0,), (0,)), ((), ())), preferred_element_type=F32)


def _split3(x):
    hi = x.astype(BF16)
    r1 = x - hi.astype(F32)
    mid = r1.astype(BF16)
    lo = (r1 - mid.astype(F32)).astype(BF16)
    return hi, mid, lo


def _tri_prefix(tri_bf16, x):
    hi, mid, lo = _split3(x)
    return _dot(tri_bf16, hi) + _dot(tri_bf16, mid) + _dot(tri_bf16, lo)


def _norm_to_scratch(h_ref, gn_ref, un_ref):
    n_rows = h_ref.shape[1]
    for r0, nr in _row_blocks(n_rows):
        x = h_ref[0, pl.ds(r0, nr), :]
        un_ref[pl.ds(r0, nr), :] = _rms(x, gn_ref[...]).astype(BF16)


def _assemble_weights(w_refs, wcat_ref):
    c0 = 0
    for w_ref in w_refs:
        w = w_ref[...] if len(w_ref.shape) == 2 else w_ref[0]
        wcat_ref[:, c0:c0 + w.shape[1]] = w
        c0 += w.shape[1]


def _project(un_ref, wcat_ref, z_ref):
    n_rows = un_ref.shape[0]
    for r0, nr in _row_blocks(n_rows):
        z_ref[pl.ds(r0, nr), :] = _dot(un_ref[pl.ds(r0, nr), :], wcat_ref[...])


def _tri_mask(n, lower):
    r = lax.broadcasted_iota(jnp.int32, (n, n), 0)
    c = lax.broadcasted_iota(jnp.int32, (n, n), 1)
    return (c <= r) if lower else (c >= r)


def _mlstm_chunk(qc, kt, v_aug, a_row, f_col, mask, state):
    c_aug, g_prev = state
    qb = qc.astype(BF16)
    cm = jnp.max(jnp.where(mask, a_row, NEG), axis=-1, keepdims=True)
    g_col = jnp.maximum(g_prev, cm)
    dm = jnp.exp(jnp.where(mask, a_row - g_col, NEG))
    s = _dot(qb, kt.astype(BF16)) * dm
    w_inter = jnp.exp(g_prev - g_col)
    tot = w_inter * _dot(qb, c_aug.astype(BF16)) + _dot(s.astype(BF16), v_aug)
    den = tot[:, HEAD_DIM:2 * HEAD_DIM][:, 0:1]
    out = tot[:, 0:HEAD_DIM] / jnp.maximum(jnp.abs(den), jnp.exp(-(f_col + g_col)))
    g_end = jnp.maximum(g_prev, jnp.max(a_row, axis=-1, keepdims=True))
    ktw = (kt * jnp.exp(a_row - g_end)).astype(BF16)
    c_new = jnp.exp(g_prev - g_end) * c_aug + _dot(ktw, v_aug)
    return out, (c_new, g_end)


def _mlstm_kernel(h_ref, gn_ref, wq_ref, wk_ref, wv_ref, wo_ref, wg_ref, conv_ref, gb_ref, ng_ref, out_ref,
                  un_ref, z_ref, zc_ref, acc_ref, acol_ref, fcol_ref, atm_ref, atr_ref, kt_ref, vaug_ref,
                  wcat_ref):
    n_rows = h_ref.shape[1]
    n_real = n_rows - N_META
    t = SCAN_T
    n_chunks = n_real // t
    hw = M_HPS * HEAD_DIM
    heads = range(M_HPS)

    @pl.when(pl.program_id(1) == 0)
    def _():
        _norm_to_scratch(h_ref, gn_ref, un_ref)

    _assemble_weights([wq_ref, wk_ref, wv_ref, wo_ref, wg_ref], wcat_ref)
    _project(un_ref, wcat_ref, z_ref)

    zc_ref[pl.ds(0, CONV_PAD), :] = jnp.zeros((CONV_PAD, 2 * hw), F32)
    zc_ref[pl.ds(CONV_PAD + n_rows, CONV_PAD), :] = jnp.zeros((CONV_PAD, 2 * hw), F32)
    for r0, nr in _row_blocks(n_rows):
        zc_ref[pl.ds(CONV_PAD + r0, nr), :] = z_ref[pl.ds(r0, nr), 0:2 * hw]
    lane2 = lax.broadcasted_iota(jnp.int32, (1, 2 * hw), 1)
    qk_scale = jnp.where(lane2 >= hw, HEAD_DIM ** -0.5, 1.0).astype(F32)
    for r0, nr in _row_blocks(n_rows):
        acc = jnp.zeros((nr, 2 * hw), F32)
        for j in range(CONV_K):
            off = CONV_PAD - (CONV_K - 1) // 2 + j + r0
            acc = acc + zc_ref[pl.ds(off, nr), :] * conv_ref[0, pl.ds(j, 1), :]
        z_ref[pl.ds(r0, nr), 0:2 * hw] = acc * _sigmoid(acc) * qk_scale

    gi_off, gf_off = 4 * hw, 4 * hw + LANES
    bias_i = gb_ref[0, :, 0:LANES]
    bias_f = gb_ref[0, :, LANES:2 * LANES]
    lane = lax.broadcasted_iota(jnp.int32, (1, LANES), 1)
    tri_m = jnp.where(_tri_mask(N_META, True), 1.0, 0.0).astype(BF16)
    lf_m = _log_sigmoid(z_ref[pl.ds(0, N_META), gf_off:gf_off + LANES] + bias_f)
    f_meta = _tri_prefix(tri_m, lf_m)
    f_meta_end = f_meta[N_META - 1:N_META, :]
    tri_b = jnp.where(_tri_mask(LANES, True), 1.0, 0.0).astype(BF16)

    def prefix_body(c, carry):
        r0 = pl.multiple_of(N_META + c * LANES, 8)
        lf = _log_sigmoid(z_ref[pl.ds(r0, LANES), gf_off:gf_off + LANES] + bias_f)
        p = _tri_prefix(tri_b, lf) + carry
        fcol_ref[pl.ds(r0, LANES), :] = p
        return p[LANES - 1:LANES, :]

    total = lax.fori_loop(0, n_real // LANES, prefix_body, jnp.zeros((1, LANES), F32), unroll=8)

    fcol_ref[pl.ds(0, N_META), :] = f_meta
    acol_ref[pl.ds(0, N_META), :] = z_ref[pl.ds(0, N_META), gi_off:gi_off + LANES] + bias_i - f_meta

    def finish_body(c, carry):
        r0 = pl.multiple_of(N_META + c * LANES, 8)
        lf = _log_sigmoid(z_ref[pl.ds(r0, LANES), gf_off:gf_off + LANES] + bias_f)
        p = fcol_ref[pl.ds(r0, LANES), :]
        f = f_meta_end + jnp.where(lane % 2 == 0, p, total - p + lf)
        fcol_ref[pl.ds(r0, LANES), :] = f
        a = z_ref[pl.ds(r0, LANES), gi_off:gi_off + LANES] + bias_i - f
        acol_ref[pl.ds(r0, LANES), :] = a
        l0 = pl.multiple_of(c * LANES, LANES)
        atr_ref[:, pl.ds(l0, LANES)] = a.T[0:8, :]
        for hh in heads:
            kt_ref[hh, :, pl.ds(l0, LANES)] = z_ref[pl.ds(r0, LANES), hw + hh * HEAD_DIM:hw + (hh + 1) * HEAD_DIM].T
            vaug_ref[hh, pl.ds(r0, LANES), :] = jnp.concatenate(
                [z_ref[pl.ds(r0, LANES), 2 * hw + hh * HEAD_DIM:2 * hw + (hh + 1) * HEAD_DIM], ones_col],
                axis=1).astype(BF16)
        return carry

    ones_col = jnp.where(lax.broadcasted_iota(jnp.int32, (LANES, LANES), 1) == 0, 1.0, 0.0)
    ones_col_m = jnp.where(lax.broadcasted_iota(jnp.int32, (N_META, LANES), 1) == 0, 1.0, 0.0)
    lax.fori_loop(0, n_real // LANES, finish_body, 0, unroll=8)
    atm_ref[...] = acol_ref[pl.ds(0, LANES), :].T[0:8, :]

    chains = [(hh, d) for hh in heads for d in (0, 1)]
    mask_m = _tri_mask(N_META, True)
    states = []
    for hh, d in chains:
        g_lane = 2 * hh + d
        kt_meta = z_ref[pl.ds(0, LANES), hw + hh * HEAD_DIM:hw + (hh + 1) * HEAD_DIM].T[:, 0:N_META]
        vaug_meta = jnp.concatenate(
            [z_ref[pl.ds(0, N_META), 2 * hw + hh * HEAD_DIM:2 * hw + (hh + 1) * HEAD_DIM], ones_col_m],
            axis=1).astype(BF16)
        state = (jnp.zeros((HEAD_DIM, 2 * HEAD_DIM), F32), jnp.zeros((1, 1), F32))
        out_m, state = _mlstm_chunk(
            z_ref[pl.ds(0, N_META), hh * HEAD_DIM:(hh + 1) * HEAD_DIM], kt_meta, vaug_meta,
            atm_ref[pl.ds(g_lane, 1), 0:N_META], fcol_ref[pl.ds(0, N_META), g_lane:g_lane + 1], mask_m, state)
        acc_ref[hh, d, pl.ds(0, N_META), :] = out_m
        states.append(state)

    def body(i, carry):
        new = []
        for idx, (hh, d) in enumerate(chains):
            g_lane = 2 * hh + d
            c = i if d == 0 else n_chunks - 1 - i
            r0 = pl.multiple_of(N_META + c * t, 8)
            l0 = pl.multiple_of(c * t, LANES)
            out, st = _mlstm_chunk(
                z_ref[pl.ds(r0, t), hh * HEAD_DIM:(hh + 1) * HEAD_DIM], kt_ref[hh, :, pl.ds(l0, t)],
                vaug_ref[hh, pl.ds(r0, t), :], atr_ref[pl.ds(g_lane, 1), pl.ds(l0, t)],
                fcol_ref[pl.ds(r0, t), g_lane:g_lane + 1], _tri_mask(t, d == 0), carry[idx])
            acc_ref[hh, d, pl.ds(r0, t), :] = out
            new.append(st)
        return tuple(new)

    lax.fori_loop(0, n_chunks, body, tuple(states))

    for hh in heads:
        cols = slice(hh * HEAD_DIM, (hh + 1) * HEAD_DIM)
        for r0, nr in _row_blocks(n_rows):
            hs = acc_ref[hh, 0, pl.ds(r0, nr), :] + acc_ref[hh, 1, pl.ds(r0, nr), :]
            o = z_ref[pl.ds(r0, nr), 3 * hw + hh * HEAD_DIM:3 * hw + (hh + 1) * HEAD_DIM]
            out_ref[0, pl.ds(r0, nr), cols] = (_sigmoid(o) * _rms(hs, ng_ref[0][:, cols])).astype(BF16)


def _head_column_specs(d, n_groups, heads_per_step=1):
    steps = N_HEADS // heads_per_step
    return [pl.BlockSpec((d, heads_per_step * HEAD_DIM), functools.partial(lambda g, b, s: (0, g * steps + s), g))
            for g in range(n_groups)]


def _mlstm_call(h, gn, w, wg, conv, gb, ng):
    bsz, n_rows, d = h.shape
    hw = M_HPS * HEAD_DIM
    nw = 4 * hw + wg.shape[-1]
    n_real = n_rows - N_META
    return pl.pallas_call(
        _mlstm_kernel,
        grid=(bsz, N_HEADS // M_HPS),
        in_specs=[
            pl.BlockSpec((1, n_rows, d), lambda b, s: (b, 0, 0)),
            pl.BlockSpec((1, d), lambda b, s: (0, 0)),
            *_head_column_specs(d, 4, M_HPS),
            pl.BlockSpec((1, d, wg.shape[-1]), lambda b, s: (s, 0, 0)),
            pl.BlockSpec((1, CONV_K, 2 * hw), lambda b, s: (s, 0, 0)),
            pl.BlockSpec((1, 1, 2 * LANES), lambda b, s: (s, 0, 0)),
            pl.BlockSpec((1, 1, hw), lambda b, s: (s, 0, 0)),
        ],
        out_specs=pl.BlockSpec((1, n_rows, hw), lambda b, s: (b, 0, s)),
        out_shape=jax.ShapeDtypeStruct((bsz, n_rows, N_HEADS * HEAD_DIM), BF16),
        scratch_shapes=[
            pltpu.VMEM((n_rows, d), BF16),
            pltpu.VMEM((n_rows, nw), F32),
            pltpu.VMEM((n_rows + 2 * CONV_PAD, 2 * hw), F32),
            pltpu.VMEM((M_HPS, 2, n_rows, HEAD_DIM), F32),
            pltpu.VMEM((n_rows, LANES), F32),
            pltpu.VMEM((n_rows, LANES), F32),
            pltpu.VMEM((8, LANES), F32),
            pltpu.VMEM((8, n_real), F32),
            pltpu.VMEM((M_HPS, HEAD_DIM, n_real), F32),
            pltpu.VMEM((M_HPS, n_rows, 2 * HEAD_DIM), BF16),
            pltpu.VMEM((d, nw), BF16),
        ],
        compiler_params=_cparams(2),
        name="mlstm_heads",
    )(h, gn, w, w, w, w, wg, conv, gb, ng)


def _ret_chunk(qc, kt, vc, dmat, dq, dk_row, dchunk, r_st):
    qb, vb = qc.astype(BF16), vc.astype(BF16)
    s = _dot(qb, kt.astype(BF16)) * dmat
    out = _dot(s.astype(BF16), vb) + dq * _dot(qb, r_st.astype(BF16))
    r_new = dchunk * r_st + _dot((kt * dk_row).astype(BF16), vb)
    return out, r_new


def _decay_tables(n, lg, forward):
    r = lax.broadcasted_iota(jnp.int32, (n, n), 0)
    c = lax.broadcasted_iota(jnp.int32, (n, n), 1)
    pos_q = lax.broadcasted_iota(jnp.int32, (n, 1), 0).astype(F32)
    pos_k = lax.broadcasted_iota(jnp.int32, (1, n), 1).astype(F32)
    if forward:
        dist, mask = (r - c).astype(F32), c <= r
        dq, dk = jnp.exp((pos_q + 1.0) * lg), jnp.exp((n - 1.0 - pos_k) * lg)
    else:
        dist, mask = (c - r).astype(F32), c >= r
        dq, dk = jnp.exp((n - pos_q) * lg), jnp.exp(pos_k * lg)
    dmat = jnp.exp(jnp.where(mask, dist * lg, NEG))
    return dmat, dq, dk, jnp.exp(n * lg)


def _ret_kernel(h_ref, gn_ref, wq_ref, wk_ref, wv_ref, wg_ref, cos_ref, sin_ref, dl_ref, ng_ref, out_ref,
                un_ref, z_ref, acc_ref, dmat_ref, dq_ref, kt_ref, wcat_ref):
    n_rows = h_ref.shape[1]
    n_real = n_rows - N_META
    t = SCAN_T
    n_chunks = n_real // t
    hw = R_HPS * HEAD_DIM
    heads = range(R_HPS)

    def cols(group, hh):
        return slice(group * hw + hh * HEAD_DIM, group * hw + (hh + 1) * HEAD_DIM)

    @pl.when(pl.program_id(1) == 0)
    def _():
        _norm_to_scratch(h_ref, gn_ref, un_ref)

    _assemble_weights([wq_ref, wk_ref, wv_ref, wg_ref], wcat_ref)
    _project(un_ref, wcat_ref, z_ref)

    for hh in heads:
        for r0, nr in _row_blocks(n_rows):
            cs, sn = cos_ref[pl.ds(r0, nr), :], sin_ref[pl.ds(r0, nr), :]
            q = z_ref[pl.ds(r0, nr), cols(0, hh)]
            z_ref[pl.ds(r0, nr), cols(0, hh)] = q * cs + pltpu.roll(q, HEAD_DIM // 2, 1) * sn
            k = z_ref[pl.ds(r0, nr), cols(1, hh)]
            z_ref[pl.ds(r0, nr), cols(1, hh)] = (
                (k * cs + pltpu.roll(k, HEAD_DIM // 2, 1) * sn) * HEAD_DIM ** -0.5)

    for hh in heads:
        for c in range(n_real // LANES):
            kt_ref[hh, :, c * LANES:(c + 1) * LANES] = z_ref[pl.ds(N_META + c * LANES, LANES), cols(1, hh)].T

    lg_all = _log_sigmoid(dl_ref[0])
    chains = [(hh, d) for hh in heads for d in (0, 1)]
    states, dk_rows, dchunks = [], [], []
    for hh, d in chains:
        lg = lg_all[:, 2 * hh + d:2 * hh + d + 1]
        dmat_m, _, dk_m, dch_m = _decay_tables(N_META, lg, True)
        dmat, dq, dk, dch = _decay_tables(t, lg, d == 0)
        dmat_ref[hh, d] = dmat
        dq_ref[hh, d] = jnp.broadcast_to(dq, (t, LANES))
        dk_rows.append(dk)
        dchunks.append(dch)
        kt_meta = z_ref[pl.ds(0, LANES), cols(1, hh)].T[:, 0:N_META]
        out_m, r_st = _ret_chunk(
            z_ref[pl.ds(0, N_META), cols(0, hh)], kt_meta, z_ref[pl.ds(0, N_META), cols(2, hh)],
            dmat_m, jnp.zeros((N_META, 1), F32), dk_m, dch_m, jnp.zeros((HEAD_DIM, HEAD_DIM), F32))
        acc_ref[hh, d, pl.ds(0, N_META), :] = out_m
        states.append(r_st)

    def body(i, carry):
        new = []
        for idx, (hh, d) in enumerate(chains):
            c = i if d == 0 else n_chunks - 1 - i
            r0 = pl.multiple_of(N_META + c * t, 8)
            l0 = pl.multiple_of(c * t, LANES)
            out, st = _ret_chunk(
                z_ref[pl.ds(r0, t), cols(0, hh)], kt_ref[hh, :, pl.ds(l0, t)], z_ref[pl.ds(r0, t), cols(2, hh)],
                dmat_ref[hh, d], dq_ref[hh, d], dk_rows[idx], dchunks[idx], carry[idx])
            acc_ref[hh, d, pl.ds(r0, t), :] = out
            new.append(st)
        return tuple(new)

    lax.fori_loop(0, n_chunks, body, tuple(states))

    for hh in heads:
        oc = slice(hh * HEAD_DIM, (hh + 1) * HEAD_DIM)
        for r0, nr in _row_blocks(n_rows):
            hs = acc_ref[hh, 0, pl.ds(r0, nr), :] + acc_ref[hh, 1, pl.ds(r0, nr), :]
            g = z_ref[pl.ds(r0, nr), cols(3, hh)]
            out_ref[0, pl.ds(r0, nr), oc] = (g * _sigmoid(g) * _rms(hs, ng_ref[0][:, oc])).astype(BF16)


def _ret_call(h, gn, w, cos2, sin2, dl, ng):
    bsz, n_rows, d = h.shape
    hw = R_HPS * HEAD_DIM
    nw = 4 * hw
    return pl.pallas_call(
        _ret_kernel,
        grid=(bsz, N_HEADS // R_HPS),
        in_specs=[
            pl.BlockSpec((1, n_rows, d), lambda b, s: (b, 0, 0)),
            pl.BlockSpec((1, d), lambda b, s: (0, 0)),
            *_head_column_specs(d, 4, R_HPS),
            pl.BlockSpec((n_rows, HEAD_DIM), lambda b, s: (0, 0)),
            pl.BlockSpec((n_rows, HEAD_DIM), lambda b, s: (0, 0)),
            pl.BlockSpec((1, 1, LANES), lambda b, s: (s, 0, 0)),
            pl.BlockSpec((1, 1, hw), lambda b, s: (s, 0, 0)),
        ],
        out_specs=pl.BlockSpec((1, n_rows, hw), lambda b, s: (b, 0, s)),
        out_shape=jax.ShapeDtypeStruct((bsz, n_rows, N_HEADS * HEAD_DIM), BF16),
        scratch_shapes=[
            pltpu.VMEM((n_rows, d), BF16),
            pltpu.VMEM((n_rows, nw), F32),
            pltpu.VMEM((R_HPS, 2, n_rows, HEAD_DIM), F32),
            pltpu.VMEM((R_HPS, 2, SCAN_T, SCAN_T), F32),
            pltpu.VMEM((R_HPS, 2, SCAN_T, LANES), F32),
            pltpu.VMEM((R_HPS, HEAD_DIM, n_rows - N_META), F32),
            pltpu.VMEM((d, nw), BF16),
        ],
        compiler_params=_cparams(2),
        name="retention_heads",
    )(h, gn, w, w, w, w, cos2, sin2, dl, ng)


def _head_pair_block_diag(qp):
    lane = lax.broadcasted_iota(jnp.int32, qp.shape, 1)
    zero = jnp.zeros_like(qp)
    return jnp.concatenate([jnp.where(lane < NA_HEAD_DIM, qp, zero),
                            jnp.where(lane >= NA_HEAD_DIM, qp, zero)], axis=0)


def _na_attend(qp, keys, values, bias_t):
    n = qp.shape[0]
    bd = _head_pair_block_diag(qp)
    scores = []
    for kb, bt in zip(keys, bias_t):
        s = _dot_nt(kb, bd)
        scores.append(s if bt is None else s + bt)
    m = scores[0].max(axis=0, keepdims=True)
    for s in scores[1:]:
        m = jnp.maximum(m, s.max(axis=0, keepdims=True))
    probs = [jnp.exp(s - m) for s in scores]
    den = probs[0].sum(axis=0, keepdims=True)
    for p in probs[1:]:
        den = den + p.sum(axis=0, keepdims=True)
    inv = 1.0 / den
    o2 = None
    for p, vb in zip(probs, values):
        part = _dot_tn((p * inv).astype(BF16), vb)
        o2 = part if o2 is None else o2 + part
    lane = lax.broadcasted_iota(jnp.int32, (n, 2 * NA_HEAD_DIM), 1)
    return jnp.where(lane < NA_HEAD_DIM, o2[0:n, :], o2[n:2 * n, :])


def _na_kernel(h_ref, gn_ref, wq_ref, wk_ref, wv_ref, qkg_ref, bias_ref, out_ref,
               un_ref, z_ref, q_ref, k_ref, vt_ref, ot_ref, s_ref, p_ref, den_ref, wcat_ref, bias_s_ref):
    n_rows = h_ref.shape[1]
    n_real = n_rows - N_META
    n_grid_rows = n_real // GRID_W
    dh = NA_HEAD_DIM
    pw = 2 * dh
    gw = NA_PAIRS * pw
    band = NA_WIN_ROWS * GRID_W
    pairs = range(NA_PAIRS)
    v_off = 2 * gw

    @pl.when(pl.program_id(1) == 0)
    def _():
        _norm_to_scratch(h_ref, gn_ref, un_ref)

    _assemble_weights([wq_ref, wk_ref, wv_ref], wcat_ref)
    _project(un_ref, wcat_ref, z_ref)

    for p in pairs:
        for dr in range(bias_ref.shape[1]):
            bias_s_ref[p, dr] = jnp.concatenate([bias_ref[2 * p, dr], bias_ref[2 * p + 1, dr]], axis=1)

    r = lax.broadcasted_iota(jnp.int32, (gw, gw), 0)
    c = lax.broadcasted_iota(jnp.int32, (gw, gw), 1)
    head_ones = jnp.where(r // dh == c // dh, 1.0, 0.0).astype(BF16)
    for g, dst_ref in enumerate((q_ref, k_ref)):
        for r0, nr in _row_blocks(n_rows):
            x = z_ref[pl.ds(r0, nr), g * gw:(g + 1) * gw]
            ssq = _dot((x * x).astype(BF16), head_ones)
            y = (x * lax.rsqrt(ssq * (1.0 / dh) + EPS) * qkg_ref[:, g * gw:(g + 1) * gw]).astype(BF16)
            for p in pairs:
                dst_ref[p, pl.ds(r0, nr), :] = y[:, p * pw:(p + 1) * pw]

    n_tblocks = n_real // LANES
    k_meta, vt_meta = [], []
    for p in pairs:
        v_cols = slice(v_off + p * pw, v_off + (p + 1) * pw)
        for cpy in (0, 1):
            for c in range(n_tblocks - cpy):
                vb = z_ref[pl.ds(N_META + cpy * GRID_W + c * LANES, LANES), v_cols]
                vt_ref[p, cpy, :, c * LANES:(c + 1) * LANES] = vb.T.astype(BF16)
        vt_ref[p, 1, :, (n_tblocks - 1) * LANES:n_tblocks * LANES] = jnp.zeros((pw, LANES), BF16)
        vt_meta.append(z_ref[pl.ds(0, LANES), v_cols].T[:, 0:N_META].astype(BF16))
        k_meta.append(k_ref[p, pl.ds(0, N_META), :])
        out_ref[0, pl.ds(0, N_META), p * pw:(p + 1) * pw] = _na_attend(
            q_ref[p, pl.ds(0, N_META), :], [k_meta[p]], [z_ref[pl.ds(0, N_META), v_cols].astype(BF16)],
            [None]).astype(BF16)

    sub = lax.broadcasted_iota(jnp.int32, (pw, pw), 0)
    lane = lax.broadcasted_iota(jnp.int32, (pw, pw), 1)

    def row_start(r):
        return jnp.clip(r - NA_WIN_ROWS // 2, 0, n_grid_rows - NA_WIN_ROWS)

    def stage_scores(r, slot):
        rs = row_start(r)
        q0 = pl.multiple_of(N_META + r * GRID_W, 8)
        k0 = pl.multiple_of(N_META + rs * GRID_W, 8)
        for p in pairs:
            bd = _head_pair_block_diag(q_ref[p, pl.ds(q0, GRID_W), :])
            bias = bias_s_ref[p, pl.ds(NA_WIN_ROWS - 1 - (r - rs), NA_WIN_ROWS)].reshape(band, pw)
            s_ref[p, slot, pl.ds(0, band), :] = _dot_nt(k_ref[p, pl.ds(k0, band), :], bd) + bias
            s_ref[p, slot, pl.ds(band, N_META), :] = _dot_nt(k_meta[p], bd)

    def stage_softmax(slot):
        for p in pairs:
            s = s_ref[p, slot]
            e = jnp.exp(s - s.max(axis=0, keepdims=True))
            den_ref[p, slot] = e.sum(axis=0, keepdims=True)
            p_ref[p, slot] = e.astype(BF16)

    def stage_values(r, slot):
        rs = row_start(r)
        par = rs % 2
        l0 = pl.multiple_of((rs - par) * GRID_W, LANES)
        for p in pairs:
            o_t = (_dot(vt_ref[p, par, :, pl.ds(l0, band)], p_ref[p, slot, pl.ds(0, band), :])
                   + _dot(vt_meta[p], p_ref[p, slot, pl.ds(band, N_META), :]))
            ot_ref[p, r] = o_t / den_ref[p, slot]

    stage_scores(0, 0)
    stage_softmax(0)
    stage_scores(1, 1)

    def pipe_body(j, carry):
        i0 = 2 + 2 * j
        stage_values(i0 - 2, 0)
        stage_softmax(1)
        stage_scores(i0, 0)
        stage_values(i0 - 1, 1)
        stage_softmax(0)
        stage_scores(i0 + 1, 1)
        return carry

    lax.fori_loop(0, (n_grid_rows - 2) // 2, pipe_body, 0, unroll=3)
    stage_values(n_grid_rows - 2, 0)
    stage_softmax(1)
    stage_values(n_grid_rows - 1, 1)

    for p in pairs:
        for c in range(n_tblocks):
            oa, ob = ot_ref[p, 2 * c], ot_ref[p, 2 * c + 1]
            sel_a = jnp.where(sub < dh, oa, pltpu.roll(oa, dh, 1))
            sel_b = jnp.where(sub < dh, pltpu.roll(ob, dh, 1), ob)
            out_ref[0, pl.ds(N_META + c * LANES, LANES), p * pw:(p + 1) * pw] = (
                jnp.where(lane < dh, sel_a, sel_b).T.astype(BF16))


def _na_call(h, gn, w, qg, kg, bias):
    bsz, n_rows, d = h.shape
    pw = 2 * NA_HEAD_DIM
    gw = NA_PAIRS * pw
    n_steps = w.shape[1] // (3 * gw)
    nw = 3 * gw
    band = NA_WIN_ROWS * GRID_W
    n_real = n_rows - N_META
    heads_per_step = 2 * NA_PAIRS
    qkg = jnp.concatenate([jnp.tile(qg, heads_per_step) * NA_HEAD_DIM ** -0.5,
                           jnp.tile(kg, heads_per_step)])[None, :].astype(F32)
    group_spec = lambda g: pl.BlockSpec((d, gw), functools.partial(lambda g, b, s: (0, g * n_steps + s), g))
    return pl.pallas_call(
        _na_kernel,
        grid=(bsz, n_steps),
        in_specs=[
            pl.BlockSpec((1, n_rows, d), lambda b, s: (b, 0, 0)),
            pl.BlockSpec((1, d), lambda b, s: (0, 0)),
            group_spec(0), group_spec(1), group_spec(2),
            pl.BlockSpec((1, 2 * gw), lambda b, s: (0, 0)),
            pl.BlockSpec((heads_per_step,) + bias.shape[1:], lambda b, s: (s, 0, 0, 0)),
        ],
        out_specs=pl.BlockSpec((1, n_rows, NA_PAIRS * pw), lambda b, s: (b, 0, s)),
        out_shape=jax.ShapeDtypeStruct((bsz, n_rows, n_steps * NA_PAIRS * pw), BF16),
        scratch_shapes=[
            pltpu.VMEM((n_rows, d), BF16),
            pltpu.VMEM((n_rows, nw), F32),
            pltpu.VMEM((NA_PAIRS, n_rows, pw), BF16),
            pltpu.VMEM((NA_PAIRS, n_rows, pw), BF16),
            pltpu.VMEM((NA_PAIRS, 2, pw, n_real), BF16),
            pltpu.VMEM((NA_PAIRS, n_real // GRID_W, pw, pw), F32),
            pltpu.VMEM((NA_PAIRS, 2, band + N_META, pw), F32),
            pltpu.VMEM((NA_PAIRS, 2, band + N_META, pw), BF16),
            pltpu.VMEM((NA_PAIRS, 2, 1, pw), F32),
            pltpu.VMEM((d, nw), BF16),
            pltpu.VMEM((NA_PAIRS, bias.shape[1], GRID_W, pw), F32),
        ],
        compiler_params=_cparams(2),
        name="neighbourhood_attention",
    )(h, gn, w, w, w, qkg, bias)


def _lane_prefix_exclusive(x, tri_strict):
    n_blocks = x.shape[1] // LANES
    carry = jnp.zeros((x.shape[0], 1), F32)
    pieces = []
    for j in range(n_blocks):
        blk = x[:, j * LANES:(j + 1) * LANES]
        pieces.append(_dot(blk.astype(BF16), tri_strict) + carry)
        carry = carry + jnp.sum(blk, axis=-1, keepdims=True)
    return jnp.concatenate(pieces, axis=1)


def _aligned_row_blocks(n_rows, step=512):
    blocks, r0 = [], 0
    while r0 < n_rows:
        nr = min(step, n_rows - r0)
        blocks.append((r0, nr))
        r0 += nr
    return blocks


def _outproj_kernel(n_mix, *refs):
    h_ref = refs[0]
    mix_refs = refs[1:1 + n_mix]
    wo_ref, fg_ref, wr_ref, hn_ref, u2_ref, lg_ref = refs[1 + n_mix:7 + n_mix]
    acc = h_ref[0]
    k0 = 0
    for m_ref in mix_refs:
        kw = m_ref.shape[2]
        acc = acc + _dot(m_ref[0], wo_ref[pl.ds(k0, kw), :])
        k0 += kw
    hn_ref[0] = acc
    u = _rms(acc, fg_ref[...])
    u_hi = u.astype(BF16)
    u2_ref[0] = u_hi
    u_lo = (u - u_hi.astype(F32)).astype(BF16)
    hh_hl = _dot(u_hi, wr_ref[...])
    lg_ref[0] = hh_hl[:, 0:LANES] + hh_hl[:, LANES:2 * LANES] + _dot(u_lo, wr_ref[:, 0:LANES])


def _outproj_call(h, mixes, wo, fg, wr):
    bsz, n_rows, d = h.shape
    n_mix = len(mixes)
    nr = _row_blocks(n_rows)[0][1]
    row_spec = lambda w: pl.BlockSpec((1, nr, w), lambda b, r: (b, r, 0))
    return pl.pallas_call(
        functools.partial(_outproj_kernel, n_mix),
        grid=(bsz, n_rows // nr),
        in_specs=[row_spec(d)] + [row_spec(m.shape[2]) for m in mixes] + [
            pl.BlockSpec((d, d), lambda b, r: (0, 0)),
            pl.BlockSpec((1, d), lambda b, r: (0, 0)),
            pl.BlockSpec((d, 2 * LANES), lambda b, r: (0, 0)),
        ],
        out_specs=[row_spec(d), row_spec(d), row_spec(LANES)],
        out_shape=[
            jax.ShapeDtypeStruct((bsz, n_rows, d), F32),
            jax.ShapeDtypeStruct((bsz, n_rows, d), BF16),
            jax.ShapeDtypeStruct((bsz, n_rows, LANES), F32),
        ],
        compiler_params=_cparams(2),
        name="outproj_norm_logits",
    )(h, *mixes, wo, fg, wr)


def _route_kernel(cap, lg_ref, slot_ref, gate_ref, lgc_ref):
    n_rows = lg_ref.shape[1]
    n_pad = slot_ref.shape[2]
    lgc_ref[pl.ds(0, n_rows), :] = lg_ref[0]
    lgc_ref[pl.ds(n_rows, n_pad - n_rows), :] = jnp.zeros((n_pad - n_rows, LANES), F32)

    logits = jnp.concatenate(
        [lgc_ref[pl.ds(c * LANES, LANES), :].T[0:N_EXPERTS, :] for c in range(n_pad // LANES)], axis=1)
    ex = jnp.exp(logits - jnp.max(logits, axis=0, keepdims=True))
    aff = ex / jnp.sum(ex, axis=0, keepdims=True)
    tok = lax.broadcasted_iota(jnp.int32, aff.shape, 1)
    aff = jnp.where(tok < n_rows, aff, -1.0)

    def count_ge(x):
        return jnp.sum(jnp.where(aff >= x, 1.0, 0.0), axis=-1, keepdims=True)

    capf = float(cap)

    def refine(base, cands):
        best = base
        for cand in cands:
            best = jnp.where(count_ge(cand) >= capf, cand, best)
        return best

    tiny = jnp.full((aff.shape[0], 1), 2.0 ** -126, F32)
    ok0 = count_ge(tiny) >= capf
    p = tiny
    p = refine(p, [p * (2.0 ** (16 * k)) for k in range(1, 8)])
    p = refine(p, [p * (2.0 ** (2 * k)) for k in range(1, 8)])
    p = refine(p, [p * 2.0])
    m = p
    for i in range(1, 8):
        step = p * (2.0 ** (-3 * i))
        m = refine(m, [m + k * step for k in range(1, 8)])
    step = p * (2.0 ** -23)
    m = refine(m, [m + k * step for k in range(1, 4)])
    thr = jnp.where(ok0, m, 0.0)

    gt = aff > thr
    eq = aff == thr
    need = capf - jnp.sum(jnp.where(gt, 1.0, 0.0), axis=-1, keepdims=True)
    r = lax.broadcasted_iota(jnp.int32, (LANES, LANES), 0)
    c = lax.broadcasted_iota(jnp.int32, (LANES, LANES), 1)
    tri_strict = jnp.where(r < c, 1.0, 0.0).astype(BF16)
    eq_rank = _lane_prefix_exclusive(jnp.where(eq, 1.0, 0.0), tri_strict)
    sel = gt | (eq & (eq_rank < need))
    pos = _lane_prefix_exclusive(jnp.where(sel, 1.0, 0.0), tri_strict)
    slot_ref[0] = jnp.where(sel, pos, -1.0).astype(jnp.int32)
    gate_ref[0] = jnp.where(sel, aff, 0.0)


def _route_call(lg, cap):
    bsz, n_rows, _ = lg.shape
    n_pad = ((n_rows + LANES - 1) // LANES) * LANES
    return pl.pallas_call(
        functools.partial(_route_kernel, cap),
        grid=(bsz,),
        in_specs=[pl.BlockSpec((1, n_rows, LANES), lambda b: (b, 0, 0))],
        out_specs=[
            pl.BlockSpec((1, N_EXPERTS, n_pad), lambda b: (b, 0, 0)),
            pl.BlockSpec((1, N_EXPERTS, n_pad), lambda b: (b, 0, 0)),
        ],
        out_shape=[
            jax.ShapeDtypeStruct((bsz, N_EXPERTS, n_pad), jnp.int32),
            jax.ShapeDtypeStruct((bsz, N_EXPERTS, n_pad), F32),
        ],
        scratch_shapes=[pltpu.VMEM((n_pad, LANES), F32)],
        compiler_params=_cparams(1),
        name="expert_choice_router",
    )(lg)


def _gather_kernel(u2_ref, slot_ref, xs_ref):
    cap_pad = xs_ref.shape[2]
    n_rows = u2_ref.shape[1]
    srow_id = lax.broadcasted_iota(jnp.int32, (cap_pad, n_rows), 0)

    def body(i, carry):
        pieces = [jnp.where(srow_id == slot_ref[0, pl.ds(i * GATHER_GROUP + j, 1), 0:n_rows],
                            1.0, 0.0).astype(BF16)
                  for j in range(GATHER_GROUP)]
        rows = _dot(jnp.concatenate(pieces, axis=0), u2_ref[0]).astype(BF16)
        for j in range(GATHER_GROUP):
            xs_ref[0, i * GATHER_GROUP + j] = rows[j * cap_pad:(j + 1) * cap_pad, :]
        return carry

    lax.fori_loop(0, N_EXPERTS // GATHER_GROUP, body, 0)


def _gather_call(u2, slot, cap_pad):
    bsz, n_rows, d = u2.shape
    n_pad = slot.shape[2]
    return pl.pallas_call(
        _gather_kernel,
        grid=(bsz,),
        in_specs=[
            pl.BlockSpec((1, n_rows, d), lambda b: (b, 0, 0)),
            pl.BlockSpec((1, N_EXPERTS, n_pad), lambda b: (b, 0, 0)),
        ],
        out_specs=pl.BlockSpec((1, N_EXPERTS, cap_pad, d), lambda b: (b, 0, 0, 0)),
        out_shape=jax.ShapeDtypeStruct((bsz, N_EXPERTS, cap_pad, d), BF16),
        compiler_params=_cparams(1),
        name="expert_gather",
    )(u2, slot)


def _expert_kernel(xs_ref, wg_ref, wu_ref, wd_ref, ys_ref, wgb_ref, wub_ref, wdb_ref):
    bb, _, cap_pad, d = xs_ref.shape

    @pl.when(pl.program_id(1) == 0)
    def _():
        wgb_ref[...] = wg_ref[0].astype(BF16)
        wub_ref[...] = wu_ref[0].astype(BF16)
        wdb_ref[...] = wd_ref[0].astype(BF16)

    x = xs_ref[...].reshape(bb * cap_pad, d)
    g = _dot(x, wgb_ref[...])
    u = _dot(x, wub_ref[...])
    hdn = (g * _sigmoid(g) * u).astype(BF16)
    ys_ref[...] = _dot(hdn, wdb_ref[...]).astype(BF16).reshape(ys_ref.shape)


def _expert_call(xs, wg, wu, wd, layer):
    bsz, n_e, cap_pad, d = xs.shape
    ff = wg.shape[-1]
    seq_blocks = 2 if bsz % 2 == 0 else 1
    bb = bsz // seq_blocks
    return pl.pallas_call(
        _expert_kernel,
        grid=(n_e, seq_blocks),
        in_specs=[
            pl.BlockSpec((bb, 1, cap_pad, d), lambda e, m: (m, e, 0, 0)),
            pl.BlockSpec((1, None, d, ff), lambda e, m: (layer, e, 0, 0)),
            pl.BlockSpec((1, None, d, ff), lambda e, m: (layer, e, 0, 0)),
            pl.BlockSpec((1, None, ff, d), lambda e, m: (layer, e, 0, 0)),
        ],
        out_specs=pl.BlockSpec((bb, 1, cap_pad, d), lambda e, m: (m, e, 0, 0)),
        out_shape=jax.ShapeDtypeStruct((bsz, n_e, cap_pad, d), BF16),
        scratch_shapes=[pltpu.VMEM((d, ff), BF16), pltpu.VMEM((d, ff), BF16), pltpu.VMEM((ff, d), BF16)],
        compiler_params=_cparams(2),
        name="expert_swiglu",
    )(xs, wg, wu, wd)


def _combine_kernel(h_ref, ym_ref, yt_ref, slot_ref, gate_ref, out_ref):
    n_rows = h_ref.shape[1]
    n_out = out_ref.shape[1]
    skip = n_rows - n_out
    _, e_grp, n_main, d = ym_ref.shape
    _, n_e, n_tail, _ = yt_ref.shape
    n_pad = slot_ref.shape[2]
    g = pl.program_id(1)

    @pl.when(g == 0)
    def _():
        out_ref[0] = h_ref[0, pl.ds(skip, n_out), :]

    def weights_for(e, first_slot, n_slots):
        srow_id = lax.broadcasted_iota(jnp.int32, (n_slots, n_pad), 0) + first_slot
        hit = srow_id == slot_ref[0, pl.ds(e, 1), :]
        return jnp.where(hit, gate_ref[0, pl.ds(e, 1), :], 0.0).astype(BF16)

    def accumulate(weights, ys):
        for t0, tn in _aligned_row_blocks(n_pad, 1024):
            lo, hi = max(t0, skip), min(t0 + tn, n_rows)
            if hi <= lo:
                continue
            part = _dot_tn(weights[:, t0:t0 + tn], ys)
            out_ref[0, pl.ds(lo - skip, hi - lo), :] += part[lo - t0:hi - t0, :]

    @pl.when(g == 0)
    def _():
        weights = jnp.concatenate([weights_for(e, n_main, n_tail) for e in range(n_e)], axis=0)
        accumulate(weights, yt_ref[0].reshape(n_e * n_tail, d))

    @pl.when(g > 0)
    def _():
        weights = jnp.concatenate([weights_for((g - 1) * e_grp + j, 0, n_main) for j in range(e_grp)], axis=0)
        accumulate(weights, ym_ref[0].reshape(e_grp * n_main, d))


def _combine_call(h, ys, slot, gate, n_out, e_grp=4):
    bsz, n_rows, d = h.shape
    _, n_e, cap_pad, _ = ys.shape
    n_pad = slot.shape[2]
    n_main_steps = n_e // e_grp
    n_tail = cap_pad - MXU_DEPTH
    return pl.pallas_call(
        _combine_kernel,
        grid=(bsz, n_main_steps + 1),
        in_specs=[
            pl.BlockSpec((1, n_rows, d), lambda b, g: (b, 0, 0)),
            pl.BlockSpec((1, e_grp, MXU_DEPTH, d), lambda b, g: (b, jnp.maximum(g - 1, 0), 0, 0)),
            pl.BlockSpec((1, n_e, n_tail, d), lambda b, g: (b, 0, MXU_DEPTH // n_tail, 0)),
            pl.BlockSpec((1, n_e, n_pad), lambda b, g: (b, 0, 0)),
            pl.BlockSpec((1, n_e, n_pad), lambda b, g: (b, 0, 0)),
        ],
        out_specs=pl.BlockSpec((1, n_out, d), lambda b, g: (b, 0, 0)),
        out_shape=jax.ShapeDtypeStruct((bsz, n_out, d), F32),
        compiler_params=_cparams(2),
        name="expert_combine",
    )(h, ys, ys, slot, gate)


def _even_weights(w_in, conv_w, gate_b, decay_logit):
    mw = N_HEADS * HEAD_DIM
    gates = w_in[:, 4 * mw:4 * mw + 4 * N_HEADS]
    r0 = 4 * mw + 4 * N_HEADS

    d = w_in.shape[0]
    steps, hw = N_HEADS // M_HPS, M_HPS * HEAD_DIM
    used = 2 * M_HPS

    def gate_cols(fw_off, bw_off):
        cols = jnp.stack([gates[:, fw_off:fw_off + N_HEADS], gates[:, bw_off:bw_off + N_HEADS]], axis=-1)
        cols = cols.reshape(d, steps, used).transpose(1, 0, 2)
        return jnp.pad(cols, ((0, 0), (0, 0), (0, LANES - used)))

    def gate_bias(fw_off, bw_off):
        b = jnp.stack([gate_b[fw_off:fw_off + N_HEADS], gate_b[bw_off:bw_off + N_HEADS]], axis=-1)
        return jnp.pad(b.reshape(steps, used), ((0, 0), (0, LANES - used)))

    w_m = w_in[:, 0:4 * mw].astype(BF16)
    w_g = jnp.concatenate([gate_cols(0, 2 * N_HEADS), gate_cols(N_HEADS, 3 * N_HEADS)], axis=-1).astype(BF16)
    gb = jnp.concatenate([gate_bias(0, 2 * N_HEADS), gate_bias(N_HEADS, 3 * N_HEADS)], axis=-1)[:, None, :]
    conv = jnp.concatenate([conv_w[:, :mw].reshape(CONV_K, steps, hw),
                            conv_w[:, mw:].reshape(CONV_K, steps, hw)], axis=-1).transpose(1, 0, 2)
    w_r = w_in[:, r0:r0 + 4 * mw].astype(BF16)
    dl = jnp.pad(decay_logit.T.reshape(N_HEADS // R_HPS, 2 * R_HPS),
                 ((0, 0), (0, LANES - 2 * R_HPS)))[:, None, :]
    return w_m, w_g, gb.astype(F32), conv.astype(F32), w_r, dl.astype(F32)


def _rotary_tables(n_rows):
    half = HEAD_DIM // 2
    inv = ROPE_BASE ** (-jnp.arange(half, dtype=F32) / half)
    ang = jnp.arange(n_rows, dtype=F32)[:, None] * inv[None, :]
    cos, sin = jnp.cos(ang), jnp.sin(ang)
    return jnp.concatenate([cos, cos], axis=-1), jnp.concatenate([-sin, sin], axis=-1)


def _na_bias_table(rpb):
    col = np.arange(GRID_W)
    col_start = np.clip(col - NA_WIN_COLS // 2, 0, GRID_W - NA_WIN_COLS)
    col_in = (col[None, :] >= col_start[:, None]) & (col[None, :] < col_start[:, None] + NA_WIN_COLS)
    dc_idx = np.clip(col[None, :] - col[:, None], -(NA_WIN_COLS - 1), NA_WIN_COLS - 1) + NA_WIN_COLS - 1
    selector = (np.arange(rpb.shape[2])[:, None, None] == dc_idx.T[None]).astype(np.float32)
    rpb_cols = jnp.einsum('hrd,dkq->hrkq', rpb.astype(F32), selector, precision=lax.Precision.HIGHEST)
    return jnp.where(col_in.T[None, None], rpb_cols, NEG)


def _ffn(h, mixes, wo, fg, wr, wg, wu, wd, layer, n_out):
    n_rows = h.shape[1]
    cap = CAP_FACTOR * n_rows // N_EXPERTS
    cap_pad = ((cap + BF16_ROWS - 1) // BF16_ROWS) * BF16_ROWS
    wr_hi = wr.astype(BF16)
    wr_lo = (wr - wr_hi.astype(F32)).astype(BF16)
    pad = ((0, 0), (0, LANES - N_EXPERTS))
    wr_pieces = jnp.concatenate([jnp.pad(wr_hi, pad), jnp.pad(wr_lo, pad)], axis=1)
    hn, u2, lg = _outproj_call(h, mixes, wo.astype(BF16), fg[None, :], wr_pieces)
    slot, gate = _route_call(lg, cap)
    xs = _gather_call(u2, slot, cap_pad)
    ys = _expert_call(xs, wg, wu, wd, layer)
    return _combine_call(hn, ys, slot, gate, n_out)


def kernel(x, meta_tokens, attn_norm_g, ffn_norm_g, even_w_in, even_conv_w, even_gate_b, even_m_norm_g, even_ret_decay_logit, even_r_norm_g, even_w_out, odd_w_in, odd_q_norm_g, odd_k_norm_g, odd_rpb, odd_w_out, router_w, expert_w_gate, expert_w_up, expert_w_down):
    bsz = x.shape[0]
    depth = attn_norm_g.shape[0]
    meta = jnp.broadcast_to(meta_tokens.astype(x.dtype)[None], (bsz,) + meta_tokens.shape)
    h = jnp.concatenate([meta, x], axis=1)
    n_rows = h.shape[1]
    cos2, sin2 = _rotary_tables(n_rows)
    for layer in range(depth):
        j = layer // 2
        gn = attn_norm_g[layer][None, :]
        if layer % 2 == 0:
            w_m, w_g, gb, conv, w_r, dl = _even_weights(even_w_in[j], even_conv_w[j], even_gate_b[j],
                                                        even_ret_decay_logit[j])
            m_out = _mlstm_call(h, gn, w_m, w_g, conv, gb,
                                even_m_norm_g[j].reshape(N_HEADS // M_HPS, 1, M_HPS * HEAD_DIM))
            r_out = _ret_call(h, gn, w_r, cos2, sin2, dl,
                              even_r_norm_g[j].reshape(N_HEADS // R_HPS, 1, R_HPS * HEAD_DIM))
            mixes, wo = [m_out, r_out], even_w_out[j]
        else:
            a_out = _na_call(h, gn, odd_w_in[j].astype(BF16), odd_q_norm_g[j], odd_k_norm_g[j],
                             _na_bias_table(odd_rpb[j]))
            mixes, wo = [a_out], odd_w_out[j]
        n_out = n_rows - N_META if layer == depth - 1 else n_rows
        h = _ffn(h, mixes, wo, ffn_norm_g[layer], router_w[layer],
                 expert_w_gate, expert_w_up, expert_w_down, layer, n_out)
    return h
```
